```python
import math
import jax
import jax.numpy as jnp
from jax import lax
import numpy as np

D_MODEL = 1024
BATCH = 4
SEQ = 4096
DEPTH = 4

GRID_W = 64
CTX_LEN = 256
EPS = 1e-6
F32 = jnp.float32

S5_GROUP = 16
S5_GROUPS = 16
S5_WIDTH = S5_GROUPS * S5_GROUP
S5_STATE = 64

MLA_HEADS = 8
MLA_NOPE = 64
MLA_ROPE = 32
MLA_V = 64
MLA_DH = MLA_NOPE + MLA_ROPE
MLA_Q_RANK = 256
MLA_KV_RANK = 128
MLA_WIDTH = MLA_HEADS * MLA_V
ROPE_BASE = 10000.0
ROPE_FREQS = MLA_ROPE // 4
Q_BLOCK = 128

DN_HEADS = 4
DN_DK = 64
DN_DV = 64
DN_QKV = DN_HEADS * (2 * DN_DK + DN_DV)
DN_WIDTH = DN_HEADS * DN_DV
DN_CONV = 5
DN_CHUNK = 64

D_MIX = S5_WIDTH + MLA_WIDTH + DN_WIDTH
D_FF = 4 * D_MODEL
IN_SIZES = (S5_WIDTH, MLA_Q_RANK, MLA_KV_RANK, MLA_ROPE, DN_QKV, DN_WIDTH, 4 * DN_HEADS)
IN_WIDTH = sum(IN_SIZES)

kernel_name = "hybrid_s5_mla_gdn_dit_trunk"


def _rms(x, w):
    xf = x.astype(F32)
    y = xf * lax.rsqrt(jnp.mean(xf * xf, axis=-1, keepdims=True) + EPS)
    return y.astype(x.dtype) * w


def _l2norm(x):
    xf = x.astype(F32)
    return xf * lax.rsqrt(jnp.sum(xf * xf, axis=-1, keepdims=True) + EPS)


def _modulate(x, w, shift, scale):
    return _rms(x, w) * (1.0 + scale) + shift


def _split_in(z):
    parts, start = [], 0
    for size in IN_SIZES:
        parts.append(z[..., start:start + size])
        start += size
    return parts


def _cmul(ar, ai, br, bi):
    return ar * br - ai * bi, ar * bi + ai * br


def _s5_discretize(lam_re, lam_im, log_dt, b_re, b_im):
    dt = jnp.exp(log_dt.astype(F32))[:, None]
    lr, li = lam_re.astype(F32), lam_im.astype(F32)
    mag = jnp.exp(lr * dt)
    a_re, a_im = mag * jnp.cos(li * dt), mag * jnp.sin(li * dt)
    den = lr * lr + li * li
    f_re = ((a_re - 1.0) * lr + a_im * li) / den
    f_im = (a_im * lr - (a_re - 1.0) * li) / den
    bb_re, bb_im = _cmul(f_re[..., None], f_im[..., None], b_re.astype(F32), b_im.astype(F32))
    return a_re, a_im, bb_re, bb_im


def _s5_combine(e1, e2):
    a1r, a1i, h1r, h1i = e1
    a2r, a2i, h2r, h2i = e2
    ar, ai = _cmul(a2r, a2i, a1r, a1i)
    hr, hi = _cmul(a2r, a2i, h1r, h1i)
    return ar, ai, hr + h2r, hi + h2i


def _s5_scan(u, a_re, a_im, bb_re, bb_im, c_re, c_im, h0_re, h0_im, reverse):
    bu_re = jnp.einsum('blgh,gph->blgp', u, bb_re)
    bu_im = jnp.einsum('blgh,gph->blgp', u, bb_im)
    pr, pi, hr, hi = lax.associative_scan(
        _s5_combine,
        (jnp.broadcast_to(a_re, bu_re.shape), jnp.broadcast_to(a_im, bu_re.shape), bu_re, bu_im),
        reverse=reverse, axis=1)
    cr, ci = _cmul(pr, pi, h0_re[:, None], h0_im[:, None])
    hr, hi = hr + cr, hi + ci
    y = (jnp.einsum('ghp,blgp->blgh', c_re.astype(F32), hr)
         - jnp.einsum('ghp,blgp->blgh', c_im.astype(F32), hi))
    end = 0 if reverse else -1
    return y, hr[:, end], hi[:, end]


def _s5_mixer(u_x, u_c, lam_re, lam_im, log_dt, b_re, b_im, c_re, c_im, d_skip, glu_w, ctx_out):
    def groups(u):
        return u.astype(F32).reshape(u.shape[0], u.shape[1], S5_GROUPS, S5_GROUP)

    ux, uc = groups(u_x), groups(u_c)
    d = d_skip.astype(F32).reshape(S5_GROUPS, S5_GROUP)
    y_x, y_c = d * ux, d * uc
    zeros = jnp.zeros((u_c.shape[0], S5_GROUPS, S5_STATE), F32)
    for direction in range(2):
        rev = direction == 1
        a_re, a_im, bb_re, bb_im = _s5_discretize(
            lam_re[direction], lam_im[direction], log_dt[direction], b_re[direction], b_im[direction])
        yc, hc_re, hc_im = _s5_scan(uc, a_re, a_im, bb_re, bb_im, c_re[direction], c_im[direction],
                                    zeros, zeros, rev)
        yx, _, _ = _s5_scan(ux, a_re, a_im, bb_re, bb_im, c_re[direction], c_im[direction],
                            hc_re, hc_im, rev)
        y_x, y_c = y_x + yx, y_c + yc

    def glu(y, like):
        z = jax.nn.gelu(y.reshape(like.shape[0], like.shape[1], S5_WIDTH)).astype(like.dtype) @ glu_w
        return z[..., :S5_WIDTH] * jax.nn.sigmoid(z[..., S5_WIDTH:])

    return glu(y_x, u_x), (glu(y_c, u_c) if ctx_out else None)


def _axial_rope_tables(seq_len):
    rows = seq_len // GRID_W
    row = jnp.repeat(jnp.arange(rows, dtype=F32), GRID_W)
    col = jnp.tile(jnp.arange(GRID_W, dtype=F32), rows)
    inv = ROPE_BASE ** (-jnp.arange(ROPE_FREQS, dtype=F32) / ROPE_FREQS)
    ang = jnp.concatenate([row[:, None] * inv, col[:, None] * inv], axis=-1)
    return jnp.cos(ang)[None, :, None, :], jnp.sin(ang)[None, :, None, :]


def _rope(x, cos, sin):
    half = MLA_ROPE // 2
    xr = x[..., MLA_NOPE:].astype(F32)
    r1, r2 = xr[..., :half], xr[..., half:]
    rot = jnp.concatenate([r1 * cos - r2 * sin, r1 * sin + r2 * cos], axis=-1).astype(x.dtype)
    return jnp.concatenate([x[..., :MLA_NOPE], rot], axis=-1)


def _mla_project(cq, ckv, k_rope, q_norm_w, kv_norm_w, w_uq, w_ukv, q_gain, k_gain):
    b, l = cq.shape[:2]
    q = (_rms(cq, q_norm_w) @ w_uq).reshape(b, l, MLA_HEADS, MLA_DH)
    kv = (_rms(ckv, kv_norm_w) @ w_ukv).reshape(b, l, MLA_HEADS, MLA_NOPE + MLA_V)
    k_r = jnp.broadcast_to(k_rope[:, :, None, :], (b, l, MLA_HEADS, MLA_ROPE))
    k = jnp.concatenate([kv[..., :MLA_NOPE], k_r], axis=-1)
    return _rms(q, q_gain), _rms(k, k_gain), kv[..., MLA_NOPE:]


def _softmax_attend(q, k, v):
    s = jnp.einsum('bqhd,bkhd->bhqk', q, k).astype(F32) * (MLA_DH ** -0.5)
    p = jax.nn.softmax(s, axis=-1).astype(v.dtype)
    return jnp.einsum('bhqk,bkhd->bqhd', p, v)


def _blocked_attention(q, k, v):
    b, l = q.shape[:2]
    nb = l // Q_BLOCK
    qb = jnp.moveaxis(q.reshape(b, nb, Q_BLOCK, MLA_HEADS, MLA_DH), 1, 0)
    o = lax.map(lambda blk: _softmax_attend(blk, k, v), qb)
    return jnp.moveaxis(o, 0, 1).reshape(b, l, MLA_WIDTH)


def _mla_mixer(cq_x, ckv_x, kr_x, cq_c, ckv_c, kr_c, q_norm_w, kv_norm_w, w_uq, w_ukv,
               q_gain, k_gain, cos, sin, ctx_out):
    qx, kx, vx = _mla_project(cq_x, ckv_x, kr_x, q_norm_w, kv_norm_w, w_uq, w_ukv, q_gain, k_gain)
    qx, kx = _rope(qx, cos, sin), _rope(kx, cos, sin)
    qc, kc, vc = _mla_project(cq_c, ckv_c, kr_c, q_norm_w, kv_norm_w, w_uq, w_ukv, q_gain, k_gain)
    k_all = jnp.concatenate([kx, kc], axis=1)
    v_all = jnp.concatenate([vx, vc], axis=1)
    o_x = _blocked_attention(qx, k_all, v_all)
    o_c = _softmax_attend(qc, kc, vc).reshape(qc.shape[0], qc.shape[1], MLA_WIDTH) if ctx_out else None
    return o_x, o_c


def _short_conv(x, w):
    pad = DN_CONV // 2
    return lax.conv_general_dilated(
        x, w[:, None, :].astype(x.dtype), window_strides=(1,), padding=[(pad, pad)],
        dimension_numbers=("NWC", "WIO", "NWC"), feature_group_count=x.shape[-1])


def _gated_delta_chunked(q, k, v, g, beta, s0):
    b, l, h, dk = q.shape
    n = l // DN_CHUNK

    def to_chunks(t):
        return jnp.moveaxis(t.astype(F32).reshape(b, n, DN_CHUNK, h, -1), 3, 1)

    qc = to_chunks(q) * (dk ** -0.5)
    kc, vc = to_chunks(k), to_chunks(v)
    gcum = jnp.cumsum(to_chunks(g[..., None])[..., 0], axis=-1)
    bc = to_chunks(beta[..., None])[..., 0]
    idx = jnp.arange(DN_CHUNK)
    incl = idx[:, None] >= idx[None, :]
    strict = idx[:, None] > idx[None, :]
    decay = jnp.exp(jnp.where(incl, gcum[..., :, None] - gcum[..., None, :], -jnp.inf))
    kb = kc * bc[..., None]
    a_mat = jnp.where(strict, jnp.einsum('bhncd,bhnsd->bhncs', kb, kc) * decay, 0.0)
    eye = jnp.eye(DN_CHUNK, dtype=F32)
    t_mat = lax.linalg.triangular_solve(a_mat + eye, jnp.broadcast_to(eye, a_mat.shape),
                                        left_side=True, lower=True)
    w_v = t_mat @ (vc * bc[..., None])
    w_k = t_mat @ (kb * jnp.exp(gcum)[..., None])
    attn = jnp.where(incl, jnp.einsum('bhncd,bhnsd->bhncs', qc, kc) * decay, 0.0)
    q_dec = qc * jnp.exp(gcum)[..., None]
    g_last = gcum[..., -1]
    k_dec = kc * jnp.exp(g_last[..., None] - gcum)[..., None]

    def step(state, xs):
        wv, wk, at, qd, kd, gl = xs
        v_new = wv - wk @ state
        o = qd @ state + at @ v_new
        state = state * jnp.exp(gl)[..., None, None] + jnp.swapaxes(kd, -1, -2) @ v_new
        return state, o

    xs = tuple(jnp.moveaxis(t, 2, 0) for t in (w_v, w_k, attn, q_dec, k_dec, g_last))
    s_fin, o = lax.scan(step, s0.astype(F32), xs)
    o = jnp.moveaxis(o, 0, 2).transpose(0, 2, 3, 1, 4).reshape(b, l, h, v.shape[-1])
    return o, s_fin


def _dn_mixer(qkv_x, qkv_c, gt_x, gt_c, ba_x, ba_c, conv_w, a_log, dt_bias, norm_w, ctx_out):
    def prep(qkv, ba):
        b, l = qkv.shape[:2]
        y = jax.nn.silu(_short_conv(qkv, conv_w))
        qk_w = DN_HEADS * DN_DK
        q = _l2norm(y[..., :qk_w].reshape(b, l, DN_HEADS, DN_DK))
        k = _l2norm(y[..., qk_w:2 * qk_w].reshape(b, l, DN_HEADS, DN_DK))
        v = y[..., 2 * qk_w:].reshape(b, l, DN_HEADS, DN_DV)
        ba = ba.astype(F32).reshape(b, l, 2, 2, DN_HEADS)
        beta = jax.nn.sigmoid(ba[:, :, 0])
        g = -jnp.exp(a_log.astype(F32)) * jax.nn.softplus(ba[:, :, 1] + dt_bias.astype(F32))
        return q, k, v, g, beta

    qx, kx, vx, gx, bx = prep(qkv_x, ba_x)
    qc, kc, vc, gc, bc = prep(qkv_c, ba_c)
    s0 = jnp.zeros((qkv_c.shape[0], DN_HEADS, DN_DK, DN_DV), F32)
    o_x, o_c = 0.0, 0.0
    for d in range(2):
        flip = (lambda t: jnp.flip(t, axis=1)) if d == 1 else (lambda t: t)
        oc, s_ctx = _gated_delta_chunked(flip(qc), flip(kc), flip(vc), flip(gc[:, :, d]), flip(bc[:, :, d]), s0)
        ox, _ = _gated_delta_chunked(flip(qx), flip(kx), flip(vx), flip(gx[:, :, d]), flip(bx[:, :, d]), s_ctx)
        o_x, o_c = o_x + flip(ox), o_c + flip(oc)

    def out(o, gt):
        b, l = gt.shape[:2]
        gate = jax.nn.silu(gt.reshape(b, l, DN_HEADS, DN_DV).astype(F32))
        return (_rms(o, norm_w) * gate).reshape(b, l, DN_WIDTH).astype(gt.dtype)

    return out(o_x, gt_x), (out(o_c, gt_c) if ctx_out else None)


def _sq_relu_mlp(h, w1, w2):
    return jnp.square(jax.nn.relu(h @ w1)) @ w2


def setup_inputs(seed: int = 0) -> dict:
    key = jax.random.key(seed)
    ks = iter(jax.random.split(key, 40))

    def nrm(shape, scale):
        return scale * jax.random.normal(next(ks), shape, F32)

    def gain(shape):
        return 1.0 + 0.02 * jax.random.normal(next(ks), shape, F32)

    dirs = (DEPTH, 2)
    state_idx = jnp.arange(S5_STATE, dtype=F32)
    dn_dt = jnp.exp(jax.random.uniform(next(ks), dirs + (DN_HEADS,), F32, math.log(1e-3), math.log(1e-1)))
    return {
        "x": nrm((BATCH, SEQ, D_MODEL), 1.0),
        "c": nrm((BATCH, D_MODEL), 1.0),
        "ctx": nrm((BATCH, CTX_LEN, D_MODEL), 1.0),
        "c_ctx": nrm((D_MODEL,), 1.0),
        "w_ada": nrm((DEPTH, D_MODEL, 6 * D_MODEL), 0.5 * D_MODEL ** -0.5),
        "b_ada": nrm((DEPTH, 6 * D_MODEL), 0.02),
        "norm1_w": gain((DEPTH, D_MODEL)),
        "w_in": nrm((DEPTH, D_MODEL, IN_WIDTH), D_MODEL ** -0.5),
        "s5_lambda_re": -0.5 + nrm(dirs + (S5_GROUPS, S5_STATE), 0.01),
        "s5_lambda_im": math.pi * state_idx + nrm(dirs + (S5_GROUPS, S5_STATE), 0.01),
        "s5_log_dt": jax.random.uniform(next(ks), dirs + (S5_GROUPS,), F32, math.log(1e-3), math.log(1e-1)),
        "s5_b_re": nrm(dirs + (S5_GROUPS, S5_STATE, S5_GROUP), S5_GROUP ** -0.5),
        "s5_b_im": nrm(dirs + (S5_GROUPS, S5_STATE, S5_GROUP), S5_GROUP ** -0.5),
        "s5_c_re": nrm(dirs + (S5_GROUPS, S5_GROUP, S5_STATE), S5_STATE ** -0.5),
        "s5_c_im": nrm(dirs + (S5_GROUPS, S5_GROUP, S5_STATE), S5_STATE ** -0.5),
        "s5_d": nrm((DEPTH, S5_WIDTH), 1.0),
        "s5_glu_w": nrm((DEPTH, S5_WIDTH, 2 * S5_WIDTH), S5_WIDTH ** -0.5),
        "mla_q_norm_w": gain((DEPTH, MLA_Q_RANK)),
        "mla_kv_norm_w": gain((DEPTH, MLA_KV_RANK)),
        "mla_w_uq": nrm((DEPTH, MLA_Q_RANK, MLA_HEADS * MLA_DH), MLA_Q_RANK ** -0.5),
        "mla_w_ukv": nrm((DEPTH, MLA_KV_RANK, MLA_HEADS * (MLA_NOPE + MLA_V)), MLA_KV_RANK ** -0.5),
        "mla_q_gain": gain((DEPTH, MLA_DH)),
        "mla_k_gain": gain((DEPTH, MLA_DH)),
        "dn_conv_w": nrm((DEPTH, DN_CONV, DN_QKV), DN_CONV ** -0.5),
        "dn_a_log": jnp.log(jax.random.uniform(next(ks), dirs + (DN_HEADS,), F32, 1.0, 16.0)),
        "dn_dt_bias": dn_dt + jnp.log(-jnp.expm1(-dn_dt)),
        "dn_norm_w": gain((DEPTH, DN_DV)),
        "w_out": nrm((DEPTH, D_MIX, D_MODEL), D_MIX ** -0.5),
        "norm2_w": gain((DEPTH, D_MODEL)),
        "w_ff1": nrm((DEPTH, D_MODEL, D_FF), D_MODEL ** -0.5),
        "w_ff2": nrm((DEPTH, D_FF, D_MODEL), D_FF ** -0.5),
    }


def reference(x, c, ctx, c_ctx, w_ada, b_ada, norm1_w, w_in, s5_lambda_re, s5_lambda_im, s5_log_dt,
              s5_b_re, s5_b_im, s5_c_re, s5_c_im, s5_d, s5_glu_w, mla_q_norm_w, mla_kv_norm_w,
              mla_w_uq, mla_w_ukv, mla_q_gain, mla_k_gain, dn_conv_w, dn_a_log, dn_dt_bias, dn_norm_w,
              w_out, norm2_w, w_ff1, w_ff2):
    cos, sin = _axial_rope_tables(x.shape[1])
    cx = ctx
    sc = jax.nn.silu(c)
    scc = jax.nn.silu(c_ctx)[None]
    for l in range(DEPTH):
        ctx_out = l < DEPTH - 1
        mx = (sc @ w_ada[l] + b_ada[l]).reshape(sc.shape[0], 6, D_MODEL)
        mc = (scc @ w_ada[l] + b_ada[l]).reshape(1, 6, D_MODEL)
        zx = _modulate(x, norm1_w[l], mx[:, 0, None], mx[:, 1, None]) @ w_in[l]
        zc = _modulate(cx, norm1_w[l], mc[:, 0, None], mc[:, 1, None]) @ w_in[l]
        u_x, cq_x, ckv_x, kr_x, qkv_x, gt_x, ba_x = _split_in(zx)
        u_c, cq_c, ckv_c, kr_c, qkv_c, gt_c, ba_c = _split_in(zc)
        s5_x, s5_c = _s5_mixer(u_x, u_c, s5_lambda_re[l], s5_lambda_im[l], s5_log_dt[l], s5_b_re[l],
                               s5_b_im[l], s5_c_re[l], s5_c_im[l], s5_d[l], s5_glu_w[l], ctx_out)
        mla_x, mla_c = _mla_mixer(cq_x, ckv_x, kr_x, cq_c, ckv_c, kr_c, mla_q_norm_w[l], mla_kv_norm_w[l],
                                  mla_w_uq[l], mla_w_ukv[l], mla_q_gain[l], mla_k_gain[l], cos, sin, ctx_out)
        dn_x, dn_c = _dn_mixer(qkv_x, qkv_c, gt_x, gt_c, ba_x, ba_c, dn_conv_w[l], dn_a_log[l],
                               dn_dt_bias[l], dn_norm_w[l], ctx_out)
        x = x + mx[:, 2, None] * (jnp.concatenate([s5_x, mla_x, dn_x], axis=-1) @ w_out[l])
        x = x + mx[:, 5, None] * _sq_relu_mlp(
            _modulate(x, norm2_w[l], mx[:, 3, None], mx[:, 4, None]), w_ff1[l], w_ff2[l])
        if ctx_out:
            cx = cx + mc[:, 2, None] * (jnp.concatenate([s5_c, mla_c, dn_c], axis=-1) @ w_out[l])
            cx = cx + mc[:, 5, None] * _sq_relu_mlp(
                _modulate(cx, norm2_w[l], mc[:, 3, None], mc[:, 4, None]), w_ff1[l], w_ff2[l])
    return x
```

```python
import functools
import math

import jax
import jax.numpy as jnp
from jax import lax
from jax.experimental import pallas as pl
from jax.experimental.pallas import tpu as pltpu

F32 = jnp.float32
BF16 = jnp.bfloat16

D_MODEL = 1024
BATCH = 4
SEQ = 4096
DEPTH = 4
GRID_W = 64
CTX_LEN = 256
EPS = 1e-6

S5_GROUP = 16
S5_GROUPS = 16
S5_WIDTH = 256
S5_STATE = 64

MLA_HEADS = 8
MLA_NOPE = 64
MLA_ROPE = 32
MLA_V = 64
MLA_DH = 96
MLA_Q_RANK = 256
MLA_KV_RANK = 128
MLA_WIDTH = 512
ROPE_BASE = 10000.0
ROPE_FREQS = 8

DN_HEADS = 4
DN_DK = 64
DN_DV = 64
DN_QKV = 768
DN_WIDTH = 256
DN_CONV = 5
DN_CHUNK = 64

D_FF = 4 * D_MODEL
IN_SIZES = (S5_WIDTH, MLA_Q_RANK, MLA_KV_RANK, MLA_ROPE, DN_QKV, DN_WIDTH, 4 * DN_HEADS)

ROWS_X = BATCH * SEQ
ROWS_C = BATCH * CTX_LEN
ROWS = ROWS_X + ROWS_C
LANES = 128
HEAD_SLOT = LANES
KEY_BLK = 256
N_XBLK = SEQ // KEY_BLK
CTX_BLK0 = ROWS_X // KEY_BLK

IN_PAD = 1920
COL_U, COL_CQ, COL_CKV, COL_KR, COL_QKV, COL_GT, COL_BA = 0, 256, 512, 640, 768, 1536, 1792
KR_LANE = MLA_NOPE

TM = 512
TF = 1024
TQ = 256
VMEM_LIMIT = 56 * 1024 * 1024

LOG2E = 1.4426950408889634


def _cparams(sem):
    return pltpu.CompilerParams(dimension_semantics=sem, vmem_limit_bytes=VMEM_LIMIT)


def _split3_dot(a, w):
    a_hi = a.astype(BF16)
    a_lo = (a - a_hi.astype(F32)).astype(BF16)
    w_hi = w.astype(BF16)
    w_lo = (w - w_hi.astype(F32)).astype(BF16)
    out = jnp.dot(a_hi, w_hi, preferred_element_type=F32)
    out += jnp.dot(a_lo, w_hi, preferred_element_type=F32)
    out += jnp.dot(a_hi, w_lo, preferred_element_type=F32)
    return out


def _rms_rows(x, w):
    return x * lax.rsqrt(jnp.mean(x * x, axis=-1, keepdims=True) + EPS) * w


def _ada_kernel(c_ref, w_ref, b_ref, o_ref):
    c = c_ref[...]
    s = c * jax.nn.sigmoid(c)
    o_ref[...] = _split3_dot(s, w_ref[...]) + b_ref[...]


def _ada_all(c_pad, w_ada, b_ada):
    tn = 1536
    return pl.pallas_call(
        _ada_kernel,
        out_shape=jax.ShapeDtypeStruct((DEPTH, 8, 6 * D_MODEL), F32),
        grid=(DEPTH, 6 * D_MODEL // tn),
        in_specs=[
            pl.BlockSpec((8, D_MODEL), lambda l, j: (0, 0)),
            pl.BlockSpec((None, D_MODEL, tn), lambda l, j: (l, 0, j)),
            pl.BlockSpec((None, 1, tn), lambda l, j: (l, 0, j)),
        ],
        out_specs=pl.BlockSpec((None, 8, tn), lambda l, j: (l, 0, j)),
        compiler_params=_cparams(("arbitrary", "arbitrary")),
        name="ada_mod",
    )(c_pad, w_ada, b_ada.reshape(DEPTH, 1, 6 * D_MODEL))


def _mod_index(i):
    return jnp.minimum(i // (SEQ // TM), BATCH)


def _in_proj_kernel(h_ref, nw_ref, sh_ref, sc_ref, w_ref, u_ref, mla_ref, qkv_ref, gt_ref, ba_ref):
    y = _rms_rows(h_ref[...], nw_ref[...]) * (1.0 + sc_ref[...]) + sh_ref[...]
    z = jnp.dot(y.astype(BF16), w_ref[...], preferred_element_type=F32)
    u_ref[...] = z[:, COL_U:COL_CQ]
    mla_ref[...] = z[:, COL_CQ:COL_QKV]
    qkv_ref[...] = z[:, COL_QKV:COL_GT]
    gt_ref[...] = z[:, COL_GT:COL_BA]
    ba_ref[...] = z[:, COL_BA:IN_PAD]


def _in_proj(h, nw, mods, w_pad):
    widths = (256, 512, 768, 256, 128)
    mod_spec = lambda j: pl.BlockSpec((None, None, 1, D_MODEL), lambda i: (_mod_index(i), j, 0, 0))
    return pl.pallas_call(
        _in_proj_kernel,
        out_shape=[jax.ShapeDtypeStruct((ROWS, n), F32) for n in widths],
        grid=(ROWS // TM,),
        in_specs=[
            pl.BlockSpec((TM, D_MODEL), lambda i: (i, 0)),
            pl.BlockSpec((1, D_MODEL), lambda i: (0, 0)),
            mod_spec(0),
            mod_spec(1),
            pl.BlockSpec((D_MODEL, IN_PAD), lambda i: (0, 0)),
        ],
        out_specs=[pl.BlockSpec((TM, n), lambda i: (i, 0)) for n in widths],
        compiler_params=_cparams(("arbitrary",)),
        name="in_proj",
    )(h, nw, mods, mods, w_pad)


def _mla_proj_kernel(m_ref, cos_ref, sin_ref, qnw_ref, kvnw_ref, wq_ref, wk_ref, wvt_ref, qg_ref, kg_ref,
                     q_ref, k_ref, vt_ref):
    m = m_ref[...]
    cqn = _rms_rows(m[:, 0:MLA_Q_RANK], qnw_ref[...]).astype(BF16)
    ckvn = _rms_rows(m[:, MLA_Q_RANK:MLA_Q_RANK + MLA_KV_RANK], kvnw_ref[...]).astype(BF16)
    kr = m[:, MLA_Q_RANK + MLA_KV_RANK:]
    qf = jnp.dot(cqn, wq_ref[...], preferred_element_type=F32)
    kf = jnp.dot(ckvn, wk_ref[...], preferred_element_type=F32)
    vt = lax.dot_general(wvt_ref[...], ckvn, (((1,), (1,)), ((), ())), preferred_element_type=F32)
    cos = cos_ref[...]
    sin = sin_ref[...]
    lane = lax.broadcasted_iota(jnp.int32, (1, HEAD_SLOT), 1)
    first_half = lane < (MLA_NOPE + MLA_ROPE // 2)

    def head_norm_rope(xh, gain):
        ss = jnp.sum(xh * xh, axis=-1, keepdims=True) * (1.0 / MLA_DH)
        xh = xh * lax.rsqrt(ss + EPS) * gain
        swapped = jnp.where(first_half, pltpu.roll(xh, HEAD_SLOT - MLA_ROPE // 2, 1), pltpu.roll(xh, MLA_ROPE // 2, 1))
        return xh * cos + swapped * sin

    n_sub = m.shape[0] // KEY_BLK
    for h in range(MLA_HEADS):
        sl = slice(h * HEAD_SLOT, (h + 1) * HEAD_SLOT)
        qh = head_norm_rope(qf[:, sl], qg_ref[...]) * (MLA_DH ** -0.5 * LOG2E)
        q_ref[h] = qh.astype(BF16)
        kh = head_norm_rope(kf[:, sl] + kr, kg_ref[...]).astype(BF16)
        vh = vt[h * MLA_V:(h + 1) * MLA_V, :].astype(BF16)
        for s in range(n_sub):
            k_ref[h, s] = kh[s * KEY_BLK:(s + 1) * KEY_BLK, :]
            vt_ref[h, s] = vh[:, s * KEY_BLK:(s + 1) * KEY_BLK]


def _mla_proj(mla_in, cos_t, sin_t, qnw, kvnw, wq, wk, wvt, qg, kg):
    n_sub = TM // KEY_BLK
    full = lambda a: pl.BlockSpec(a.shape, lambda i: (0,) * a.ndim)
    return pl.pallas_call(
        _mla_proj_kernel,
        out_shape=[
            jax.ShapeDtypeStruct((MLA_HEADS, ROWS, HEAD_SLOT), BF16),
            jax.ShapeDtypeStruct((MLA_HEADS, ROWS // KEY_BLK, KEY_BLK, HEAD_SLOT), BF16),
            jax.ShapeDtypeStruct((MLA_HEADS, ROWS // KEY_BLK, MLA_V, KEY_BLK), BF16),
        ],
        grid=(ROWS // TM,),
        in_specs=[
            pl.BlockSpec((TM, 512), lambda i: (i, 0)),
            pl.BlockSpec((TM, HEAD_SLOT), lambda i: (i, 0)),
            pl.BlockSpec((TM, HEAD_SLOT), lambda i: (i, 0)),
            full(qnw), full(kvnw), full(wq), full(wk), full(wvt), full(qg), full(kg),
        ],
        out_specs=[
            pl.BlockSpec((MLA_HEADS, TM, HEAD_SLOT), lambda i: (0, i, 0)),
            pl.BlockSpec((MLA_HEADS, n_sub, KEY_BLK, HEAD_SLOT), lambda i: (0, i, 0, 0)),
            pl.BlockSpec((MLA_HEADS, n_sub, MLA_V, KEY_BLK), lambda i: (0, i, 0, 0)),
        ],
        compiler_params=_cparams(("arbitrary",)),
        name="mla_proj",
    )(mla_in, cos_t, sin_t, qnw, kvnw, wq, wk, wvt, qg, kg)


def _flash_kernel(q_ref, kx_ref, kc_ref, vtx_ref, vtc_ref, o_ref, *, n_qx):
    qi = pl.program_id(2)
    q = q_ref[...]

    def block(carry, k, vt):
        m, l, acc = carry
        s = lax.dot_general(k, q, (((1,), (1,)), ((), ())), preferred_element_type=F32)
        m_new = jnp.maximum(m, jnp.max(s, axis=0, keepdims=True))
        p = jnp.exp2(s - m_new)
        alpha = jnp.exp2(m - m_new)
        l = alpha * l + jnp.sum(p, axis=0, keepdims=True)
        acc = alpha * acc + jnp.dot(vt, p.astype(BF16), preferred_element_type=F32)
        return m_new, l, acc

    init = (jnp.full((1, TQ), -jnp.inf, F32), jnp.zeros((1, TQ), F32), jnp.zeros((MLA_V, TQ), F32))
    carry = block(init, kc_ref[0], vtc_ref[0])
    n_x = jnp.where(qi < n_qx, N_XBLK, 0)
    carry = lax.fori_loop(0, n_x, lambda j, c: block(c, kx_ref[j], vtx_ref[j]), carry)
    m, l, acc = carry
    o_ref[...] = (acc * (1.0 / l)).astype(BF16)


def _flash(q, k4, vt4, ctx_out):
    n_qx = SEQ // TQ
    n_q = n_qx + (1 if ctx_out else 0)
    ctx_tile0 = ROWS_X // TQ

    def q_idx(b, h, qi):
        return jnp.where(qi < n_qx, b * n_qx + qi, ctx_tile0 + b)

    return pl.pallas_call(
        functools.partial(_flash_kernel, n_qx=n_qx),
        out_shape=jax.ShapeDtypeStruct((MLA_WIDTH, ROWS), BF16),
        grid=(BATCH, MLA_HEADS, n_q),
        in_specs=[
            pl.BlockSpec((None, TQ, HEAD_SLOT), lambda b, h, qi: (h, q_idx(b, h, qi), 0)),
            pl.BlockSpec((None, N_XBLK, KEY_BLK, HEAD_SLOT), lambda b, h, qi: (h, b, 0, 0)),
            pl.BlockSpec((None, 1, KEY_BLK, HEAD_SLOT), lambda b, h, qi: (h, CTX_BLK0 + b, 0, 0)),
            pl.BlockSpec((None, N_XBLK, MLA_V, KEY_BLK), lambda b, h, qi: (h, b, 0, 0)),
            pl.BlockSpec((None, 1, MLA_V, KEY_BLK), lambda b, h, qi: (h, CTX_BLK0 + b, 0, 0)),
        ],
        out_specs=pl.BlockSpec((MLA_V, TQ), lambda b, h, qi: (h, q_idx(b, h, qi))),
        compiler_params=_cparams(("arbitrary", "arbitrary", "arbitrary")),
        name="mla_flash",
    )(q, k4, k4, vt4, vt4)


def _out_mlp_kernel(h_ref, s5_ref, ot_ref, dn_ref, wo_s5_ref, wo_mla_ref, wo_dn_ref,
                    g2_ref, sh_ref, sc_ref, g5_ref, nw_ref, w1_ref, w2_ref, o_ref,
                    x1_ref, xn_ref, acc_ref):
    k = pl.program_id(1)

    @pl.when(k == 0)
    def _():
        mix = jnp.dot(s5_ref[...], wo_s5_ref[...], preferred_element_type=F32)
        mix += lax.dot_general(ot_ref[...], wo_mla_ref[...], (((0,), (0,)), ((), ())), preferred_element_type=F32)
        mix += jnp.dot(dn_ref[...], wo_dn_ref[...], preferred_element_type=F32)
        x1 = h_ref[...] + g2_ref[...] * mix
        x1_ref[...] = x1
        xn_ref[...] = (_rms_rows(x1, nw_ref[...]) * (1.0 + sc_ref[...]) + sh_ref[...]).astype(BF16)
        acc_ref[...] = jnp.zeros_like(acc_ref)

    hid = jnp.dot(xn_ref[...], w1_ref[...], preferred_element_type=F32)
    hid = jnp.maximum(hid, 0.0)
    acc_ref[...] += jnp.dot((hid * hid).astype(BF16), w2_ref[...], preferred_element_type=F32)

    @pl.when(k == pl.num_programs(1) - 1)
    def _():
        o_ref[...] = x1_ref[...] + g5_ref[...] * acc_ref[...]


def _out_mlp(h, s5, ot, dn, wo_s5, wo_mla, wo_dn, mods, nw, w1, w2, n_rows):
    mod_spec = lambda j: pl.BlockSpec((None, None, 1, D_MODEL), lambda i, k: (_mod_index(i), j, 0, 0))
    full = lambda a: pl.BlockSpec(a.shape, lambda i, k: (0,) * a.ndim)
    return pl.pallas_call(
        _out_mlp_kernel,
        out_shape=jax.ShapeDtypeStruct((n_rows, D_MODEL), F32),
        grid=(n_rows // TM, D_FF // TF),
        in_specs=[
            pl.BlockSpec((TM, D_MODEL), lambda i, k: (i, 0)),
            pl.BlockSpec((TM, S5_WIDTH), lambda i, k: (i, 0)),
            pl.BlockSpec((MLA_WIDTH, TM), lambda i, k: (0, i)),
            pl.BlockSpec((TM, DN_WIDTH), lambda i, k: (i, 0)),
            full(wo_s5), full(wo_mla), full(wo_dn),
            mod_spec(2), mod_spec(3), mod_spec(4), mod_spec(5),
            full(nw),
            pl.BlockSpec((D_MODEL, TF), lambda i, k: (0, k)),
            pl.BlockSpec((TF, D_MODEL), lambda i, k: (k, 0)),
        ],
        out_specs=pl.BlockSpec((TM, D_MODEL), lambda i, k: (i, 0)),
        scratch_shapes=[
            pltpu.VMEM((TM, D_MODEL), F32),
            pltpu.VMEM((TM, D_MODEL), BF16),
            pltpu.VMEM((TM, D_MODEL), F32),
        ],
        compiler_params=_cparams(("arbitrary", "arbitrary")),
        name="out_mlp",
    )(h, s5, ot, dn, wo_s5, wo_mla, wo_dn, mods, mods, mods, mods, nw, w1, w2)


def _cmul(ar, ai, br, bi):
    return ar * br - ai * bi, ar * bi + ai * br


def _s5_discretize(lam_re, lam_im, log_dt, b_re, b_im):
    dt = jnp.exp(log_dt.astype(F32))[:, None]
    lr, li = lam_re.astype(F32), lam_im.astype(F32)
    mag = jnp.exp(lr * dt)
    a_re, a_im = mag * jnp.cos(li * dt), mag * jnp.sin(li * dt)
    den = lr * lr + li * li
    f_re = ((a_re - 1.0) * lr + a_im * li) / den
    f_im = (a_im * lr - (a_re - 1.0) * li) / den
    bb_re, bb_im = _cmul(f_re[..., None], f_im[..., None], b_re.astype(F32), b_im.astype(F32))
    return a_re, a_im, bb_re, bb_im


def _s5_combine(e1, e2):
    a1r, a1i, h1r, h1i = e1
    a2r, a2i, h2r, h2i = e2
    ar, ai = _cmul(a2r, a2i, a1r, a1i)
    hr, hi = _cmul(a2r, a2i, h1r, h1i)
    return ar, ai, hr + h2r, hi + h2i


def _s5_scan(u, a_re, a_im, bb_re, bb_im, c_re, c_im, h0_re, h0_im, reverse):
    bu_re = jnp.einsum('blgh,gph->blgp', u, bb_re)
    bu_im = jnp.einsum('blgh,gph->blgp', u, bb_im)
    pr, pi, hr, hi = lax.associative_scan(
        _s5_combine,
        (jnp.broadcast_to(a_re, bu_re.shape), jnp.broadcast_to(a_im, bu_re.shape), bu_re, bu_im),
        reverse=reverse, axis=1)
    cr, ci = _cmul(pr, pi, h0_re[:, None], h0_im[:, None])
    hr, hi = hr + cr, hi + ci
    y = (jnp.einsum('ghp,blgp->blgh', c_re.astype(F32), hr)
         - jnp.einsum('ghp,blgp->blgh', c_im.astype(F32), hi))
    end = 0 if reverse else -1
    return y, hr[:, end], hi[:, end]


def _s5_mixer(u_x, u_c, lam_re, lam_im, log_dt, b_re, b_im, c_re, c_im, d_skip, glu_w):
    def groups(u):
        return u.astype(F32).reshape(u.shape[0], u.shape[1], S5_GROUPS, S5_GROUP)

    ux, uc = groups(u_x), groups(u_c)
    d = d_skip.astype(F32).reshape(S5_GROUPS, S5_GROUP)
    y_x, y_c = d * ux, d * uc
    zeros = jnp.zeros((u_c.shape[0], S5_GROUPS, S5_STATE), F32)
    for direction in range(2):
        rev = direction == 1
        a_re, a_im, bb_re, bb_im = _s5_discretize(
            lam_re[direction], lam_im[direction], log_dt[direction], b_re[direction], b_im[direction])
        yc, hc_re, hc_im = _s5_scan(uc, a_re, a_im, bb_re, bb_im, c_re[direction], c_im[direction],
                                    zeros, zeros, rev)
        yx, _, _ = _s5_scan(ux, a_re, a_im, bb_re, bb_im, c_re[direction], c_im[direction],
                            hc_re, hc_im, rev)
        y_x, y_c = y_x + yx, y_c + yc

    def glu(y, like):
        z = jax.nn.gelu(y.reshape(like.shape[0], like.shape[1], S5_WIDTH)).astype(like.dtype) @ glu_w
        return z[..., :S5_WIDTH] * jax.nn.sigmoid(z[..., S5_WIDTH:])

    return glu(y_x, u_x), glu(y_c, u_c)


def _l2norm(x):
    xf = x.astype(F32)
    return xf * lax.rsqrt(jnp.sum(xf * xf, axis=-1, keepdims=True) + EPS)


def _rms(x, w):
    xf = x.astype(F32)
    y = xf * lax.rsqrt(jnp.mean(xf * xf, axis=-1, keepdims=True) + EPS)
    return y.astype(x.dtype) * w


def _short_conv(x, w):
    pad = DN_CONV // 2
    return lax.conv_general_dilated(
        x, w[:, None, :].astype(x.dtype), window_strides=(1,), padding=[(pad, pad)],
        dimension_numbers=("NWC", "WIO", "NWC"), feature_group_count=x.shape[-1])


def _gated_delta_chunked(q, k, v, g, beta, s0):
    b, l, h, dk = q.shape
    n = l // DN_CHUNK

    def to_chunks(t):
        return jnp.moveaxis(t.astype(F32).reshape(b, n, DN_CHUNK, h, -1), 3, 1)

    qc = to_chunks(q) * (dk ** -0.5)
    kc, vc = to_chunks(k), to_chunks(v)
    gcum = jnp.cumsum(to_chunks(g[..., None])[..., 0], axis=-1)
    bc = to_chunks(beta[..., None])[..., 0]
    idx = jnp.arange(DN_CHUNK)
    incl = idx[:, None] >= idx[None, :]
    strict = idx[:, None] > idx[None, :]
    decay = jnp.exp(jnp.where(incl, gcum[..., :, None] - gcum[..., None, :], -jnp.inf))
    kb = kc * bc[..., None]
    a_mat = jnp.where(strict, jnp.einsum('bhncd,bhnsd->bhncs', kb, kc) * decay, 0.0)
    eye = jnp.eye(DN_CHUNK, dtype=F32)
    t_mat = lax.linalg.triangular_solve(a_mat + eye, jnp.broadcast_to(eye, a_mat.shape),
                                        left_side=True, lower=True)
    w_v = t_mat @ (vc * bc[..., None])
    w_k = t_mat @ (kb * jnp.exp(gcum)[..., None])
    attn = jnp.where(incl, jnp.einsum('bhncd,bhnsd->bhncs', qc, kc) * decay, 0.0)
    q_dec = qc * jnp.exp(gcum)[..., None]
    g_last = gcum[..., -1]
    k_dec = kc * jnp.exp(g_last[..., None] - gcum)[..., None]

    def step(state, xs):
        wv, wk, at, qd, kd, gl = xs
        v_new = wv - wk @ state
        o = qd @ state + at @ v_new
        state = state * jnp.exp(gl)[..., None, None] + jnp.swapaxes(kd, -1, -2) @ v_new
        return state, o

    xs = tuple(jnp.moveaxis(t, 2, 0) for t in (w_v, w_k, attn, q_dec, k_dec, g_last))
    s_fin, o = lax.scan(step, s0.astype(F32), xs)
    o = jnp.moveaxis(o, 0, 2).transpose(0, 2, 3, 1, 4).reshape(b, l, h, v.shape[-1])
    return o, s_fin


def _dn_mixer(qkv_x, qkv_c, gt_x, gt_c, ba_x, ba_c, conv_w, a_log, dt_bias, norm_w):
    def prep(qkv, ba):
        b, l = qkv.shape[:2]
        y = jax.nn.silu(_short_conv(qkv, conv_w))
        qk_w = DN_HEADS * DN_DK
        q = _l2norm(y[..., :qk_w].reshape(b, l, DN_HEADS, DN_DK))
        k = _l2norm(y[..., qk_w:2 * qk_w].reshape(b, l, DN_HEADS, DN_DK))
        v = y[..., 2 * qk_w:].reshape(b, l, DN_HEADS, DN_DV)
        ba = ba.astype(F32).reshape(b, l, 2, 2, DN_HEADS)
        beta = jax.nn.sigmoid(ba[:, :, 0])
        g = -jnp.exp(a_log.astype(F32)) * jax.nn.softplus(ba[:, :, 1] + dt_bias.astype(F32))
        return q, k, v, g, beta

    qx, kx, vx, gx, bx = prep(qkv_x, ba_x)
    qc, kc, vc, gc, bc = prep(qkv_c, ba_c)
    s0 = jnp.zeros((qkv_c.shape[0], DN_HEADS, DN_DK, DN_DV), F32)
    o_x, o_c = 0.0, 0.0
    for d in range(2):
        flip = (lambda t: jnp.flip(t, axis=1)) if d == 1 else (lambda t: t)
        oc, s_ctx = _gated_delta_chunked(flip(qc), flip(kc), flip(vc), flip(gc[:, :, d]), flip(bc[:, :, d]), s0)
        ox, _ = _gated_delta_chunked(flip(qx), flip(kx), flip(vx), flip(gx[:, :, d]), flip(bx[:, :, d]), s_ctx)
        o_x, o_c = o_x + flip(ox), o_c + flip(oc)

    def out(o, gt):
        b, l = gt.shape[:2]
        gate = jax.nn.silu(gt.reshape(b, l, DN_HEADS, DN_DV).astype(F32))
        return (_rms(o, norm_w) * gate).reshape(b, l, DN_WIDTH).astype(gt.dtype)

    return out(o_x, gt_x), out(o_c, gt_c)


def _pad_w_in(w):
    d = w.shape[0]
    s = [0]
    for n in IN_SIZES:
        s.append(s[-1] + n)
    u, cq, ckv, kr, qkv, gt, ba = (w[:, s[i]:s[i + 1]] for i in range(7))
    z = lambda n: jnp.zeros((d, n), w.dtype)
    return jnp.concatenate(
        [u, cq, ckv, z(KR_LANE), kr, z(LANES - KR_LANE - MLA_ROPE), qkv, gt, ba, z(LANES - 4 * DN_HEADS)], axis=1)


def _head_slots(w, width):
    kdim = w.shape[0]
    w = w.reshape(kdim, MLA_HEADS, width)
    return jnp.pad(w, ((0, 0), (0, 0), (0, HEAD_SLOT - width))).reshape(kdim, MLA_HEADS * HEAD_SLOT)


def _rope_tables():
    rows = SEQ // GRID_W
    row = jnp.repeat(jnp.arange(rows, dtype=F32), GRID_W)
    col = jnp.tile(jnp.arange(GRID_W, dtype=F32), rows)
    inv = ROPE_BASE ** (-jnp.arange(ROPE_FREQS, dtype=F32) / ROPE_FREQS)
    ang = jnp.concatenate([row[:, None] * inv, col[:, None] * inv], axis=-1)
    cos, sin = jnp.cos(ang), jnp.sin(ang)
    half = MLA_ROPE // 2
    ones = lambda n: jnp.ones((SEQ, n), F32)
    zeros = lambda n: jnp.zeros((SEQ, n), F32)
    cos_t = jnp.concatenate([ones(MLA_NOPE), cos, cos, ones(HEAD_SLOT - MLA_DH)], axis=1)
    sin_t = jnp.concatenate([zeros(MLA_NOPE), -sin, sin, zeros(HEAD_SLOT - MLA_DH)], axis=1)
    cos_t = jnp.concatenate([jnp.tile(cos_t, (BATCH, 1)), jnp.ones((ROWS_C, HEAD_SLOT), F32)], axis=0)
    sin_t = jnp.concatenate([jnp.tile(sin_t, (BATCH, 1)), jnp.zeros((ROWS_C, HEAD_SLOT), F32)], axis=0)
    del half
    return cos_t, sin_t


def _split_rows(a, width):
    return a[:ROWS_X].reshape(BATCH, SEQ, width), a[ROWS_X:].reshape(BATCH, CTX_LEN, width)


def _join_rows(ax, ac, width):
    return jnp.concatenate([ax.reshape(ROWS_X, width), ac.reshape(ROWS_C, width)], axis=0)


def kernel(x, c, ctx, c_ctx, w_ada, b_ada, norm1_w, w_in, s5_lambda_re, s5_lambda_im, s5_log_dt, s5_b_re, s5_b_im, s5_c_re, s5_c_im, s5_d, s5_glu_w, mla_q_norm_w, mla_kv_norm_w, mla_w_uq, mla_w_ukv, mla_q_gain, mla_k_gain, dn_conv_w, dn_a_log, dn_dt_bias, dn_norm_w, w_out, norm2_w, w_ff1, w_ff2):
    h = jnp.concatenate([x.reshape(ROWS_X, D_MODEL), ctx.reshape(ROWS_C, D_MODEL)], axis=0)
    c_pad = jnp.concatenate([c, c_ctx[None], jnp.zeros((3, D_MODEL), F32)], axis=0)
    mods_all = _ada_all(c_pad, w_ada, b_ada)
    cos_t, sin_t = _rope_tables()

    for l in range(DEPTH):
        ctx_out = l < DEPTH - 1
        mods = mods_all[l, :BATCH + 1].reshape(BATCH + 1, 6, 1, D_MODEL)
        u, mla_in, qkv, gt, ba = _in_proj(h, norm1_w[l][None], mods, _pad_w_in(w_in[l]).astype(BF16))

        u_x, u_c = _split_rows(u, S5_WIDTH)
        s5_x, s5_c = _s5_mixer(u_x, u_c, s5_lambda_re[l], s5_lambda_im[l], s5_log_dt[l], s5_b_re[l],
                               s5_b_im[l], s5_c_re[l], s5_c_im[l], s5_d[l], s5_glu_w[l])
        s5 = _join_rows(s5_x, s5_c, S5_WIDTH).astype(BF16)

        wq = _head_slots(mla_w_uq[l], MLA_DH).astype(BF16)
        w_ukv = mla_w_ukv[l].reshape(MLA_KV_RANK, MLA_HEADS, MLA_NOPE + MLA_V)
        wk = _head_slots(w_ukv[:, :, :MLA_NOPE].reshape(MLA_KV_RANK, MLA_HEADS * MLA_NOPE), MLA_NOPE).astype(BF16)
        wvt = w_ukv[:, :, MLA_NOPE:].reshape(MLA_KV_RANK, MLA_WIDTH).T.astype(BF16)
        gain_pad = lambda g: jnp.pad(g, (0, HEAD_SLOT - MLA_DH))[None]
        q, k4, vt4 = _mla_proj(mla_in, cos_t, sin_t, mla_q_norm_w[l][None], mla_kv_norm_w[l][None],
                               wq, wk, wvt, gain_pad(mla_q_gain[l]), gain_pad(mla_k_gain[l]))
        ot = _flash(q, k4, vt4, ctx_out)

        qkv_x, qkv_c = _split_rows(qkv, DN_QKV)
        gt_x, gt_c = _split_rows(gt, DN_WIDTH)
        ba_x, ba_c = _split_rows(ba[:, :4 * DN_HEADS], 4 * DN_HEADS)
        dn_x, dn_c = _dn_mixer(qkv_x, qkv_c, gt_x, gt_c, ba_x, ba_c, dn_conv_w[l], dn_a_log[l],
                               dn_dt_bias[l], dn_norm_w[l])
        dn = _join_rows(dn_x, dn_c, DN_WIDTH).astype(BF16)

        wo = w_out[l].astype(BF16)
        n_rows = ROWS if ctx_out else ROWS_X
        h = _out_mlp(h, s5, ot, dn, wo[:S5_WIDTH], wo[S5_WIDTH:S5_WIDTH + MLA_WIDTH], wo[S5_WIDTH + MLA_WIDTH:],
                     mods, norm2_w[l][None], w_ff1[l].astype(BF16), w_ff2[l].astype(BF16), n_rows)
    return h.reshape(BATCH, SEQ, D_MODEL)
```

```python
import functools
import math

import jax
import jax.numpy as jnp
from jax import lax
from jax.experimental import pallas as pl
from jax.experimental.pallas import tpu as pltpu

F32 = jnp.float32
BF16 = jnp.bfloat16

D_MODEL = 1024
BATCH = 4
SEQ = 4096
DEPTH = 4
GRID_W = 64
CTX_LEN = 256
EPS = 1e-6

S5_GROUP = 16
S5_GROUPS = 16
S5_WIDTH = 256
S5_STATE = 64

MLA_HEADS = 8
MLA_NOPE = 64
MLA_ROPE = 32
MLA_V = 64
MLA_DH = 96
MLA_Q_RANK = 256
MLA_KV_RANK = 128
MLA_WIDTH = 512
ROPE_BASE = 10000.0
ROPE_FREQS = 8

DN_HEADS = 4
DN_DK = 64
DN_DV = 64
DN_QKV = 768
DN_WIDTH = 256
DN_CONV = 5
DN_CHUNK = 64

D_FF = 4 * D_MODEL
IN_SIZES = (S5_WIDTH, MLA_Q_RANK, MLA_KV_RANK, MLA_ROPE, DN_QKV, DN_WIDTH, 4 * DN_HEADS)

ROWS_X = BATCH * SEQ
ROWS_C = BATCH * CTX_LEN
ROWS = ROWS_X + ROWS_C
LANES = 128
HEAD_SLOT = LANES
KEY_BLK = 256
N_XBLK = SEQ // KEY_BLK
CTX_BLK0 = ROWS_X // KEY_BLK

IN_PAD = 1920
COL_U, COL_CQ, COL_CKV, COL_KR, COL_QKV, COL_GT, COL_BA = 0, 256, 512, 640, 768, 1536, 1792
KR_LANE = MLA_NOPE

TM = 512
TF = 1024
TQ = 256
VMEM_LIMIT = 56 * 1024 * 1024

LOG2E = 1.4426950408889634


def _cparams(sem):
    return pltpu.CompilerParams(dimension_semantics=sem, vmem_limit_bytes=VMEM_LIMIT)


def _split3_dot(a, w):
    a_hi = a.astype(BF16)
    a_lo = (a - a_hi.astype(F32)).astype(BF16)
    w_hi = w.astype(BF16)
    w_lo = (w - w_hi.astype(F32)).astype(BF16)
    out = jnp.dot(a_hi, w_hi, preferred_element_type=F32)
    out += jnp.dot(a_lo, w_hi, preferred_element_type=F32)
    out += jnp.dot(a_hi, w_lo, preferred_element_type=F32)
    return out


def _rms_rows(x, w):
    return x * lax.rsqrt(jnp.mean(x * x, axis=-1, keepdims=True) + EPS) * w


def _ada_kernel(c_ref, w_ref, b_ref, o_ref):
    c = c_ref[...]
    s = c * jax.nn.sigmoid(c)
    o_ref[...] = _split3_dot(s, w_ref[...]) + b_ref[...]


def _ada_all(c_pad, w_ada, b_ada):
    tn = 1536
    return pl.pallas_call(
        _ada_kernel,
        out_shape=jax.ShapeDtypeStruct((DEPTH, 8, 6 * D_MODEL), F32),
        grid=(DEPTH, 6 * D_MODEL // tn),
        in_specs=[
            pl.BlockSpec((8, D_MODEL), lambda l, j: (0, 0)),
            pl.BlockSpec((None, D_MODEL, tn), lambda l, j: (l, 0, j)),
            pl.BlockSpec((None, 1, tn), lambda l, j: (l, 0, j)),
        ],
        out_specs=pl.BlockSpec((None, 8, tn), lambda l, j: (l, 0, j)),
        compiler_params=_cparams(("arbitrary", "arbitrary")),
        name="ada_mod",
    )(c_pad, w_ada, b_ada.reshape(DEPTH, 1, 6 * D_MODEL))


def _mod_index(i):
    return jnp.minimum(i // (SEQ // TM), BATCH)


def _in_proj_kernel(h_ref, nw_ref, sh_ref, sc_ref, w_ref, u_ref, mla_ref, qkv_ref, gt_ref, ba_ref):
    y = _rms_rows(h_ref[...], nw_ref[...]) * (1.0 + sc_ref[...]) + sh_ref[...]
    z = jnp.dot(y.astype(BF16), w_ref[...], preferred_element_type=F32)
    u_ref[...] = z[:, COL_U:COL_CQ]
    mla_ref[...] = z[:, COL_CQ:COL_QKV]
    qkv_ref[...] = z[:, COL_QKV:COL_GT]
    gt_ref[...] = z[:, COL_GT:COL_BA]
    ba_ref[...] = z[:, COL_BA:IN_PAD]


def _in_proj(h, nw, mods, w_pad):
    widths = (256, 512, 768, 256, 128)
    mod_spec = lambda j: pl.BlockSpec((None, None, 1, D_MODEL), lambda i: (_mod_index(i), j, 0, 0))
    return pl.pallas_call(
        _in_proj_kernel,
        out_shape=[jax.ShapeDtypeStruct((ROWS, n), F32) for n in widths],
        grid=(ROWS // TM,),
        in_specs=[
            pl.BlockSpec((TM, D_MODEL), lambda i: (i, 0)),
            pl.BlockSpec((1, D_MODEL), lambda i: (0, 0)),
            mod_spec(0),
            mod_spec(1),
            pl.BlockSpec((D_MODEL, IN_PAD), lambda i: (0, 0)),
        ],
        out_specs=[pl.BlockSpec((TM, n), lambda i: (i, 0)) for n in widths],
        compiler_params=_cparams(("arbitrary",)),
        name="in_proj",
    )(h, nw, mods, mods, w_pad)


def _mla_proj_kernel(m_ref, cos_ref, sin_ref, qnw_ref, kvnw_ref, wq_ref, wk_ref, wvt_ref, qg_ref, kg_ref,
                     q_ref, k_ref, vt_ref):
    m = m_ref[...]
    cqn = _rms_rows(m[:, 0:MLA_Q_RANK], qnw_ref[...]).astype(BF16)
    ckvn = _rms_rows(m[:, MLA_Q_RANK:MLA_Q_RANK + MLA_KV_RANK], kvnw_ref[...]).astype(BF16)
    kr = m[:, MLA_Q_RANK + MLA_KV_RANK:]
    qf = jnp.dot(cqn, wq_ref[...], preferred_element_type=F32)
    kf = jnp.dot(ckvn, wk_ref[...], preferred_element_type=F32)
    vt = lax.dot_general(wvt_ref[...], ckvn, (((1,), (1,)), ((), ())), preferred_element_type=F32)
    cos = cos_ref[...]
    sin = sin_ref[...]
    lane = lax.broadcasted_iota(jnp.int32, (1, HEAD_SLOT), 1)
    first_half = lane < (MLA_NOPE + MLA_ROPE // 2)

    def head_norm_rope(xh, gain):
        ss = jnp.sum(xh * xh, axis=-1, keepdims=True) * (1.0 / MLA_DH)
        xh = xh * lax.rsqrt(ss + EPS) * gain
        swapped = jnp.where(first_half, pltpu.roll(xh, HEAD_SLOT - MLA_ROPE // 2, 1), pltpu.roll(xh, MLA_ROPE // 2, 1))
        return xh * cos + swapped * sin

    n_sub = m.shape[0] // KEY_BLK
    for h in range(MLA_HEADS):
        sl = slice(h * HEAD_SLOT, (h + 1) * HEAD_SLOT)
        qh = head_norm_rope(qf[:, sl], qg_ref[...]) * (MLA_DH ** -0.5 * LOG2E)
        q_ref[h] = qh.astype(BF16)
        kh = head_norm_rope(kf[:, sl] + kr, kg_ref[...]).astype(BF16)
        vh = vt[h * MLA_V:(h + 1) * MLA_V, :].astype(BF16)
        for s in range(n_sub):
            k_ref[h, s] = kh[s * KEY_BLK:(s + 1) * KEY_BLK, :]
            vt_ref[h, s] = vh[:, s * KEY_BLK:(s + 1) * KEY_BLK]


def _mla_proj(mla_in, cos_t, sin_t, qnw, kvnw, wq, wk, wvt, qg, kg):
    n_sub = TM // KEY_BLK
    full = lambda a: pl.BlockSpec(a.shape, lambda i: (0,) * a.ndim)
    return pl.pallas_call(
        _mla_proj_kernel,
        out_shape=[
            jax.ShapeDtypeStruct((MLA_HEADS, ROWS, HEAD_SLOT), BF16),
            jax.ShapeDtypeStruct((MLA_HEADS, ROWS // KEY_BLK, KEY_BLK, HEAD_SLOT), BF16),
            jax.ShapeDtypeStruct((MLA_HEADS, ROWS // KEY_BLK, MLA_V, KEY_BLK), BF16),
        ],
        grid=(ROWS // TM,),
        in_specs=[
            pl.BlockSpec((TM, 512), lambda i: (i, 0)),
            pl.BlockSpec((TM, HEAD_SLOT), lambda i: (i, 0)),
            pl.BlockSpec((TM, HEAD_SLOT), lambda i: (i, 0)),
            full(qnw), full(kvnw), full(wq), full(wk), full(wvt), full(qg), full(kg),
        ],
        out_specs=[
            pl.BlockSpec((MLA_HEADS, TM, HEAD_SLOT), lambda i: (0, i, 0)),
            pl.BlockSpec((MLA_HEADS, n_sub, KEY_BLK, HEAD_SLOT), lambda i: (0, i, 0, 0)),
            pl.BlockSpec((MLA_HEADS, n_sub, MLA_V, KEY_BLK), lambda i: (0, i, 0, 0)),
        ],
        compiler_params=_cparams(("arbitrary",)),
        name="mla_proj",
    )(mla_in, cos_t, sin_t, qnw, kvnw, wq, wk, wvt, qg, kg)


def _flash_kernel(q_ref, kx_ref, kc_ref, vtx_ref, vtc_ref, o_ref, *, n_qx):
    qi = pl.program_id(2)
    q = q_ref[...]

    def block(carry, k, vt):
        m, l, acc = carry
        s = lax.dot_general(k, q, (((1,), (1,)), ((), ())), preferred_element_type=F32)
        m_new = jnp.maximum(m, jnp.max(s, axis=0, keepdims=True))
        p = jnp.exp2(s - m_new)
        alpha = jnp.exp2(m - m_new)
        l = alpha * l + jnp.sum(p, axis=0, keepdims=True)
        acc = alpha * acc + jnp.dot(vt, p.astype(BF16), preferred_element_type=F32)
        return m_new, l, acc

    init = (jnp.full((1, TQ), -jnp.inf, F32), jnp.zeros((1, TQ), F32), jnp.zeros((MLA_V, TQ), F32))
    carry = block(init, kc_ref[0], vtc_ref[0])
    n_x = jnp.where(qi < n_qx, N_XBLK, 0)
    carry = lax.fori_loop(0, n_x, lambda j, c: block(c, kx_ref[j], vtx_ref[j]), carry)
    m, l, acc = carry
    o_ref[...] = (acc * (1.0 / l)).astype(BF16)


def _flash(q, k4, vt4, ctx_out):
    n_qx = SEQ // TQ
    n_q = n_qx + (1 if ctx_out else 0)
    ctx_tile0 = ROWS_X // TQ

    def q_idx(b, h, qi):
        return jnp.where(qi < n_qx, b * n_qx + qi, ctx_tile0 + b)

    return pl.pallas_call(
        functools.partial(_flash_kernel, n_qx=n_qx),
        out_shape=jax.ShapeDtypeStruct((MLA_WIDTH, ROWS), BF16),
        grid=(BATCH, MLA_HEADS, n_q),
        in_specs=[
            pl.BlockSpec((None, TQ, HEAD_SLOT), lambda b, h, qi: (h, q_idx(b, h, qi), 0)),
            pl.BlockSpec((None, N_XBLK, KEY_BLK, HEAD_SLOT), lambda b, h, qi: (h, b, 0, 0)),
            pl.BlockSpec((None, 1, KEY_BLK, HEAD_SLOT), lambda b, h, qi: (h, CTX_BLK0 + b, 0, 0)),
            pl.BlockSpec((None, N_XBLK, MLA_V, KEY_BLK), lambda b, h, qi: (h, b, 0, 0)),
            pl.BlockSpec((None, 1, MLA_V, KEY_BLK), lambda b, h, qi: (h, CTX_BLK0 + b, 0, 0)),
        ],
        out_specs=pl.BlockSpec((MLA_V, TQ), lambda b, h, qi: (h, q_idx(b, h, qi))),
        compiler_params=_cparams(("arbitrary", "arbitrary", "arbitrary")),
        name="mla_flash",
    )(q, k4, k4, vt4, vt4)


def _out_mlp_kernel(h_ref, s5_ref, ot_ref, dn_ref, wo_s5_ref, wo_mla_ref, wo_dn_ref,
                    g2_ref, sh_ref, sc_ref, g5_ref, nw_ref, w1_ref, w2_ref, o_ref,
                    x1_ref, xn_ref, acc_ref):
    k = pl.program_id(1)

    @pl.when(k == 0)
    def _():
        mix = jnp.dot(s5_ref[...], wo_s5_ref[...], preferred_element_type=F32)
        mix += lax.dot_general(ot_ref[...], wo_mla_ref[...], (((0,), (0,)), ((), ())), preferred_element_type=F32)
        mix += jnp.dot(dn_ref[...], wo_dn_ref[...], preferred_element_type=F32)
        x1 = h_ref[...] + g2_ref[...] * mix
        x1_ref[...] = x1
        xn_ref[...] = (_rms_rows(x1, nw_ref[...]) * (1.0 + sc_ref[...]) + sh_ref[...]).astype(BF16)
        acc_ref[...] = jnp.zeros_like(acc_ref)

    hid = jnp.dot(xn_ref[...], w1_ref[...], preferred_element_type=F32)
    hid = jnp.maximum(hid, 0.0)
    acc_ref[...] += jnp.dot((hid * hid).astype(BF16), w2_ref[...], preferred_element_type=F32)

    @pl.when(k == pl.num_programs(1) - 1)
    def _():
        o_ref[...] = x1_ref[...] + g5_ref[...] * acc_ref[...]


def _out_mlp(h, s5, ot, dn, wo_s5, wo_mla, wo_dn, mods, nw, w1, w2, n_rows):
    mod_spec = lambda j: pl.BlockSpec((None, None, 1, D_MODEL), lambda i, k: (_mod_index(i), j, 0, 0))
    full = lambda a: pl.BlockSpec(a.shape, lambda i, k: (0,) * a.ndim)
    return pl.pallas_call(
        _out_mlp_kernel,
        out_shape=jax.ShapeDtypeStruct((n_rows, D_MODEL), F32),
        grid=(n_rows // TM, D_FF // TF),
        in_specs=[
            pl.BlockSpec((TM, D_MODEL), lambda i, k: (i, 0)),
            pl.BlockSpec((TM, S5_WIDTH), lambda i, k: (i, 0)),
            pl.BlockSpec((MLA_WIDTH, TM), lambda i, k: (0, i)),
            pl.BlockSpec((TM, DN_WIDTH), lambda i, k: (i, 0)),
            full(wo_s5), full(wo_mla), full(wo_dn),
            mod_spec(2), mod_spec(3), mod_spec(4), mod_spec(5),
            full(nw),
            pl.BlockSpec((D_MODEL, TF), lambda i, k: (0, k)),
            pl.BlockSpec((TF, D_MODEL), lambda i, k: (k, 0)),
        ],
        out_specs=pl.BlockSpec((TM, D_MODEL), lambda i, k: (i, 0)),
        scratch_shapes=[
            pltpu.VMEM((TM, D_MODEL), F32),
            pltpu.VMEM((TM, D_MODEL), BF16),
            pltpu.VMEM((TM, D_MODEL), F32),
        ],
        compiler_params=_cparams(("arbitrary", "arbitrary")),
        name="out_mlp",
    )(h, s5, ot, dn, wo_s5, wo_mla, wo_dn, mods, mods, mods, mods, nw, w1, w2)


S5_NSTATE = 2 * S5_GROUPS * S5_STATE
S5_SLABS = S5_NSTATE // LANES
S5_CHAINS = 2 * BATCH
S5_STEPS = CTX_LEN + SEQ
S5_TB = 128
S5_PITCH = S5_TB + 8


def _s5_kernel(u_ref, bmat_ref, cmat_ref, are_ref, aim_ref, y_ref, st_ref, h_ref):
    @pl.when(pl.program_id(0) == 0)
    def _():
        h_ref[...] = jnp.zeros_like(h_ref)

    half = S5_SLABS // 2
    for d in range(2):
        u = u_ref[d * BATCH:(d + 1) * BATCH].reshape(BATCH * S5_TB, S5_WIDTH).astype(BF16)
        bu = jnp.dot(u, bmat_ref[d], preferred_element_type=F32)
        for b in range(BATCH):
            row0 = (d * BATCH + b) * S5_PITCH
            for j in range(S5_SLABS):
                st_ref[j, row0:row0 + S5_TB, :] = bu[b * S5_TB:(b + 1) * S5_TB, j * LANES:(j + 1) * LANES]

    a_re = [are_ref[j] for j in range(half)]
    a_im = [aim_ref[j] for j in range(half)]

    def step(s, h):
        new = []
        rows = pl.ds(s, S5_CHAINS, stride=S5_PITCH)
        for j in range(half):
            hr, hi = h[j], h[half + j]
            nr = a_re[j] * hr - a_im[j] * hi + st_ref[j, rows, :]
            ni = a_re[j] * hi + a_im[j] * hr + st_ref[half + j, rows, :]
            st_ref[j, rows, :] = nr
            st_ref[half + j, rows, :] = ni
            new.append((nr, ni))
        return tuple(n[0] for n in new) + tuple(n[1] for n in new)

    h = lax.fori_loop(0, S5_TB, step, tuple(h_ref[j] for j in range(S5_SLABS)), unroll=4)
    for j in range(S5_SLABS):
        h_ref[j] = h[j]

    for d in range(2):
        parts = []
        for b in range(BATCH):
            row0 = (d * BATCH + b) * S5_PITCH
            parts.append(jnp.concatenate([st_ref[j, row0:row0 + S5_TB, :] for j in range(S5_SLABS)], axis=1))
        hm = jnp.concatenate(parts, axis=0).astype(BF16)
        y = jnp.dot(hm, cmat_ref[d], preferred_element_type=F32)
        y_ref[d * BATCH:(d + 1) * BATCH] = y.reshape(BATCH, S5_TB, S5_WIDTH)


def _s5_scan_chains(u_ch, bmat, cmat, a_re, a_im):
    full = lambda a: pl.BlockSpec(a.shape, lambda i: (0,) * a.ndim)
    return pl.pallas_call(
        _s5_kernel,
        out_shape=jax.ShapeDtypeStruct((S5_CHAINS, S5_STEPS, S5_WIDTH), F32),
        grid=(S5_STEPS // S5_TB,),
        in_specs=[pl.BlockSpec((S5_CHAINS, S5_TB, S5_WIDTH), lambda i: (0, i, 0)),
                  full(bmat), full(cmat), full(a_re), full(a_im)],
        out_specs=pl.BlockSpec((S5_CHAINS, S5_TB, S5_WIDTH), lambda i: (0, i, 0)),
        scratch_shapes=[pltpu.VMEM((S5_SLABS, S5_CHAINS * S5_PITCH, LANES), F32),
                        pltpu.VMEM((S5_SLABS, S5_CHAINS, LANES), F32)],
        compiler_params=_cparams(("arbitrary",)),
        name="s5_scan",
    )(u_ch, bmat, cmat, a_re, a_im)


def _s5_glu_kernel(u_ref, yf_ref, yb_ref, d_ref, w_ref, o_ref):
    y = d_ref[...] * u_ref[...] + yf_ref[...] + yb_ref[...]
    z = jnp.dot(jax.nn.gelu(y).astype(BF16), w_ref[...], preferred_element_type=F32)
    o_ref[...] = (z[:, :S5_WIDTH] * jax.nn.sigmoid(z[:, S5_WIDTH:])).astype(BF16)


def _s5_glu(u, yf, yb, d_skip, glu_w):
    row = pl.BlockSpec((TM, S5_WIDTH), lambda i: (i, 0))
    return pl.pallas_call(
        _s5_glu_kernel,
        out_shape=jax.ShapeDtypeStruct((ROWS, S5_WIDTH), BF16),
        grid=(ROWS // TM,),
        in_specs=[row, row, row,
                  pl.BlockSpec((1, S5_WIDTH), lambda i: (0, 0)),
                  pl.BlockSpec((S5_WIDTH, 2 * S5_WIDTH), lambda i: (0, 0))],
        out_specs=row,
        compiler_params=_cparams(("arbitrary",)),
        name="s5_glu",
    )(u, yf, yb, d_skip, glu_w)


def _s5_params(lam_re, lam_im, log_dt, b_re, b_im, c_re, c_im):
    dt = jnp.exp(log_dt)[..., None]
    mag = jnp.exp(lam_re * dt)
    a_re, a_im = mag * jnp.cos(lam_im * dt), mag * jnp.sin(lam_im * dt)
    den = lam_re * lam_re + lam_im * lam_im
    f_re = ((a_re - 1.0) * lam_re + a_im * lam_im) / den
    f_im = (a_im * lam_re - (a_re - 1.0) * lam_im) / den
    bb_re = f_re[..., None] * b_re - f_im[..., None] * b_im
    bb_im = f_re[..., None] * b_im + f_im[..., None] * b_re
    eye = jnp.eye(S5_GROUPS, dtype=F32)
    blk_b = lambda m: jnp.einsum('dgph,gk->dghkp', m, eye).reshape(2, S5_WIDTH, S5_GROUPS * S5_STATE)
    blk_c = lambda m: jnp.einsum('dghp,gk->dgpkh', m, eye).reshape(2, S5_GROUPS * S5_STATE, S5_WIDTH)
    bmat = jnp.concatenate([blk_b(bb_re), blk_b(bb_im)], axis=2).astype(BF16)
    cmat = jnp.concatenate([blk_c(c_re), -blk_c(c_im)], axis=1).astype(BF16)
    per_chain = lambda a: jnp.repeat(a.reshape(2, S5_SLABS // 2, LANES), BATCH, axis=0).transpose(1, 0, 2)
    return bmat, cmat, per_chain(a_re), per_chain(a_im)


def _s5_mixer(u, lam_re, lam_im, log_dt, b_re, b_im, c_re, c_im, d_skip, glu_w):
    bmat, cmat, a_re, a_im = _s5_params(lam_re, lam_im, log_dt, b_re, b_im, c_re, c_im)
    u_x, u_c = _split_rows(u, S5_WIDTH)
    fwd = jnp.concatenate([u_c, u_x], axis=1)
    bwd = jnp.concatenate([u_c[:, ::-1], u_x[:, ::-1]], axis=1)
    y_ch = _s5_scan_chains(jnp.concatenate([fwd, bwd], axis=0), bmat, cmat, a_re, a_im)
    yf = _join_rows(y_ch[:BATCH, CTX_LEN:], y_ch[:BATCH, :CTX_LEN], S5_WIDTH)
    yb = _join_rows(y_ch[BATCH:, CTX_LEN:][:, ::-1], y_ch[BATCH:, :CTX_LEN][:, ::-1], S5_WIDTH)
    return _s5_glu(u, yf, yb, d_skip[None], glu_w.astype(BF16))


def _l2norm(x):
    xf = x.astype(F32)
    return xf * lax.rsqrt(jnp.sum(xf * xf, axis=-1, keepdims=True) + EPS)


def _rms(x, w):
    xf = x.astype(F32)
    y = xf * lax.rsqrt(jnp.mean(xf * xf, axis=-1, keepdims=True) + EPS)
    return y.astype(x.dtype) * w


def _short_conv(x, w):
    pad = DN_CONV // 2
    return lax.conv_general_dilated(
        x, w[:, None, :].astype(x.dtype), window_strides=(1,), padding=[(pad, pad)],
        dimension_numbers=("NWC", "WIO", "NWC"), feature_group_count=x.shape[-1])


def _gated_delta_chunked(q, k, v, g, beta, s0):
    b, l, h, dk = q.shape
    n = l // DN_CHUNK

    def to_chunks(t):
        return jnp.moveaxis(t.astype(F32).reshape(b, n, DN_CHUNK, h, -1), 3, 1)

    qc = to_chunks(q) * (dk ** -0.5)
    kc, vc = to_chunks(k), to_chunks(v)
    gcum = jnp.cumsum(to_chunks(g[..., None])[..., 0], axis=-1)
    bc = to_chunks(beta[..., None])[..., 0]
    idx = jnp.arange(DN_CHUNK)
    incl = idx[:, None] >= idx[None, :]
    strict = idx[:, None] > idx[None, :]
    decay = jnp.exp(jnp.where(incl, gcum[..., :, None] - gcum[..., None, :], -jnp.inf))
    kb = kc * bc[..., None]
    a_mat = jnp.where(strict, jnp.einsum('bhncd,bhnsd->bhncs', kb, kc) * decay, 0.0)
    eye = jnp.eye(DN_CHUNK, dtype=F32)
    t_mat = lax.linalg.triangular_solve(a_mat + eye, jnp.broadcast_to(eye, a_mat.shape),
                                        left_side=True, lower=True)
    w_v = t_mat @ (vc * bc[..., None])
    w_k = t_mat @ (kb * jnp.exp(gcum)[..., None])
    attn = jnp.where(incl, jnp.einsum('bhncd,bhnsd->bhncs', qc, kc) * decay, 0.0)
    q_dec = qc * jnp.exp(gcum)[..., None]
    g_last = gcum[..., -1]
    k_dec = kc * jnp.exp(g_last[..., None] - gcum)[..., None]

    def step(state, xs):
        wv, wk, at, qd, kd, gl = xs
        v_new = wv - wk @ state
        o = qd @ state + at @ v_new
        state = state * jnp.exp(gl)[..., None, None] + jnp.swapaxes(kd, -1, -2) @ v_new
        return state, o

    xs = tuple(jnp.moveaxis(t, 2, 0) for t in (w_v, w_k, attn, q_dec, k_dec, g_last))
    s_fin, o = lax.scan(step, s0.astype(F32), xs)
    o = jnp.moveaxis(o, 0, 2).transpose(0, 2, 3, 1, 4).reshape(b, l, h, v.shape[-1])
    return o, s_fin


def _dn_mixer(qkv_x, qkv_c, gt_x, gt_c, ba_x, ba_c, conv_w, a_log, dt_bias, norm_w):
    def prep(qkv, ba):
        b, l = qkv.shape[:2]
        y = jax.nn.silu(_short_conv(qkv, conv_w))
        qk_w = DN_HEADS * DN_DK
        q = _l2norm(y[..., :qk_w].reshape(b, l, DN_HEADS, DN_DK))
        k = _l2norm(y[..., qk_w:2 * qk_w].reshape(b, l, DN_HEADS, DN_DK))
        v = y[..., 2 * qk_w:].reshape(b, l, DN_HEADS, DN_DV)
        ba = ba.astype(F32).reshape(b, l, 2, 2, DN_HEADS)
        beta = jax.nn.sigmoid(ba[:, :, 0])
        g = -jnp.exp(a_log.astype(F32)) * jax.nn.softplus(ba[:, :, 1] + dt_bias.astype(F32))
        return q, k, v, g, beta

    qx, kx, vx, gx, bx = prep(qkv_x, ba_x)
    qc, kc, vc, gc, bc = prep(qkv_c, ba_c)
    s0 = jnp.zeros((qkv_c.shape[0], DN_HEADS, DN_DK, DN_DV), F32)
    o_x, o_c = 0.0, 0.0
    for d in range(2):
        flip = (lambda t: jnp.flip(t, axis=1)) if d == 1 else (lambda t: t)
        oc, s_ctx = _gated_delta_chunked(flip(qc), flip(kc), flip(vc), flip(gc[:, :, d]), flip(bc[:, :, d]), s0)
        ox, _ = _gated_delta_chunked(flip(qx), flip(kx), flip(vx), flip(gx[:, :, d]), flip(bx[:, :, d]), s_ctx)
        o_x, o_c = o_x + flip(ox), o_c + flip(oc)

    def out(o, gt):
        b, l = gt.shape[:2]
        gate = jax.nn.silu(gt.reshape(b, l, DN_HEADS, DN_DV).astype(F32))
        return (_rms(o, norm_w) * gate).reshape(b, l, DN_WIDTH).astype(gt.dtype)

    return out(o_x, gt_x), out(o_c, gt_c)


def _pad_w_in(w):
    d = w.shape[0]
    s = [0]
    for n in IN_SIZES:
        s.append(s[-1] + n)
    u, cq, ckv, kr, qkv, gt, ba = (w[:, s[i]:s[i + 1]] for i in range(7))
    z = lambda n: jnp.zeros((d, n), w.dtype)
    return jnp.concatenate(
        [u, cq, ckv, z(KR_LANE), kr, z(LANES - KR_LANE - MLA_ROPE), qkv, gt, ba, z(LANES - 4 * DN_HEADS)], axis=1)


def _head_slots(w, width):
    kdim = w.shape[0]
    w = w.reshape(kdim, MLA_HEADS, width)
    return jnp.pad(w, ((0, 0), (0, 0), (0, HEAD_SLOT - width))).reshape(kdim, MLA_HEADS * HEAD_SLOT)


def _rope_tables():
    rows = SEQ // GRID_W
    row = jnp.repeat(jnp.arange(rows, dtype=F32), GRID_W)
    col = jnp.tile(jnp.arange(GRID_W, dtype=F32), rows)
    inv = ROPE_BASE ** (-jnp.arange(ROPE_FREQS, dtype=F32) / ROPE_FREQS)
    ang = jnp.concatenate([row[:, None] * inv, col[:, None] * inv], axis=-1)
    cos, sin = jnp.cos(ang), jnp.sin(ang)
    half = MLA_ROPE // 2
    ones = lambda n: jnp.ones((SEQ, n), F32)
    zeros = lambda n: jnp.zeros((SEQ, n), F32)
    cos_t = jnp.concatenate([ones(MLA_NOPE), cos, cos, ones(HEAD_SLOT - MLA_DH)], axis=1)
    sin_t = jnp.concatenate([zeros(MLA_NOPE), -sin, sin, zeros(HEAD_SLOT - MLA_DH)], axis=1)
    cos_t = jnp.concatenate([jnp.tile(cos_t, (BATCH, 1)), jnp.ones((ROWS_C, HEAD_SLOT), F32)], axis=0)
    sin_t = jnp.concatenate([jnp.tile(sin_t, (BATCH, 1)), jnp.zeros((ROWS_C, HEAD_SLOT), F32)], axis=0)
    del half
    return cos_t, sin_t


def _split_rows(a, width):
    return a[:ROWS_X].reshape(BATCH, SEQ, width), a[ROWS_X:].reshape(BATCH, CTX_LEN, width)


def _join_rows(ax, ac, width):
    return jnp.concatenate([ax.reshape(ROWS_X, width), ac.reshape(ROWS_C, width)], axis=0)


def kernel(x, c, ctx, c_ctx, w_ada, b_ada, norm1_w, w_in, s5_lambda_re, s5_lambda_im, s5_log_dt, s5_b_re, s5_b_im, s5_c_re, s5_c_im, s5_d, s5_glu_w, mla_q_norm_w, mla_kv_norm_w, mla_w_uq, mla_w_ukv, mla_q_gain, mla_k_gain, dn_conv_w, dn_a_log, dn_dt_bias, dn_norm_w, w_out, norm2_w, w_ff1, w_ff2):
    h = jnp.concatenate([x.reshape(ROWS_X, D_MODEL), ctx.reshape(ROWS_C, D_MODEL)], axis=0)
    c_pad = jnp.concatenate([c, c_ctx[None], jnp.zeros((3, D_MODEL), F32)], axis=0)
    mods_all = _ada_all(c_pad, w_ada, b_ada)
    cos_t, sin_t = _rope_tables()

    for l in range(DEPTH):
        ctx_out = l < DEPTH - 1
        mods = mods_all[l, :BATCH + 1].reshape(BATCH + 1, 6, 1, D_MODEL)
        u, mla_in, qkv, gt, ba = _in_proj(h, norm1_w[l][None], mods, _pad_w_in(w_in[l]).astype(BF16))

        s5 = _s5_mixer(u, s5_lambda_re[l], s5_lambda_im[l], s5_log_dt[l], s5_b_re[l], s5_b_im[l],
                       s5_c_re[l], s5_c_im[l], s5_d[l], s5_glu_w[l])

        wq = _head_slots(mla_w_uq[l], MLA_DH).astype(BF16)
        w_ukv = mla_w_ukv[l].reshape(MLA_KV_RANK, MLA_HEADS, MLA_NOPE + MLA_V)
        wk = _head_slots(w_ukv[:, :, :MLA_NOPE].reshape(MLA_KV_RANK, MLA_HEADS * MLA_NOPE), MLA_NOPE).astype(BF16)
        wvt = w_ukv[:, :, MLA_NOPE:].reshape(MLA_KV_RANK, MLA_WIDTH).T.astype(BF16)
        gain_pad = lambda g: jnp.pad(g, (0, HEAD_SLOT - MLA_DH))[None]
        q, k4, vt4 = _mla_proj(mla_in, cos_t, sin_t, mla_q_norm_w[l][None], mla_kv_norm_w[l][None],
                               wq, wk, wvt, gain_pad(mla_q_gain[l]), gain_pad(mla_k_gain[l]))
        ot = _flash(q, k4, vt4, ctx_out)

        qkv_x, qkv_c = _split_rows(qkv, DN_QKV)
        gt_x, gt_c = _split_rows(gt, DN_WIDTH)
        ba_x, ba_c = _split_rows(ba[:, :4 * DN_HEADS], 4 * DN_HEADS)
        dn_x, dn_c = _dn_mixer(qkv_x, qkv_c, gt_x, gt_c, ba_x, ba_c, dn_conv_w[l], dn_a_log[l],
                               dn_dt_bias[l], dn_norm_w[l])
        dn = _join_rows(dn_x, dn_c, DN_WIDTH).astype(BF16)

        wo = w_out[l].astype(BF16)
        n_rows = ROWS if ctx_out else ROWS_X
        h = _out_mlp(h, s5, ot, dn, wo[:S5_WIDTH], wo[S5_WIDTH:S5_WIDTH + MLA_WIDTH], wo[S5_WIDTH + MLA_WIDTH:],
                     mods, norm2_w[l][None], w_ff1[l].astype(BF16), w_ff2[l].astype(BF16), n_rows)
    return h.reshape(BATCH, SEQ, D_MODEL)
```

```python
import functools
import math

import jax
import jax.numpy as jnp
from jax import lax
from jax.experimental import pallas as pl
from jax.experimental.pallas import tpu as pltpu

F32 = jnp.float32
BF16 = jnp.bfloat16

D_MODEL = 1024
BATCH = 4
SEQ = 4096
DEPTH = 4
GRID_W = 64
CTX_LEN = 256
EPS = 1e-6

S5_GROUP = 16
S5_GROUPS = 16
S5_WIDTH = 256
S5_STATE = 64

MLA_HEADS = 8
MLA_NOPE = 64
MLA_ROPE = 32
MLA_V = 64
MLA_DH = 96
MLA_Q_RANK = 256
MLA_KV_RANK = 128
MLA_WIDTH = 512
ROPE_BASE = 10000.0
ROPE_FREQS = 8

DN_HEADS = 4
DN_DK = 64
DN_DV = 64
DN_QKV = 768
DN_WIDTH = 256
DN_CONV = 5
DN_CHUNK = 64

D_FF = 4 * D_MODEL
IN_SIZES = (S5_WIDTH, MLA_Q_RANK, MLA_KV_RANK, MLA_ROPE, DN_QKV, DN_WIDTH, 4 * DN_HEADS)

ROWS_X = BATCH * SEQ
ROWS_C = BATCH * CTX_LEN
ROWS = ROWS_X + ROWS_C
LANES = 128
HEAD_SLOT = LANES
KEY_BLK = 256
N_XBLK = SEQ // KEY_BLK
CTX_BLK0 = ROWS_X // KEY_BLK

IN_PAD = 1920
COL_U, COL_CQ, COL_CKV, COL_KR, COL_QKV, COL_GT, COL_BA = 0, 256, 512, 640, 768, 1536, 1792
KR_LANE = MLA_NOPE

TM = 512
TF = 1024
TQ = 512
FLASH_KB = 2
VMEM_LIMIT = 56 * 1024 * 1024

LOG2E = 1.4426950408889634


def _cparams(sem):
    return pltpu.CompilerParams(dimension_semantics=sem, vmem_limit_bytes=VMEM_LIMIT)


def _split3_dot(a, w):
    a_hi = a.astype(BF16)
    a_lo = (a - a_hi.astype(F32)).astype(BF16)
    w_hi = w.astype(BF16)
    w_lo = (w - w_hi.astype(F32)).astype(BF16)
    out = jnp.dot(a_hi, w_hi, preferred_element_type=F32)
    out += jnp.dot(a_lo, w_hi, preferred_element_type=F32)
    out += jnp.dot(a_hi, w_lo, preferred_element_type=F32)
    return out


def _rms_rows(x, w):
    return x * lax.rsqrt(jnp.mean(x * x, axis=-1, keepdims=True) + EPS) * w


def _ada_kernel(c_ref, w_ref, b_ref, o_ref):
    c = c_ref[...]
    s = c * jax.nn.sigmoid(c)
    o_ref[...] = _split3_dot(s, w_ref[...]) + b_ref[...]


def _ada_all(c_pad, w_ada, b_ada):
    tn = 1536
    return pl.pallas_call(
        _ada_kernel,
        out_shape=jax.ShapeDtypeStruct((DEPTH, 8, 6 * D_MODEL), F32),
        grid=(DEPTH, 6 * D_MODEL // tn),
        in_specs=[
            pl.BlockSpec((8, D_MODEL), lambda l, j: (0, 0)),
            pl.BlockSpec((None, D_MODEL, tn), lambda l, j: (l, 0, j)),
            pl.BlockSpec((None, 1, tn), lambda l, j: (l, 0, j)),
        ],
        out_specs=pl.BlockSpec((None, 8, tn), lambda l, j: (l, 0, j)),
        compiler_params=_cparams(("arbitrary", "arbitrary")),
        name="ada_mod",
    )(c_pad, w_ada, b_ada.reshape(DEPTH, 1, 6 * D_MODEL))


def _mod_index(i):
    return jnp.minimum(i // (SEQ // TM), BATCH)


def _in_proj_kernel(h_ref, nw_ref, sh_ref, sc_ref, w_ref, u_ref, mla_ref, qkv_ref, gt_ref, ba_ref):
    y = _rms_rows(h_ref[...], nw_ref[...]) * (1.0 + sc_ref[...]) + sh_ref[...]
    z = jnp.dot(y.astype(BF16), w_ref[...], preferred_element_type=F32)
    u_ref[...] = z[:, COL_U:COL_CQ]
    mla_ref[...] = z[:, COL_CQ:COL_QKV]
    qkv_ref[...] = z[:, COL_QKV:COL_GT]
    gt_ref[...] = z[:, COL_GT:COL_BA]
    ba_ref[...] = z[:, COL_BA:IN_PAD]


def _in_proj(h, nw, mods, w_pad):
    widths = (256, 512, 768, 256, 128)
    mod_spec = lambda j: pl.BlockSpec((None, None, 1, D_MODEL), lambda i: (_mod_index(i), j, 0, 0))
    return pl.pallas_call(
        _in_proj_kernel,
        out_shape=[jax.ShapeDtypeStruct((ROWS, n), F32) for n in widths],
        grid=(ROWS // TM,),
        in_specs=[
            pl.BlockSpec((TM, D_MODEL), lambda i: (i, 0)),
            pl.BlockSpec((1, D_MODEL), lambda i: (0, 0)),
            mod_spec(0),
            mod_spec(1),
            pl.BlockSpec((D_MODEL, IN_PAD), lambda i: (0, 0)),
        ],
        out_specs=[pl.BlockSpec((TM, n), lambda i: (i, 0)) for n in widths],
        compiler_params=_cparams(("arbitrary",)),
        name="in_proj",
    )(h, nw, mods, mods, w_pad)


def _mla_proj_kernel(m_ref, cos_ref, sin_ref, qnw_ref, kvnw_ref, wq_ref, wk_ref, wvt_ref, qg_ref, kg_ref,
                     q_ref, k_ref, vt_ref):
    m = m_ref[...]
    cqn = _rms_rows(m[:, 0:MLA_Q_RANK], qnw_ref[...]).astype(BF16)
    ckvn = _rms_rows(m[:, MLA_Q_RANK:MLA_Q_RANK + MLA_KV_RANK], kvnw_ref[...]).astype(BF16)
    kr = m[:, MLA_Q_RANK + MLA_KV_RANK:]
    qf = jnp.dot(cqn, wq_ref[...], preferred_element_type=F32)
    kf = jnp.dot(ckvn, wk_ref[...], preferred_element_type=F32)
    vt = lax.dot_general(wvt_ref[...], ckvn, (((1,), (1,)), ((), ())), preferred_element_type=F32)
    cos = cos_ref[...]
    sin = sin_ref[...]
    lane = lax.broadcasted_iota(jnp.int32, (1, HEAD_SLOT), 1)
    first_half = lane < (MLA_NOPE + MLA_ROPE // 2)

    def head_norm_rope(xh, gain):
        ss = jnp.sum(xh * xh, axis=-1, keepdims=True) * (1.0 / MLA_DH)
        xh = xh * lax.rsqrt(ss + EPS) * gain
        swapped = jnp.where(first_half, pltpu.roll(xh, HEAD_SLOT - MLA_ROPE // 2, 1), pltpu.roll(xh, MLA_ROPE // 2, 1))
        return xh * cos + swapped * sin

    n_sub = m.shape[0] // KEY_BLK
    for h in range(MLA_HEADS):
        sl = slice(h * HEAD_SLOT, (h + 1) * HEAD_SLOT)
        qh = head_norm_rope(qf[:, sl], qg_ref[...]) * (MLA_DH ** -0.5 * LOG2E)
        q_ref[h] = qh.astype(BF16)
        kh = head_norm_rope(kf[:, sl] + kr, kg_ref[...]).astype(BF16)
        vh = vt[h * MLA_V:(h + 1) * MLA_V, :].astype(BF16)
        for s in range(n_sub):
            k_ref[h, s] = kh[s * KEY_BLK:(s + 1) * KEY_BLK, :]
            vt_ref[h, s] = vh[:, s * KEY_BLK:(s + 1) * KEY_BLK]


def _mla_proj(mla_in, cos_t, sin_t, qnw, kvnw, wq, wk, wvt, qg, kg):
    n_sub = TM // KEY_BLK
    full = lambda a: pl.BlockSpec(a.shape, lambda i: (0,) * a.ndim)
    return pl.pallas_call(
        _mla_proj_kernel,
        out_shape=[
            jax.ShapeDtypeStruct((MLA_HEADS, ROWS, HEAD_SLOT), BF16),
            jax.ShapeDtypeStruct((MLA_HEADS, ROWS // KEY_BLK, KEY_BLK, HEAD_SLOT), BF16),
            jax.ShapeDtypeStruct((MLA_HEADS, ROWS // KEY_BLK, MLA_V, KEY_BLK), BF16),
        ],
        grid=(ROWS // TM,),
        in_specs=[
            pl.BlockSpec((TM, 512), lambda i: (i, 0)),
            pl.BlockSpec((TM, HEAD_SLOT), lambda i: (i, 0)),
            pl.BlockSpec((TM, HEAD_SLOT), lambda i: (i, 0)),
            full(qnw), full(kvnw), full(wq), full(wk), full(wvt), full(qg), full(kg),
        ],
        out_specs=[
            pl.BlockSpec((MLA_HEADS, TM, HEAD_SLOT), lambda i: (0, i, 0)),
            pl.BlockSpec((MLA_HEADS, n_sub, KEY_BLK, HEAD_SLOT), lambda i: (0, i, 0, 0)),
            pl.BlockSpec((MLA_HEADS, n_sub, MLA_V, KEY_BLK), lambda i: (0, i, 0, 0)),
        ],
        compiler_params=_cparams(("arbitrary",)),
        name="mla_proj",
    )(mla_in, cos_t, sin_t, qnw, kvnw, wq, wk, wvt, qg, kg)


def _softmax_block(carry, q, k, vts):
    m, l, acc = carry
    s = lax.dot_general(k, q, (((1,), (1,)), ((), ())), preferred_element_type=F32)
    m_new = jnp.maximum(m, jnp.max(s, axis=0, keepdims=True))
    p = jnp.exp2(s - m_new)
    alpha = jnp.exp2(m - m_new)
    l = alpha * l + jnp.sum(p, axis=0, keepdims=True)
    pb = p.astype(BF16)
    acc = alpha * acc
    for t, vt in enumerate(vts):
        acc += jnp.dot(vt, pb[t * KEY_BLK:(t + 1) * KEY_BLK], preferred_element_type=F32)
    return m_new, l, acc


def _softmax_init(tq):
    return jnp.full((1, tq), -jnp.inf, F32), jnp.zeros((1, tq), F32), jnp.zeros((MLA_V, tq), F32)


def _flash_x_kernel(q_ref, kx_ref, kc_ref, vtx_ref, vtc_ref, o_ref):
    q = q_ref[...]
    carry = _softmax_block(_softmax_init(TQ), q, kc_ref[0], [vtc_ref[0]])
    for j in range(N_XBLK // FLASH_KB):
        k = kx_ref[j * FLASH_KB:(j + 1) * FLASH_KB].reshape(FLASH_KB * KEY_BLK, HEAD_SLOT)
        carry = _softmax_block(carry, q, k, [vtx_ref[j * FLASH_KB + t] for t in range(FLASH_KB)])
    m, l, acc = carry
    o_ref[...] = (acc * (1.0 / l)).astype(BF16)


def _flash_c_kernel(q_ref, kc_ref, vtc_ref, o_ref):
    m, l, acc = _softmax_block(_softmax_init(CTX_LEN), q_ref[...], kc_ref[0], [vtc_ref[0]])
    o_ref[...] = (acc * (1.0 / l)).astype(BF16)


def _flash(q, k4, vt4, ctx_out):
    n_qx = SEQ // TQ
    ot_x = pl.pallas_call(
        _flash_x_kernel,
        out_shape=jax.ShapeDtypeStruct((MLA_WIDTH, ROWS_X), BF16),
        grid=(BATCH, MLA_HEADS, n_qx),
        in_specs=[
            pl.BlockSpec((None, TQ, HEAD_SLOT), lambda b, h, qi: (h, b * n_qx + qi, 0)),
            pl.BlockSpec((None, N_XBLK, KEY_BLK, HEAD_SLOT), lambda b, h, qi: (h, b, 0, 0)),
            pl.BlockSpec((None, 1, KEY_BLK, HEAD_SLOT), lambda b, h, qi: (h, CTX_BLK0 + b, 0, 0)),
            pl.BlockSpec((None, N_XBLK, MLA_V, KEY_BLK), lambda b, h, qi: (h, b, 0, 0)),
            pl.BlockSpec((None, 1, MLA_V, KEY_BLK), lambda b, h, qi: (h, CTX_BLK0 + b, 0, 0)),
        ],
        out_specs=pl.BlockSpec((MLA_V, TQ), lambda b, h, qi: (h, b * n_qx + qi)),
        compiler_params=_cparams(("arbitrary", "arbitrary", "arbitrary")),
        name="mla_flash_x",
    )(q, k4, k4, vt4, vt4)
    if not ctx_out:
        return ot_x, None
    ot_c = pl.pallas_call(
        _flash_c_kernel,
        out_shape=jax.ShapeDtypeStruct((MLA_WIDTH, ROWS_C), BF16),
        grid=(BATCH, MLA_HEADS),
        in_specs=[
            pl.BlockSpec((None, CTX_LEN, HEAD_SLOT), lambda b, h: (h, CTX_BLK0 + b, 0)),
            pl.BlockSpec((None, 1, KEY_BLK, HEAD_SLOT), lambda b, h: (h, CTX_BLK0 + b, 0, 0)),
            pl.BlockSpec((None, 1, MLA_V, KEY_BLK), lambda b, h: (h, CTX_BLK0 + b, 0, 0)),
        ],
        out_specs=pl.BlockSpec((MLA_V, CTX_LEN), lambda b, h: (h, b)),
        compiler_params=_cparams(("arbitrary", "arbitrary")),
        name="mla_flash_c",
    )(q, k4, vt4)
    return ot_x, ot_c


def _out_mlp_kernel(h_ref, s5_ref, otx_ref, otc_ref, dn_ref, wo_s5_ref, wo_mla_ref, wo_dn_ref,
                    g2_ref, sh_ref, sc_ref, g5_ref, nw_ref, w1_ref, w2_ref, o_ref,
                    x1_ref, xn_ref, acc_ref):
    k = pl.program_id(1)

    @pl.when(k == 0)
    def _():
        ot = jnp.where(pl.program_id(0) >= ROWS_X // TM, otc_ref[...], otx_ref[...])
        mix = jnp.dot(s5_ref[...], wo_s5_ref[...], preferred_element_type=F32)
        mix += lax.dot_general(ot, wo_mla_ref[...], (((0,), (0,)), ((), ())), preferred_element_type=F32)
        mix += jnp.dot(dn_ref[...], wo_dn_ref[...], preferred_element_type=F32)
        x1 = h_ref[...] + g2_ref[...] * mix
        x1_ref[...] = x1
        xn_ref[...] = (_rms_rows(x1, nw_ref[...]) * (1.0 + sc_ref[...]) + sh_ref[...]).astype(BF16)
        acc_ref[...] = jnp.zeros_like(acc_ref)

    hid = jnp.dot(xn_ref[...], w1_ref[...], preferred_element_type=F32)
    hid = jnp.maximum(hid, 0.0)
    acc_ref[...] += jnp.dot((hid * hid).astype(BF16), w2_ref[...], preferred_element_type=F32)

    @pl.when(k == pl.num_programs(1) - 1)
    def _():
        o_ref[...] = x1_ref[...] + g5_ref[...] * acc_ref[...]


def _out_mlp(h, s5, ot_x, ot_c, dn, wo_s5, wo_mla, wo_dn, mods, nw, w1, w2, n_rows):
    n_xtiles = ROWS_X // TM
    if ot_c is None:
        ot_c = ot_x
    mod_spec = lambda j: pl.BlockSpec((None, None, 1, D_MODEL), lambda i, k: (_mod_index(i), j, 0, 0))
    full = lambda a: pl.BlockSpec(a.shape, lambda i, k: (0,) * a.ndim)
    return pl.pallas_call(
        _out_mlp_kernel,
        out_shape=jax.ShapeDtypeStruct((n_rows, D_MODEL), F32),
        grid=(n_rows // TM, D_FF // TF),
        in_specs=[
            pl.BlockSpec((TM, D_MODEL), lambda i, k: (i, 0)),
            pl.BlockSpec((TM, S5_WIDTH), lambda i, k: (i, 0)),
            pl.BlockSpec((MLA_WIDTH, TM), lambda i, k: (0, jnp.minimum(i, n_xtiles - 1))),
            pl.BlockSpec((MLA_WIDTH, TM), lambda i, k: (0, jnp.maximum(i - n_xtiles, 0))),
            pl.BlockSpec((TM, DN_WIDTH), lambda i, k: (i, 0)),
            full(wo_s5), full(wo_mla), full(wo_dn),
            mod_spec(2), mod_spec(3), mod_spec(4), mod_spec(5),
            full(nw),
            pl.BlockSpec((D_MODEL, TF), lambda i, k: (0, k)),
            pl.BlockSpec((TF, D_MODEL), lambda i, k: (k, 0)),
        ],
        out_specs=pl.BlockSpec((TM, D_MODEL), lambda i, k: (i, 0)),
        scratch_shapes=[
            pltpu.VMEM((TM, D_MODEL), F32),
            pltpu.VMEM((TM, D_MODEL), BF16),
            pltpu.VMEM((TM, D_MODEL), F32),
        ],
        compiler_params=_cparams(("arbitrary", "arbitrary")),
        name="out_mlp",
    )(h, s5, ot_x, ot_c, dn, wo_s5, wo_mla, wo_dn, mods, mods, mods, mods, nw, w1, w2)


S5_NSTATE = 2 * S5_GROUPS * S5_STATE
S5_SLABS = S5_NSTATE // LANES
S5_CHAINS = 2 * BATCH
S5_STEPS = CTX_LEN + SEQ
S5_TB = 128
S5_PITCH = S5_TB + 8


def _s5_kernel(u_ref, bmat_ref, cmat_ref, are_ref, aim_ref, y_ref, st_ref, h_ref):
    @pl.when(pl.program_id(0) == 0)
    def _():
        h_ref[...] = jnp.zeros_like(h_ref)

    half = S5_SLABS // 2
    for d in range(2):
        u = u_ref[d * BATCH:(d + 1) * BATCH].reshape(BATCH * S5_TB, S5_WIDTH).astype(BF16)
        bu = jnp.dot(u, bmat_ref[d], preferred_element_type=F32)
        for b in range(BATCH):
            row0 = (d * BATCH + b) * S5_PITCH
            for j in range(S5_SLABS):
                st_ref[j, row0:row0 + S5_TB, :] = bu[b * S5_TB:(b + 1) * S5_TB, j * LANES:(j + 1) * LANES]

    a_re = [are_ref[j] for j in range(half)]
    a_im = [aim_ref[j] for j in range(half)]

    def step(s, h):
        new = []
        rows = pl.ds(s, S5_CHAINS, stride=S5_PITCH)
        for j in range(half):
            hr, hi = h[j], h[half + j]
            nr = a_re[j] * hr - a_im[j] * hi + st_ref[j, rows, :]
            ni = a_re[j] * hi + a_im[j] * hr + st_ref[half + j, rows, :]
            st_ref[j, rows, :] = nr
            st_ref[half + j, rows, :] = ni
            new.append((nr, ni))
        return tuple(n[0] for n in new) + tuple(n[1] for n in new)

    h = lax.fori_loop(0, S5_TB, step, tuple(h_ref[j] for j in range(S5_SLABS)), unroll=4)
    for j in range(S5_SLABS):
        h_ref[j] = h[j]

    for d in range(2):
        parts = []
        for b in range(BATCH):
            row0 = (d * BATCH + b) * S5_PITCH
            parts.append(jnp.concatenate([st_ref[j, row0:row0 + S5_TB, :] for j in range(S5_SLABS)], axis=1))
        hm = jnp.concatenate(parts, axis=0).astype(BF16)
        y = jnp.dot(hm, cmat_ref[d], preferred_element_type=F32)
        y_ref[d * BATCH:(d + 1) * BATCH] = y.reshape(BATCH, S5_TB, S5_WIDTH)


def _s5_scan_chains(u_ch, bmat, cmat, a_re, a_im):
    full = lambda a: pl.BlockSpec(a.shape, lambda i: (0,) * a.ndim)
    return pl.pallas_call(
        _s5_kernel,
        out_shape=jax.ShapeDtypeStruct((S5_CHAINS, S5_STEPS, S5_WIDTH), F32),
        grid=(S5_STEPS // S5_TB,),
        in_specs=[pl.BlockSpec((S5_CHAINS, S5_TB, S5_WIDTH), lambda i: (0, i, 0)),
                  full(bmat), full(cmat), full(a_re), full(a_im)],
        out_specs=pl.BlockSpec((S5_CHAINS, S5_TB, S5_WIDTH), lambda i: (0, i, 0)),
        scratch_shapes=[pltpu.VMEM((S5_SLABS, S5_CHAINS * S5_PITCH, LANES), F32),
                        pltpu.VMEM((S5_SLABS, S5_CHAINS, LANES), F32)],
        compiler_params=_cparams(("arbitrary",)),
        name="s5_scan",
    )(u_ch, bmat, cmat, a_re, a_im)


def _s5_glu_kernel(u_ref, yf_ref, yb_ref, d_ref, w_ref, o_ref):
    y = d_ref[...] * u_ref[...] + yf_ref[...] + yb_ref[...]
    z = jnp.dot(jax.nn.gelu(y).astype(BF16), w_ref[...], preferred_element_type=F32)
    o_ref[...] = (z[:, :S5_WIDTH] * jax.nn.sigmoid(z[:, S5_WIDTH:])).astype(BF16)


def _s5_glu(u, yf, yb, d_skip, glu_w):
    row = pl.BlockSpec((TM, S5_WIDTH), lambda i: (i, 0))
    return pl.pallas_call(
        _s5_glu_kernel,
        out_shape=jax.ShapeDtypeStruct((ROWS, S5_WIDTH), BF16),
        grid=(ROWS // TM,),
        in_specs=[row, row, row,
                  pl.BlockSpec((1, S5_WIDTH), lambda i: (0, 0)),
                  pl.BlockSpec((S5_WIDTH, 2 * S5_WIDTH), lambda i: (0, 0))],
        out_specs=row,
        compiler_params=_cparams(("arbitrary",)),
        name="s5_glu",
    )(u, yf, yb, d_skip, glu_w)


def _s5_params(lam_re, lam_im, log_dt, b_re, b_im, c_re, c_im):
    dt = jnp.exp(log_dt)[..., None]
    mag = jnp.exp(lam_re * dt)
    a_re, a_im = mag * jnp.cos(lam_im * dt), mag * jnp.sin(lam_im * dt)
    den = lam_re * lam_re + lam_im * lam_im
    f_re = ((a_re - 1.0) * lam_re + a_im * lam_im) / den
    f_im = (a_im * lam_re - (a_re - 1.0) * lam_im) / den
    bb_re = f_re[..., None] * b_re - f_im[..., None] * b_im
    bb_im = f_re[..., None] * b_im + f_im[..., None] * b_re
    eye = jnp.eye(S5_GROUPS, dtype=F32)
    blk_b = lambda m: jnp.einsum('dgph,gk->dghkp', m, eye).reshape(2, S5_WIDTH, S5_GROUPS * S5_STATE)
    blk_c = lambda m: jnp.einsum('dghp,gk->dgpkh', m, eye).reshape(2, S5_GROUPS * S5_STATE, S5_WIDTH)
    bmat = jnp.concatenate([blk_b(bb_re), blk_b(bb_im)], axis=2).astype(BF16)
    cmat = jnp.concatenate([blk_c(c_re), -blk_c(c_im)], axis=1).astype(BF16)
    per_chain = lambda a: jnp.repeat(a.reshape(2, S5_SLABS // 2, LANES), BATCH, axis=0).transpose(1, 0, 2)
    return bmat, cmat, per_chain(a_re), per_chain(a_im)


def _s5_mixer(u, lam_re, lam_im, log_dt, b_re, b_im, c_re, c_im, d_skip, glu_w):
    bmat, cmat, a_re, a_im = _s5_params(lam_re, lam_im, log_dt, b_re, b_im, c_re, c_im)
    u_x, u_c = _split_rows(u, S5_WIDTH)
    fwd = jnp.concatenate([u_c, u_x], axis=1)
    bwd = jnp.concatenate([u_c[:, ::-1], u_x[:, ::-1]], axis=1)
    y_ch = _s5_scan_chains(jnp.concatenate([fwd, bwd], axis=0), bmat, cmat, a_re, a_im)
    yf = _join_rows(y_ch[:BATCH, CTX_LEN:], y_ch[:BATCH, :CTX_LEN], S5_WIDTH)
    yb = _join_rows(y_ch[BATCH:, CTX_LEN:][:, ::-1], y_ch[BATCH:, :CTX_LEN][:, ::-1], S5_WIDTH)
    return _s5_glu(u, yf, yb, d_skip[None], glu_w.astype(BF16))


DN_TILE = 256
DN_NCHUNK = DN_TILE // DN_CHUNK
DN_HALO = 8
DN_XTILES = SEQ // DN_TILE
DN_BASE = 8


def _exact_rows_dot(a, e):
    a_hi = a.astype(BF16)
    r = a - a_hi.astype(F32)
    a_mid = r.astype(BF16)
    a_lo = (r - a_mid.astype(F32)).astype(BF16)
    out = jnp.dot(a_hi, e, preferred_element_type=F32)
    out += jnp.dot(a_mid, e, preferred_element_type=F32)
    out += jnp.dot(a_lo, e, preferred_element_type=F32)
    return out


def _exact_cols_dot(e, a):
    a_hi = a.astype(BF16)
    r = a - a_hi.astype(F32)
    a_mid = r.astype(BF16)
    a_lo = (r - a_mid.astype(F32)).astype(BF16)
    out = jnp.dot(e, a_hi, preferred_element_type=F32)
    out += jnp.dot(e, a_mid, preferred_element_type=F32)
    out += jnp.dot(e, a_lo, preferred_element_type=F32)
    return out


def _dn_prep_kernel(x_ref, prev_ref, next_ref, ba_ref, cw_ref, nega_ref, dtb_ref, hsum_ref, expand_ref,
                    ltri_ref, utri_ref, qn_ref, kn_ref, v_ref, bf_ref, bb_ref, gf_ref, gb_ref, xe_ref):
    i = pl.program_id(0)
    is_ctx = i >= ROWS_X // DN_TILE
    first = jnp.logical_or(is_ctx, i % DN_XTILES == 0)
    last = jnp.logical_or(is_ctx, i % DN_XTILES == DN_XTILES - 1)
    xe_ref[0:DN_HALO, :] = jnp.where(first, 0.0, prev_ref[...])
    xe_ref[DN_HALO:DN_HALO + DN_TILE, :] = x_ref[...]
    xe_ref[DN_HALO + DN_TILE:, :] = jnp.where(last, 0.0, next_ref[...])
    pad = DN_CONV // 2
    y = jnp.zeros((DN_TILE, DN_QKV), F32)
    for j in range(DN_CONV):
        off = DN_HALO + j - pad
        y += cw_ref[j:j + 1, :] * xe_ref[off:off + DN_TILE, :]
    y = y * jax.nn.sigmoid(y)
    qk_w = DN_HEADS * DN_DK
    q, k = y[:, :qk_w], y[:, qk_w:2 * qk_w]
    hsum = hsum_ref[...]
    qn_ref[...] = q * lax.rsqrt(_exact_rows_dot(q * q, hsum) + EPS) * (DN_DK ** -0.5)
    kn_ref[...] = k * lax.rsqrt(_exact_rows_dot(k * k, hsum) + EPS)
    v_ref[...] = y[:, 2 * qk_w:]

    ba = ba_ref[...]
    lane = lax.broadcasted_iota(jnp.int32, (1, LANES), 1)
    z = ba + dtb_ref[...]
    softplus = jnp.maximum(z, 0.0) + jnp.log(1.0 + jnp.exp(-jnp.abs(z)))
    bg = jnp.where(lane < 2 * DN_HEADS, jax.nn.sigmoid(ba), nega_ref[...] * softplus)
    ex = _exact_rows_dot(bg, expand_ref[...])
    w = DN_WIDTH
    bf_ref[...] = ex[:, 0:w]
    bb_ref[...] = ex[:, w:2 * w]
    gf_ref[...] = _exact_cols_dot(ltri_ref[...], ex[:, 2 * w:3 * w])
    gb_ref[...] = _exact_cols_dot(utri_ref[...], ex[:, 3 * w:4 * w])


def _dn_prep(qkv, ba, conv_w, a_log, dt_bias):
    n_halo_blocks = ROWS // DN_HALO
    per_tile = DN_TILE // DN_HALO
    cw = jnp.concatenate([conv_w, jnp.zeros((8 - DN_CONV, DN_QKV), F32)], axis=0)
    lanes_g = jnp.zeros((LANES,), F32)
    nega = lanes_g.at[2 * DN_HEADS:4 * DN_HEADS].set(-jnp.exp(a_log.reshape(-1)))[None]
    dtb = lanes_g.at[2 * DN_HEADS:4 * DN_HEADS].set(dt_bias.reshape(-1))[None]
    head_of_lane = jnp.arange(DN_WIDTH) // DN_DV
    hsum = (head_of_lane[:, None] == head_of_lane[None, :]).astype(BF16)
    src = jnp.arange(LANES)[:, None]
    expand = jnp.concatenate(
        [(src == (g * DN_HEADS + head_of_lane)[None, :]) for g in range(4)], axis=1).astype(BF16)
    r = jnp.arange(DN_TILE)
    same_chunk = (r[:, None] // DN_CHUNK) == (r[None, :] // DN_CHUNK)
    ltri = (same_chunk & (r[:, None] >= r[None, :])).astype(BF16)
    utri = (same_chunk & (r[:, None] <= r[None, :])).astype(BF16)
    full = lambda a: pl.BlockSpec(a.shape, lambda i: (0,) * a.ndim)
    row = lambda n: pl.BlockSpec((DN_TILE, n), lambda i: (i, 0))
    return pl.pallas_call(
        _dn_prep_kernel,
        out_shape=[jax.ShapeDtypeStruct((ROWS, DN_WIDTH), F32)] * 7,
        grid=(ROWS // DN_TILE,),
        in_specs=[
            row(DN_QKV),
            pl.BlockSpec((DN_HALO, DN_QKV), lambda i: (jnp.maximum(i * per_tile - 1, 0), 0)),
            pl.BlockSpec((DN_HALO, DN_QKV), lambda i: (jnp.minimum((i + 1) * per_tile, n_halo_blocks - 1), 0)),
            row(LANES),
            full(cw), full(nega), full(dtb), full(hsum), full(expand), full(ltri), full(utri),
        ],
        out_specs=[row(DN_WIDTH)] * 7,
        scratch_shapes=[pltpu.VMEM((DN_TILE + 2 * DN_HALO, DN_QKV), F32)],
        compiler_params=_cparams(("arbitrary",)),
        name="dn_prep",
    )(qkv, qkv, qkv, ba, cw, nega, dtb, hsum, expand, ltri, utri)


def _dn_bd(y, blockmask):
    yb = y.astype(BF16)
    return jnp.where(blockmask, jnp.concatenate([yb] * DN_HEADS, axis=0), jnp.zeros((), BF16))


def _dn_chunk(qn, kn, v, beta, gc, s_bd, reverse, consts):
    blockmask, eye_cat, incl, strict, blk8, off_masks = consts
    dot = functools.partial(jnp.dot, preferred_element_type=F32)
    bd = lambda y: _dn_bd(y, blockmask)
    last = 0 if reverse else DN_CHUNK - 1
    glast = gc[last:last + 1, :]
    eg = jnp.exp(gc)
    kb = kn * beta
    gcrow = jnp.sum(jnp.where(eye_cat, gc, 0.0), axis=0, keepdims=True)
    decay = jnp.exp(jnp.where(incl, gc - gcrow, -jnp.inf))
    grams = lax.dot_general(jnp.concatenate([kn, qn], axis=0).astype(BF16), bd(kn),
                            (((1,), (1,)), ((), ())), preferred_element_type=F32)
    a_mat = jnp.where(strict, grams[:DN_CHUNK] * beta * decay, 0.0)
    attn = grams[DN_CHUNK:] * decay
    def mm(x, y):
        x_hi = x.astype(BF16)
        x_lo = (x - x_hi.astype(F32)).astype(BF16)
        y_hi = y.astype(BF16)
        y_lo = y - y_hi.astype(F32)
        r = dot(jnp.concatenate([x_hi, x_lo], axis=0), bd(y_hi))
        return r[:DN_CHUNK] + r[DN_CHUNK:] + dot(x_hi, bd(y_lo))

    n_pow = jnp.where(blk8, -a_mat, 0.0)
    t_mat = jnp.where(eye_cat, 1.0, 0.0) + n_pow
    for _ in range(2):
        n_pow = mm(n_pow, n_pow)
        t_mat = t_mat + mm(t_mat, n_pow)
    for off_mask in off_masks:
        t_mat = t_mat - mm(mm(t_mat, jnp.where(off_mask, a_mat, 0.0)), t_mat)
    tb = t_mat.astype(BF16)
    w_v = dot(tb, bd(v * beta))
    w_k = dot(tb, bd(kb * eg))
    sb = s_bd.astype(BF16)
    ps = dot(jnp.concatenate([w_k, qn * eg], axis=0).astype(BF16), sb)
    v_new = w_v - ps[:DN_CHUNK]
    o = ps[DN_CHUNK:] + dot(attn.astype(BF16), bd(v_new))
    k_dec = (kn * jnp.exp(glast - gc)).astype(BF16)
    upd = lax.dot_general(k_dec, v_new.astype(BF16), (((0,), (0,)), ((), ())), preferred_element_type=F32)
    s_new = s_bd * jnp.exp(glast) + jnp.where(blockmask, upd, 0.0)
    return o, s_new


def _dn_chunks_kernel(qf_ref, kf_ref, vf_ref, bf_ref, gf_ref, qb_ref, kb_ref, vb_ref, bb_ref, gb_ref,
                      of_ref, ob_ref, s_ref):
    @pl.when(pl.program_id(1) == 0)
    def _():
        s_ref[...] = jnp.zeros_like(s_ref)

    row = lax.broadcasted_iota(jnp.int32, (DN_WIDTH, DN_WIDTH), 0)
    col = lax.broadcasted_iota(jnp.int32, (DN_WIDTH, DN_WIDTH), 1)
    blockmask = (row // DN_DV) == (col // DN_DV)
    i = lax.broadcasted_iota(jnp.int32, (DN_CHUNK, DN_WIDTH), 0)
    j = lax.broadcasted_iota(jnp.int32, (DN_CHUNK, DN_WIDTH), 1) % DN_DV
    eye_cat = i == j
    blk8 = (i // DN_BASE) == (j // DN_BASE)
    off_masks = tuple(((i // (2 * s)) == (j // (2 * s))) & ((i // s) != (j // s))
                      for s in (DN_BASE, 2 * DN_BASE, 4 * DN_BASE))
    consts_f = (blockmask, eye_cat, i >= j, i > j, blk8, off_masks)
    consts_b = (blockmask, eye_cat, i <= j, i < j, blk8, off_masks)

    s_f, s_b = s_ref[0], s_ref[1]
    for c in range(DN_NCHUNK):
        rf = slice(c * DN_CHUNK, (c + 1) * DN_CHUNK)
        o, s_f = _dn_chunk(qf_ref[rf, :], kf_ref[rf, :], vf_ref[rf, :], bf_ref[rf, :], gf_ref[rf, :], s_f, False, consts_f)
        of_ref[rf, :] = o
        cb = DN_NCHUNK - 1 - c
        rb = slice(cb * DN_CHUNK, (cb + 1) * DN_CHUNK)
        o, s_b = _dn_chunk(qb_ref[rb, :], kb_ref[rb, :], vb_ref[rb, :], bb_ref[rb, :], gb_ref[rb, :], s_b, True, consts_b)
        ob_ref[rb, :] = o
    s_ref[0] = s_f
    s_ref[1] = s_b


def _dn_chunks(qn, kn, v, beta_f, beta_b, gc_f, gc_b):
    ctx_tile0 = ROWS_X // DN_TILE

    def fwd_idx(b, t):
        return (jnp.where(t == 0, ctx_tile0 + b, b * DN_XTILES + t - 1), 0)

    def bwd_idx(b, t):
        return (jnp.where(t == 0, ctx_tile0 + b, b * DN_XTILES + DN_XTILES - t), 0)

    spec_f = pl.BlockSpec((DN_TILE, DN_WIDTH), fwd_idx)
    spec_b = pl.BlockSpec((DN_TILE, DN_WIDTH), bwd_idx)
    return pl.pallas_call(
        _dn_chunks_kernel,
        out_shape=[jax.ShapeDtypeStruct((ROWS, DN_WIDTH), F32)] * 2,
        grid=(BATCH, DN_XTILES + 1),
        in_specs=[spec_f] * 5 + [spec_b] * 5,
        out_specs=[spec_f, spec_b],
        scratch_shapes=[pltpu.VMEM((2, DN_WIDTH, DN_WIDTH), F32)],
        compiler_params=_cparams(("arbitrary", "arbitrary")),
        name="dn_chunks",
    )(qn, kn, v, beta_f, gc_f, qn, kn, v, beta_b, gc_b)


def _dn_out_kernel(of_ref, ob_ref, gt_ref, nw_ref, hsum_ref, o_ref):
    o = of_ref[...] + ob_ref[...]
    ms = _exact_rows_dot(o * o, hsum_ref[...]) * (1.0 / DN_DV)
    gt = gt_ref[...]
    o_ref[...] = (o * lax.rsqrt(ms + EPS) * nw_ref[...] * (gt * jax.nn.sigmoid(gt))).astype(BF16)


def _dn_out(o_f, o_b, gt, norm_w):
    head_of_lane = jnp.arange(DN_WIDTH) // DN_DV
    hsum = (head_of_lane[:, None] == head_of_lane[None, :]).astype(BF16)
    nw = jnp.tile(norm_w, DN_HEADS)[None]
    row = pl.BlockSpec((TM, DN_WIDTH), lambda i: (i, 0))
    return pl.pallas_call(
        _dn_out_kernel,
        out_shape=jax.ShapeDtypeStruct((ROWS, DN_WIDTH), BF16),
        grid=(ROWS // TM,),
        in_specs=[row, row, row, pl.BlockSpec((1, DN_WIDTH), lambda i: (0, 0)),
                  pl.BlockSpec((DN_WIDTH, DN_WIDTH), lambda i: (0, 0))],
        out_specs=row,
        compiler_params=_cparams(("arbitrary",)),
        name="dn_out",
    )(o_f, o_b, gt, nw, hsum)


def _dn_mixer(qkv, gt, ba, conv_w, a_log, dt_bias, norm_w):
    qn, kn, v, beta_f, beta_b, gc_f, gc_b = _dn_prep(qkv, ba, conv_w, a_log, dt_bias)
    o_f, o_b = _dn_chunks(qn, kn, v, beta_f, beta_b, gc_f, gc_b)
    return _dn_out(o_f, o_b, gt, norm_w)


def _pad_w_in(w):
    d = w.shape[0]
    s = [0]
    for n in IN_SIZES:
        s.append(s[-1] + n)
    u, cq, ckv, kr, qkv, gt, ba = (w[:, s[i]:s[i + 1]] for i in range(7))
    z = lambda n: jnp.zeros((d, n), w.dtype)
    return jnp.concatenate(
        [u, cq, ckv, z(KR_LANE), kr, z(LANES - KR_LANE - MLA_ROPE), qkv, gt, ba, z(LANES - 4 * DN_HEADS)], axis=1)


def _head_slots(w, width):
    kdim = w.shape[0]
    w = w.reshape(kdim, MLA_HEADS, width)
    return jnp.pad(w, ((0, 0), (0, 0), (0, HEAD_SLOT - width))).reshape(kdim, MLA_HEADS * HEAD_SLOT)


def _rope_tables():
    rows = SEQ // GRID_W
    row = jnp.repeat(jnp.arange(rows, dtype=F32), GRID_W)
    col = jnp.tile(jnp.arange(GRID_W, dtype=F32), rows)
    inv = ROPE_BASE ** (-jnp.arange(ROPE_FREQS, dtype=F32) / ROPE_FREQS)
    ang = jnp.concatenate([row[:, None] * inv, col[:, None] * inv], axis=-1)
    cos, sin = jnp.cos(ang), jnp.sin(ang)
    half = MLA_ROPE // 2
    ones = lambda n: jnp.ones((SEQ, n), F32)
    zeros = lambda n: jnp.zeros((SEQ, n), F32)
    cos_t = jnp.concatenate([ones(MLA_NOPE), cos, cos, ones(HEAD_SLOT - MLA_DH)], axis=1)
    sin_t = jnp.concatenate([zeros(MLA_NOPE), -sin, sin, zeros(HEAD_SLOT - MLA_DH)], axis=1)
    cos_t = jnp.concatenate([jnp.tile(cos_t, (BATCH, 1)), jnp.ones((ROWS_C, HEAD_SLOT), F32)], axis=0)
    sin_t = jnp.concatenate([jnp.tile(sin_t, (BATCH, 1)), jnp.zeros((ROWS_C, HEAD_SLOT), F32)], axis=0)
    del half
    return cos_t, sin_t


def _split_rows(a, width):
    return a[:ROWS_X].reshape(BATCH, SEQ, width), a[ROWS_X:].reshape(BATCH, CTX_LEN, width)


def _join_rows(ax, ac, width):
    return jnp.concatenate([ax.reshape(ROWS_X, width), ac.reshape(ROWS_C, width)], axis=0)


def kernel(x, c, ctx, c_ctx, w_ada, b_ada, norm1_w, w_in, s5_lambda_re, s5_lambda_im, s5_log_dt, s5_b_re, s5_b_im, s5_c_re, s5_c_im, s5_d, s5_glu_w, mla_q_norm_w, mla_kv_norm_w, mla_w_uq, mla_w_ukv, mla_q_gain, mla_k_gain, dn_conv_w, dn_a_log, dn_dt_bias, dn_norm_w, w_out, norm2_w, w_ff1, w_ff2):
    h = jnp.concatenate([x.reshape(ROWS_X, D_MODEL), ctx.reshape(ROWS_C, D_MODEL)], axis=0)
    c_pad = jnp.concatenate([c, c_ctx[None], jnp.zeros((3, D_MODEL), F32)], axis=0)
    mods_all = _ada_all(c_pad, w_ada, b_ada)
    cos_t, sin_t = _rope_tables()

    for l in range(DEPTH):
        ctx_out = l < DEPTH - 1
        mods = mods_all[l, :BATCH + 1].reshape(BATCH + 1, 6, 1, D_MODEL)
        u, mla_in, qkv, gt, ba = _in_proj(h, norm1_w[l][None], mods, _pad_w_in(w_in[l]).astype(BF16))

        s5 = _s5_mixer(u, s5_lambda_re[l], s5_lambda_im[l], s5_log_dt[l], s5_b_re[l], s5_b_im[l],
                       s5_c_re[l], s5_c_im[l], s5_d[l], s5_glu_w[l])

        wq = _head_slots(mla_w_uq[l], MLA_DH).astype(BF16)
        w_ukv = mla_w_ukv[l].reshape(MLA_KV_RANK, MLA_HEADS, MLA_NOPE + MLA_V)
        wk = _head_slots(w_ukv[:, :, :MLA_NOPE].reshape(MLA_KV_RANK, MLA_HEADS * MLA_NOPE), MLA_NOPE).astype(BF16)
        wvt = w_ukv[:, :, MLA_NOPE:].reshape(MLA_KV_RANK, MLA_WIDTH).T.astype(BF16)
        gain_pad = lambda g: jnp.pad(g, (0, HEAD_SLOT - MLA_DH))[None]
        q, k4, vt4 = _mla_proj(mla_in, cos_t, sin_t, mla_q_norm_w[l][None], mla_kv_norm_w[l][None],
                               wq, wk, wvt, gain_pad(mla_q_gain[l]), gain_pad(mla_k_gain[l]))
        ot_x, ot_c = _flash(q, k4, vt4, ctx_out)

        dn = _dn_mixer(qkv, gt, ba, dn_conv_w[l], dn_a_log[l], dn_dt_bias[l], dn_norm_w[l])

        wo = w_out[l].astype(BF16)
        n_rows = ROWS if ctx_out else ROWS_X
        h = _out_mlp(h, s5, ot_x, ot_c, dn, wo[:S5_WIDTH], wo[S5_WIDTH:S5_WIDTH + MLA_WIDTH], wo[S5_WIDTH + MLA_WIDTH:],
                     mods, norm2_w[l][None], w_ff1[l].astype(BF16), w_ff2[l].astype(BF16), n_rows)
    return h.reshape(BATCH, SEQ, D_MODEL)
```

```python
import functools
import math

import jax
import jax.numpy as jnp
from jax import lax
from jax.experimental import pallas as pl
from jax.experimental.pallas import tpu as pltpu

F32 = jnp.float32
BF16 = jnp.bfloat16

D_MODEL = 1024
BATCH = 4
SEQ = 4096
DEPTH = 4
GRID_W = 64
CTX_LEN = 256
EPS = 1e-6

S5_GROUP = 16
S5_GROUPS = 16
S5_WIDTH = 256
S5_STATE = 64

MLA_HEADS = 8
MLA_NOPE = 64
MLA_ROPE = 32
MLA_V = 64
MLA_VE = MLA_V + 16
MLA_DH = 96
MLA_Q_RANK = 256
MLA_KV_RANK = 128
MLA_WIDTH = 512
ROPE_BASE = 10000.0
ROPE_FREQS = 8

DN_HEADS = 4
DN_DK = 64
DN_DV = 64
DN_QKV = 768
DN_WIDTH = 256
DN_CONV = 5
DN_CHUNK = 64

D_FF = 4 * D_MODEL
IN_SIZES = (S5_WIDTH, MLA_Q_RANK, MLA_KV_RANK, MLA_ROPE, DN_QKV, DN_WIDTH, 4 * DN_HEADS)

ROWS_X = BATCH * SEQ
ROWS_C = BATCH * CTX_LEN
ROWS = ROWS_X + ROWS_C
LANES = 128
HEAD_SLOT = LANES
KEY_BLK = 256
N_XBLK = SEQ // KEY_BLK
CTX_BLK0 = ROWS_X // KEY_BLK

IN_PAD = 1920
COL_U, COL_CQ, COL_CKV, COL_KR, COL_QKV, COL_GT, COL_BA = 0, 256, 512, 640, 768, 1536, 1792
KR_LANE = MLA_NOPE

TM = 512
TF = 1024
TQ = 4096
FLASH_KB = 2
VMEM_LIMIT = 56 * 1024 * 1024

LOG2E = 1.4426950408889634


def _cparams(sem):
    return pltpu.CompilerParams(dimension_semantics=sem, vmem_limit_bytes=VMEM_LIMIT)


def _split3_dot(a, w):
    a_hi = a.astype(BF16)
    a_lo = (a - a_hi.astype(F32)).astype(BF16)
    w_hi = w.astype(BF16)
    w_lo = (w - w_hi.astype(F32)).astype(BF16)
    out = jnp.dot(a_hi, w_hi, preferred_element_type=F32)
    out += jnp.dot(a_lo, w_hi, preferred_element_type=F32)
    out += jnp.dot(a_hi, w_lo, preferred_element_type=F32)
    return out


def _rms_rows(x, w):
    return x * lax.rsqrt(jnp.mean(x * x, axis=-1, keepdims=True) + EPS) * w


def _ada_kernel(c_ref, w_ref, b_ref, o_ref):
    c = c_ref[...]
    s = c * jax.nn.sigmoid(c)
    o_ref[...] = _split3_dot(s, w_ref[...]) + b_ref[...]


def _ada_all(c_pad, w_ada, b_ada):
    tn = 1536
    return pl.pallas_call(
        _ada_kernel,
        out_shape=jax.ShapeDtypeStruct((DEPTH, 8, 6 * D_MODEL), F32),
        grid=(DEPTH, 6 * D_MODEL // tn),
        in_specs=[
            pl.BlockSpec((8, D_MODEL), lambda l, j: (0, 0)),
            pl.BlockSpec((None, D_MODEL, tn), lambda l, j: (l, 0, j)),
            pl.BlockSpec((None, 1, tn), lambda l, j: (l, 0, j)),
        ],
        out_specs=pl.BlockSpec((None, 8, tn), lambda l, j: (l, 0, j)),
        compiler_params=_cparams(("arbitrary", "arbitrary")),
        name="ada_mod",
    )(c_pad, w_ada, b_ada.reshape(DEPTH, 1, 6 * D_MODEL))


def _mod_index(i):
    return jnp.minimum(i // (SEQ // TM), BATCH)


def _in_proj_kernel(h_ref, nw_ref, sh_ref, sc_ref, w_ref, u_ref, mla_ref, qkv_ref, gt_ref, ba_ref):
    y = _rms_rows(h_ref[...], nw_ref[...]) * (1.0 + sc_ref[...]) + sh_ref[...]
    z = jnp.dot(y.astype(BF16), w_ref[...], preferred_element_type=F32)
    u_ref[...] = z[:, COL_U:COL_CQ]
    mla_ref[...] = z[:, COL_CQ:COL_QKV]
    qkv_ref[...] = z[:, COL_QKV:COL_GT]
    gt_ref[...] = z[:, COL_GT:COL_BA]
    ba_ref[...] = z[:, COL_BA:IN_PAD]


def _in_proj(h, nw, mods, w_pad):
    widths = (256, 512, 768, 256, 128)
    mod_spec = lambda j: pl.BlockSpec((None, None, 1, D_MODEL), lambda i: (_mod_index(i), j, 0, 0))
    return pl.pallas_call(
        _in_proj_kernel,
        out_shape=[jax.ShapeDtypeStruct((ROWS, n), F32) for n in widths],
        grid=(ROWS // TM,),
        in_specs=[
            pl.BlockSpec((TM, D_MODEL), lambda i: (i, 0)),
            pl.BlockSpec((1, D_MODEL), lambda i: (0, 0)),
            mod_spec(0),
            mod_spec(1),
            pl.BlockSpec((D_MODEL, IN_PAD), lambda i: (0, 0)),
        ],
        out_specs=[pl.BlockSpec((TM, n), lambda i: (i, 0)) for n in widths],
        compiler_params=_cparams(("arbitrary",)),
        name="in_proj",
    )(h, nw, mods, mods, w_pad)


def _mla_proj_kernel(m_ref, cos_ref, sin_ref, qnw_ref, kvnw_ref, wq_ref, wk_ref, wvt_ref, qg_ref, kg_ref,
                     q_ref, k_ref, vt_ref):
    m = m_ref[...]
    cqn = _rms_rows(m[:, 0:MLA_Q_RANK], qnw_ref[...]).astype(BF16)
    ckvn = _rms_rows(m[:, MLA_Q_RANK:MLA_Q_RANK + MLA_KV_RANK], kvnw_ref[...]).astype(BF16)
    kr = m[:, MLA_Q_RANK + MLA_KV_RANK:]
    qf = jnp.dot(cqn, wq_ref[...], preferred_element_type=F32)
    kf = jnp.dot(ckvn, wk_ref[...], preferred_element_type=F32)
    vt = lax.dot_general(wvt_ref[...], ckvn, (((1,), (1,)), ((), ())), preferred_element_type=F32)
    cos = cos_ref[...]
    sin = sin_ref[...]
    lane = lax.broadcasted_iota(jnp.int32, (1, HEAD_SLOT), 1)
    first_half = lane < (MLA_NOPE + MLA_ROPE // 2)

    def head_norm_rope(xh, gain):
        ss = jnp.sum(xh * xh, axis=-1, keepdims=True) * (1.0 / MLA_DH)
        xh = xh * lax.rsqrt(ss + EPS) * gain
        swapped = jnp.where(first_half, pltpu.roll(xh, HEAD_SLOT - MLA_ROPE // 2, 1), pltpu.roll(xh, MLA_ROPE // 2, 1))
        return xh * cos + swapped * sin

    n_sub = m.shape[0] // KEY_BLK
    ones = jnp.ones((MLA_VE - MLA_V, m.shape[0]), BF16)
    for h in range(MLA_HEADS):
        sl = slice(h * HEAD_SLOT, (h + 1) * HEAD_SLOT)
        qh = head_norm_rope(qf[:, sl], qg_ref[...]) * (MLA_DH ** -0.5 * LOG2E)
        q_ref[h] = qh.astype(BF16)
        kh = head_norm_rope(kf[:, sl] + kr, kg_ref[...]).astype(BF16)
        vh = jnp.concatenate([vt[h * MLA_V:(h + 1) * MLA_V, :].astype(BF16), ones], axis=0)
        for s in range(n_sub):
            k_ref[h, s] = kh[s * KEY_BLK:(s + 1) * KEY_BLK, :]
            vt_ref[h, s] = vh[:, s * KEY_BLK:(s + 1) * KEY_BLK]


def _mla_proj(mla_in, cos_t, sin_t, qnw, kvnw, wq, wk, wvt, qg, kg):
    n_sub = TM // KEY_BLK
    full = lambda a: pl.BlockSpec(a.shape, lambda i: (0,) * a.ndim)
    return pl.pallas_call(
        _mla_proj_kernel,
        out_shape=[
            jax.ShapeDtypeStruct((MLA_HEADS, ROWS, HEAD_SLOT), BF16),
            jax.ShapeDtypeStruct((MLA_HEADS, ROWS // KEY_BLK, KEY_BLK, HEAD_SLOT), BF16),
            jax.ShapeDtypeStruct((MLA_HEADS, ROWS // KEY_BLK, MLA_VE, KEY_BLK), BF16),
        ],
        grid=(ROWS // TM,),
        in_specs=[
            pl.BlockSpec((TM, 512), lambda i: (i, 0)),
            pl.BlockSpec((TM, HEAD_SLOT), lambda i: (i, 0)),
            pl.BlockSpec((TM, HEAD_SLOT), lambda i: (i, 0)),
            full(qnw), full(kvnw), full(wq), full(wk), full(wvt), full(qg), full(kg),
        ],
        out_specs=[
            pl.BlockSpec((MLA_HEADS, TM, HEAD_SLOT), lambda i: (0, i, 0)),
            pl.BlockSpec((MLA_HEADS, n_sub, KEY_BLK, HEAD_SLOT), lambda i: (0, i, 0, 0)),
            pl.BlockSpec((MLA_HEADS, n_sub, MLA_VE, KEY_BLK), lambda i: (0, i, 0, 0)),
        ],
        compiler_params=_cparams(("arbitrary",)),
        name="mla_proj",
    )(mla_in, cos_t, sin_t, qnw, kvnw, wq, wk, wvt, qg, kg)


def _scores(k, q):
    return lax.dot_general(k, q, (((1,), (1,)), ((), ())), preferred_element_type=F32)


def _softmax_update(carry, s, vts):
    m, acc = carry
    m_new = jnp.maximum(m, jnp.max(s, axis=0, keepdims=True))
    pb = jnp.exp2(s - m_new).astype(BF16)
    acc = jnp.exp2(m - m_new) * acc
    for t, vt in enumerate(vts):
        acc += jnp.dot(vt, pb[t * KEY_BLK:(t + 1) * KEY_BLK], preferred_element_type=F32)
    return m_new, acc


def _softmax_attend(q, key_blocks):
    tq = q.shape[0]
    carry = (jnp.full((1, tq), -jnp.inf, F32), jnp.zeros((MLA_VE, tq), F32))
    s_next = _scores(key_blocks[0][0], q)
    for n, (_, vts) in enumerate(key_blocks):
        s = s_next
        if n + 1 < len(key_blocks):
            s_next = _scores(key_blocks[n + 1][0], q)
        carry = _softmax_update(carry, s, vts)
    _, acc = carry
    return (acc[:MLA_V] * (1.0 / acc[MLA_V:MLA_V + 1])).astype(BF16)


def _flash_x_kernel(q_ref, kx_ref, kc_ref, vtx_ref, vtc_ref, o_ref):
    blocks = [(kc_ref[0], [vtc_ref[0]])]
    for j in range(N_XBLK // FLASH_KB):
        k = kx_ref[j * FLASH_KB:(j + 1) * FLASH_KB].reshape(FLASH_KB * KEY_BLK, HEAD_SLOT)
        blocks.append((k, [vtx_ref[j * FLASH_KB + t] for t in range(FLASH_KB)]))
    o_ref[...] = _softmax_attend(q_ref[...], blocks)


def _flash_c_kernel(q_ref, kc_ref, vtc_ref, o_ref):
    o_ref[...] = _softmax_attend(q_ref[...], [(kc_ref[0], [vtc_ref[0]])])


def _flash(q, k4, vt4, ctx_out):
    n_qx = SEQ // TQ
    ot_x = pl.pallas_call(
        _flash_x_kernel,
        out_shape=jax.ShapeDtypeStruct((MLA_WIDTH, ROWS_X), BF16),
        grid=(BATCH, MLA_HEADS, n_qx),
        in_specs=[
            pl.BlockSpec((None, TQ, HEAD_SLOT), lambda b, h, qi: (h, b * n_qx + qi, 0)),
            pl.BlockSpec((None, N_XBLK, KEY_BLK, HEAD_SLOT), lambda b, h, qi: (h, b, 0, 0)),
            pl.BlockSpec((None, 1, KEY_BLK, HEAD_SLOT), lambda b, h, qi: (h, CTX_BLK0 + b, 0, 0)),
            pl.BlockSpec((None, N_XBLK, MLA_VE, KEY_BLK), lambda b, h, qi: (h, b, 0, 0)),
            pl.BlockSpec((None, 1, MLA_VE, KEY_BLK), lambda b, h, qi: (h, CTX_BLK0 + b, 0, 0)),
        ],
        out_specs=pl.BlockSpec((MLA_V, TQ), lambda b, h, qi: (h, b * n_qx + qi)),
        compiler_params=_cparams(("arbitrary", "arbitrary", "arbitrary")),
        name="mla_flash_x",
    )(q, k4, k4, vt4, vt4)
    if not ctx_out:
        return ot_x, None
    ot_c = pl.pallas_call(
        _flash_c_kernel,
        out_shape=jax.ShapeDtypeStruct((MLA_WIDTH, ROWS_C), BF16),
        grid=(BATCH, MLA_HEADS),
        in_specs=[
            pl.BlockSpec((None, CTX_LEN, HEAD_SLOT), lambda b, h: (h, CTX_BLK0 + b, 0)),
            pl.BlockSpec((None, 1, KEY_BLK, HEAD_SLOT), lambda b, h: (h, CTX_BLK0 + b, 0, 0)),
            pl.BlockSpec((None, 1, MLA_VE, KEY_BLK), lambda b, h: (h, CTX_BLK0 + b, 0, 0)),
        ],
        out_specs=pl.BlockSpec((MLA_V, CTX_LEN), lambda b, h: (h, b)),
        compiler_params=_cparams(("arbitrary", "arbitrary")),
        name="mla_flash_c",
    )(q, k4, vt4)
    return ot_x, ot_c


def _out_mlp_kernel(h_ref, s5_ref, otx_ref, otc_ref, dn_ref, wo_s5_ref, wo_mla_ref, wo_dn_ref,
                    g2_ref, sh_ref, sc_ref, g5_ref, nw_ref, w1_ref, w2_ref, o_ref,
                    x1_ref, xn_ref, acc_ref):
    k = pl.program_id(1)

    @pl.when(k == 0)
    def _():
        ot = jnp.where(pl.program_id(0) >= ROWS_X // TM, otc_ref[...], otx_ref[...])
        mix = jnp.dot(s5_ref[...], wo_s5_ref[...], preferred_element_type=F32)
        mix += lax.dot_general(ot, wo_mla_ref[...], (((0,), (0,)), ((), ())), preferred_element_type=F32)
        mix += jnp.dot(dn_ref[...], wo_dn_ref[...], preferred_element_type=F32)
        x1 = h_ref[...] + g2_ref[...] * mix
        x1_ref[...] = x1
        xn_ref[...] = (_rms_rows(x1, nw_ref[...]) * (1.0 + sc_ref[...]) + sh_ref[...]).astype(BF16)
        acc_ref[...] = jnp.zeros_like(acc_ref)

    hid = jnp.dot(xn_ref[...], w1_ref[...], preferred_element_type=F32)
    hid = jnp.maximum(hid, 0.0)
    acc_ref[...] += jnp.dot((hid * hid).astype(BF16), w2_ref[...], preferred_element_type=F32)

    @pl.when(k == pl.num_programs(1) - 1)
    def _():
        o_ref[...] = x1_ref[...] + g5_ref[...] * acc_ref[...]


def _out_mlp(h, s5, ot_x, ot_c, dn, wo_s5, wo_mla, wo_dn, mods, nw, w1, w2, n_rows):
    n_xtiles = ROWS_X // TM
    if ot_c is None:
        ot_c = ot_x
    mod_spec = lambda j: pl.BlockSpec((None, None, 1, D_MODEL), lambda i, k: (_mod_index(i), j, 0, 0))
    full = lambda a: pl.BlockSpec(a.shape, lambda i, k: (0,) * a.ndim)
    return pl.pallas_call(
        _out_mlp_kernel,
        out_shape=jax.ShapeDtypeStruct((n_rows, D_MODEL), F32),
        grid=(n_rows // TM, D_FF // TF),
        in_specs=[
            pl.BlockSpec((TM, D_MODEL), lambda i, k: (i, 0)),
            pl.BlockSpec((TM, S5_WIDTH), lambda i, k: (i, 0)),
            pl.BlockSpec((MLA_WIDTH, TM), lambda i, k: (0, jnp.minimum(i, n_xtiles - 1))),
            pl.BlockSpec((MLA_WIDTH, TM), lambda i, k: (0, jnp.maximum(i - n_xtiles, 0))),
            pl.BlockSpec((TM, DN_WIDTH), lambda i, k: (i, 0)),
            full(wo_s5), full(wo_mla), full(wo_dn),
            mod_spec(2), mod_spec(3), mod_spec(4), mod_spec(5),
            full(nw),
            pl.BlockSpec((D_MODEL, TF), lambda i, k: (0, k)),
            pl.BlockSpec((TF, D_MODEL), lambda i, k: (k, 0)),
        ],
        out_specs=pl.BlockSpec((TM, D_MODEL), lambda i, k: (i, 0)),
        scratch_shapes=[
            pltpu.VMEM((TM, D_MODEL), F32),
            pltpu.VMEM((TM, D_MODEL), BF16),
            pltpu.VMEM((TM, D_MODEL), F32),
        ],
        compiler_params=_cparams(("arbitrary", "arbitrary")),
        name="out_mlp",
    )(h, s5, ot_x, ot_c, dn, wo_s5, wo_mla, wo_dn, mods, mods, mods, mods, nw, w1, w2)


S5_NSTATE = 2 * S5_GROUPS * S5_STATE
S5_SLABS = S5_NSTATE // LANES
S5_CHAINS = 2 * BATCH
S5_STEPS = CTX_LEN + SEQ
S5_TB = 128
S5_PITCH = S5_TB + 8


def _s5_kernel(*refs):
    u_refs = refs[:S5_CHAINS]
    flip_ref, bmat_ref, cmat_ref, are_ref, aim_ref, yf_ref, yb_ref, st_ref, h_ref = refs[S5_CHAINS:]

    @pl.when(pl.program_id(0) == 0)
    def _():
        h_ref[...] = jnp.zeros_like(h_ref)

    half = S5_SLABS // 2
    flip = flip_ref[...]
    for d in range(2):
        blocks = [u_refs[d * BATCH + b][...].astype(BF16) for b in range(BATCH)]
        if d == 1:
            blocks = [jnp.dot(flip, blk, preferred_element_type=F32).astype(BF16) for blk in blocks]
        bu = jnp.dot(jnp.concatenate(blocks, axis=0), bmat_ref[d], preferred_element_type=F32)
        for b in range(BATCH):
            row0 = (d * BATCH + b) * S5_PITCH
            for j in range(S5_SLABS):
                st_ref[j, row0:row0 + S5_TB, :] = bu[b * S5_TB:(b + 1) * S5_TB, j * LANES:(j + 1) * LANES]

    a_re = [are_ref[j] for j in range(half)]
    a_im = [aim_ref[j] for j in range(half)]

    def step(s, h):
        new = []
        rows = pl.ds(s, S5_CHAINS, stride=S5_PITCH)
        for j in range(half):
            hr, hi = h[j], h[half + j]
            nr = a_re[j] * hr - a_im[j] * hi + st_ref[j, rows, :]
            ni = a_re[j] * hi + a_im[j] * hr + st_ref[half + j, rows, :]
            st_ref[j, rows, :] = nr
            st_ref[half + j, rows, :] = ni
            new.append((nr, ni))
        return tuple(n[0] for n in new) + tuple(n[1] for n in new)

    h = lax.fori_loop(0, S5_TB, step, tuple(h_ref[j] for j in range(S5_SLABS)), unroll=4)
    for j in range(S5_SLABS):
        h_ref[j] = h[j]

    for d, y_ref in ((0, yf_ref), (1, yb_ref)):
        parts = []
        for b in range(BATCH):
            row0 = (d * BATCH + b) * S5_PITCH
            parts.append(jnp.concatenate([st_ref[j, row0:row0 + S5_TB, :] for j in range(S5_SLABS)], axis=1))
        hm = jnp.concatenate(parts, axis=0).astype(BF16)
        y = jnp.dot(hm, cmat_ref[d], preferred_element_type=F32)
        for b in range(BATCH):
            yb = y[b * S5_TB:(b + 1) * S5_TB]
            y_ref[b] = _exact_cols_dot(flip, yb) if d == 1 else yb


def _s5_step_block(i, b, reverse):
    n_ctx = CTX_LEN // S5_TB
    n_x = SEQ // S5_TB
    ctx0 = ROWS_X // S5_TB + b * n_ctx
    if reverse:
        return jnp.where(i < n_ctx, ctx0 + n_ctx - 1 - i, b * n_x + n_x - 1 - (i - n_ctx))
    return jnp.where(i < n_ctx, ctx0 + i, b * n_x + (i - n_ctx))


def _s5_time_block(i, reverse):
    n_ctx = CTX_LEN // S5_TB
    n_all = S5_STEPS // S5_TB
    if reverse:
        return jnp.where(i < n_ctx, n_ctx - 1 - i, n_all - 1 - (i - n_ctx))
    return i


def _s5_scan_chains(u, bmat, cmat, a_re, a_im):
    full = lambda a: pl.BlockSpec(a.shape, lambda i: (0,) * a.ndim)
    flip = jnp.eye(S5_TB, dtype=BF16)[::-1]
    u_specs = [pl.BlockSpec((S5_TB, S5_WIDTH), functools.partial(lambda i, b, rev: (_s5_step_block(i, b, rev), 0),
                                                                b=c % BATCH, rev=c >= BATCH))
               for c in range(S5_CHAINS)]
    y_shape = jax.ShapeDtypeStruct((BATCH, S5_STEPS, S5_WIDTH), F32)
    return pl.pallas_call(
        _s5_kernel,
        out_shape=[y_shape, y_shape],
        grid=(S5_STEPS // S5_TB,),
        in_specs=u_specs + [full(flip), full(bmat), full(cmat), full(a_re), full(a_im)],
        out_specs=[pl.BlockSpec((BATCH, S5_TB, S5_WIDTH), lambda i: (0, _s5_time_block(i, False), 0)),
                   pl.BlockSpec((BATCH, S5_TB, S5_WIDTH), lambda i: (0, _s5_time_block(i, True), 0))],
        scratch_shapes=[pltpu.VMEM((S5_SLABS, S5_CHAINS * S5_PITCH, LANES), F32),
                        pltpu.VMEM((S5_SLABS, S5_CHAINS, LANES), F32)],
        compiler_params=_cparams(("arbitrary",)),
        name="s5_scan",
    )(*([u] * S5_CHAINS), flip, bmat, cmat, a_re, a_im)


def _s5_glu_kernel(u_ref, yf_ref, yb_ref, d_ref, w_ref, o_ref):
    y = d_ref[...] * u_ref[...] + yf_ref[...] + yb_ref[...]
    z = jnp.dot(jax.nn.gelu(y).astype(BF16), w_ref[...], preferred_element_type=F32)
    o_ref[...] = (z[:, :S5_WIDTH] * jax.nn.sigmoid(z[:, S5_WIDTH:])).astype(BF16)


def _s5_glu(u, yf, yb, d_skip, glu_w):
    tiles_x = SEQ // CTX_LEN

    def y_idx(i):
        is_x = i < BATCH * tiles_x
        return (jnp.where(is_x, i // tiles_x, i - BATCH * tiles_x), jnp.where(is_x, 1 + i % tiles_x, 0), 0)

    row = pl.BlockSpec((CTX_LEN, S5_WIDTH), lambda i: (i, 0))
    y_spec = pl.BlockSpec((None, CTX_LEN, S5_WIDTH), y_idx)
    return pl.pallas_call(
        _s5_glu_kernel,
        out_shape=jax.ShapeDtypeStruct((ROWS, S5_WIDTH), BF16),
        grid=(ROWS // CTX_LEN,),
        in_specs=[row, y_spec, y_spec,
                  pl.BlockSpec((1, S5_WIDTH), lambda i: (0, 0)),
                  pl.BlockSpec((S5_WIDTH, 2 * S5_WIDTH), lambda i: (0, 0))],
        out_specs=row,
        compiler_params=_cparams(("arbitrary",)),
        name="s5_glu",
    )(u, yf, yb, d_skip, glu_w)


def _s5_params(lam_re, lam_im, log_dt, b_re, b_im, c_re, c_im):
    dt = jnp.exp(log_dt)[..., None]
    mag = jnp.exp(lam_re * dt)
    a_re, a_im = mag * jnp.cos(lam_im * dt), mag * jnp.sin(lam_im * dt)
    den = lam_re * lam_re + lam_im * lam_im
    f_re = ((a_re - 1.0) * lam_re + a_im * lam_im) / den
    f_im = (a_im * lam_re - (a_re - 1.0) * lam_im) / den
    bb_re = f_re[..., None] * b_re - f_im[..., None] * b_im
    bb_im = f_re[..., None] * b_im + f_im[..., None] * b_re
    eye = jnp.eye(S5_GROUPS, dtype=F32)
    blk_b = lambda m: jnp.einsum('dgph,gk->dghkp', m, eye).reshape(2, S5_WIDTH, S5_GROUPS * S5_STATE)
    blk_c = lambda m: jnp.einsum('dghp,gk->dgpkh', m, eye).reshape(2, S5_GROUPS * S5_STATE, S5_WIDTH)
    bmat = jnp.concatenate([blk_b(bb_re), blk_b(bb_im)], axis=2).astype(BF16)
    cmat = jnp.concatenate([blk_c(c_re), -blk_c(c_im)], axis=1).astype(BF16)
    per_chain = lambda a: jnp.repeat(a.reshape(2, S5_SLABS // 2, LANES), BATCH, axis=0).transpose(1, 0, 2)
    return bmat, cmat, per_chain(a_re), per_chain(a_im)


def _s5_mixer(u, lam_re, lam_im, log_dt, b_re, b_im, c_re, c_im, d_skip, glu_w):
    bmat, cmat, a_re, a_im = _s5_params(lam_re, lam_im, log_dt, b_re, b_im, c_re, c_im)
    y_f, y_b = _s5_scan_chains(u, bmat, cmat, a_re, a_im)
    return _s5_glu(u, y_f, y_b, d_skip[None], glu_w.astype(BF16))


DN_TILE = 256
DN_NCHUNK = DN_TILE // DN_CHUNK
DN_HALO = 8
DN_XTILES = SEQ // DN_TILE
DN_BASE = 8


def _exact_rows_dot(a, e):
    a_hi = a.astype(BF16)
    r = a - a_hi.astype(F32)
    a_mid = r.astype(BF16)
    a_lo = (r - a_mid.astype(F32)).astype(BF16)
    out = jnp.dot(a_hi, e, preferred_element_type=F32)
    out += jnp.dot(a_mid, e, preferred_element_type=F32)
    out += jnp.dot(a_lo, e, preferred_element_type=F32)
    return out


def _exact_cols_dot(e, a):
    a_hi = a.astype(BF16)
    r = a - a_hi.astype(F32)
    a_mid = r.astype(BF16)
    a_lo = (r - a_mid.astype(F32)).astype(BF16)
    out = jnp.dot(e, a_hi, preferred_element_type=F32)
    out += jnp.dot(e, a_mid, preferred_element_type=F32)
    out += jnp.dot(e, a_lo, preferred_element_type=F32)
    return out


def _dn_prep_kernel(x_ref, prev_ref, next_ref, ba_ref, cw_ref, nega_ref, dtb_ref, hsum_ref, expand_ref,
                    ltri_ref, utri_ref, qn_ref, kn_ref, v_ref, bf_ref, bb_ref, gf_ref, gb_ref, xe_ref):
    i = pl.program_id(0)
    is_ctx = i >= ROWS_X // DN_TILE
    first = jnp.logical_or(is_ctx, i % DN_XTILES == 0)
    last = jnp.logical_or(is_ctx, i % DN_XTILES == DN_XTILES - 1)
    xe_ref[0:DN_HALO, :] = jnp.where(first, 0.0, prev_ref[...])
    xe_ref[DN_HALO:DN_HALO + DN_TILE, :] = x_ref[...]
    xe_ref[DN_HALO + DN_TILE:, :] = jnp.where(last, 0.0, next_ref[...])
    pad = DN_CONV // 2
    y = jnp.zeros((DN_TILE, DN_QKV), F32)
    for j in range(DN_CONV):
        off = DN_HALO + j - pad
        y += cw_ref[j:j + 1, :] * xe_ref[off:off + DN_TILE, :]
    y = y * jax.nn.sigmoid(y)
    qk_w = DN_HEADS * DN_DK
    q, k = y[:, :qk_w], y[:, qk_w:2 * qk_w]
    hsum = hsum_ref[...]
    qn_ref[...] = q * lax.rsqrt(_exact_rows_dot(q * q, hsum) + EPS) * (DN_DK ** -0.5)
    kn_ref[...] = k * lax.rsqrt(_exact_rows_dot(k * k, hsum) + EPS)
    v_ref[...] = y[:, 2 * qk_w:]

    ba = ba_ref[...]
    lane = lax.broadcasted_iota(jnp.int32, (1, LANES), 1)
    z = ba + dtb_ref[...]
    softplus = jnp.maximum(z, 0.0) + jnp.log(1.0 + jnp.exp(-jnp.abs(z)))
    bg = jnp.where(lane < 2 * DN_HEADS, jax.nn.sigmoid(ba), nega_ref[...] * softplus)
    ex = _exact_rows_dot(bg, expand_ref[...])
    w = DN_WIDTH
    bf_ref[...] = ex[:, 0:w]
    bb_ref[...] = ex[:, w:2 * w]
    gf_ref[...] = _exact_cols_dot(ltri_ref[...], ex[:, 2 * w:3 * w])
    gb_ref[...] = _exact_cols_dot(utri_ref[...], ex[:, 3 * w:4 * w])


def _dn_prep(qkv, ba, conv_w, a_log, dt_bias):
    n_halo_blocks = ROWS // DN_HALO
    per_tile = DN_TILE // DN_HALO
    cw = jnp.concatenate([conv_w, jnp.zeros((8 - DN_CONV, DN_QKV), F32)], axis=0)
    lanes_g = jnp.zeros((LANES,), F32)
    nega = lanes_g.at[2 * DN_HEADS:4 * DN_HEADS].set(-jnp.exp(a_log.reshape(-1)))[None]
    dtb = lanes_g.at[2 * DN_HEADS:4 * DN_HEADS].set(dt_bias.reshape(-1))[None]
    head_of_lane = jnp.arange(DN_WIDTH) // DN_DV
    hsum = (head_of_lane[:, None] == head_of_lane[None, :]).astype(BF16)
    src = jnp.arange(LANES)[:, None]
    expand = jnp.concatenate(
        [(src == (g * DN_HEADS + head_of_lane)[None, :]) for g in range(4)], axis=1).astype(BF16)
    r = jnp.arange(DN_TILE)
    same_chunk = (r[:, None] // DN_CHUNK) == (r[None, :] // DN_CHUNK)
    ltri = (same_chunk & (r[:, None] >= r[None, :])).astype(BF16)
    utri = (same_chunk & (r[:, None] <= r[None, :])).astype(BF16)
    full = lambda a: pl.BlockSpec(a.shape, lambda i: (0,) * a.ndim)
    row = lambda n: pl.BlockSpec((DN_TILE, n), lambda i: (i, 0))
    return pl.pallas_call(
        _dn_prep_kernel,
        out_shape=[jax.ShapeDtypeStruct((ROWS, DN_WIDTH), F32)] * 7,
        grid=(ROWS // DN_TILE,),
        in_specs=[
            row(DN_QKV),
            pl.BlockSpec((DN_HALO, DN_QKV), lambda i: (jnp.maximum(i * per_tile - 1, 0), 0)),
            pl.BlockSpec((DN_HALO, DN_QKV), lambda i: (jnp.minimum((i + 1) * per_tile, n_halo_blocks - 1), 0)),
            row(LANES),
            full(cw), full(nega), full(dtb), full(hsum), full(expand), full(ltri), full(utri),
        ],
        out_specs=[row(DN_WIDTH)] * 7,
        scratch_shapes=[pltpu.VMEM((DN_TILE + 2 * DN_HALO, DN_QKV), F32)],
        compiler_params=_cparams(("arbitrary",)),
        name="dn_prep",
    )(qkv, qkv, qkv, ba, cw, nega, dtb, hsum, expand, ltri, utri)


def _dn_bd(y, blockmask):
    yb = y.astype(BF16)
    return jnp.where(blockmask, jnp.concatenate([yb] * DN_HEADS, axis=0), jnp.zeros((), BF16))


def _dn_prepare(chunks, blockmask, eye_cat, blk8, off_masks):
    dot = functools.partial(jnp.dot, preferred_element_type=F32)
    bd = lambda y: _dn_bd(y, blockmask)

    def mm(x, y):
        x_hi = x.astype(BF16)
        x_lo = (x - x_hi.astype(F32)).astype(BF16)
        y_hi = y.astype(BF16)
        y_lo = y - y_hi.astype(F32)
        r = dot(jnp.concatenate([x_hi, x_lo], axis=0), bd(y_hi))
        return r[:DN_CHUNK] + r[DN_CHUNK:] + dot(x_hi, bd(y_lo))

    a_mats, attns, egs = [], [], []
    for qn, kn, v, beta, gc, incl, strict, last in chunks:
        gcrow = jnp.sum(jnp.where(eye_cat, gc, 0.0), axis=0, keepdims=True)
        decay = jnp.exp(jnp.where(incl, gc - gcrow, -jnp.inf))
        grams = lax.dot_general(jnp.concatenate([kn, qn], axis=0).astype(BF16), bd(kn),
                                (((1,), (1,)), ((), ())), preferred_element_type=F32)
        a_mats.append(jnp.where(strict, grams[:DN_CHUNK] * beta * decay, 0.0))
        attns.append((grams[DN_CHUNK:] * decay).astype(BF16))
        egs.append(jnp.exp(gc))

    n_pows = [jnp.where(blk8, -a, 0.0) for a in a_mats]
    t_mats = [jnp.where(eye_cat, 1.0, 0.0) + n for n in n_pows]
    for _ in range(2):
        n_pows = [mm(n, n) for n in n_pows]
        t_mats = [t + mm(t, n) for t, n in zip(t_mats, n_pows)]
    for off_mask in off_masks:
        z = [mm(t, jnp.where(off_mask, a, 0.0)) for t, a in zip(t_mats, a_mats)]
        t_mats = [t - mm(zi, t) for t, zi in zip(t_mats, z)]

    out = []
    for (qn, kn, v, beta, gc, incl, strict, last), t, attn, eg in zip(chunks, t_mats, attns, egs):
        tb = t.astype(BF16)
        glast = gc[last:last + 1, :]
        w_v = dot(tb, bd(v * beta))
        w_k = dot(tb, bd(kn * beta * eg))
        lhs = jnp.concatenate([w_k, qn * eg], axis=0).astype(BF16)
        k_dec = (kn * jnp.exp(glast - gc)).astype(BF16)
        out.append((w_v, lhs, attn, k_dec, jnp.exp(glast)))
    return out


def _dn_state_step(prep, s_bd, blockmask):
    w_v, lhs, attn, k_dec, e_last = prep
    dot = functools.partial(jnp.dot, preferred_element_type=F32)
    ps = dot(lhs, s_bd.astype(BF16))
    v_new = w_v - ps[:DN_CHUNK]
    o = ps[DN_CHUNK:] + dot(attn, _dn_bd(v_new, blockmask))
    upd = lax.dot_general(k_dec, v_new.astype(BF16), (((0,), (0,)), ((), ())), preferred_element_type=F32)
    return o, s_bd * e_last + jnp.where(blockmask, upd, 0.0)


def _dn_chunks_kernel(qf_ref, kf_ref, vf_ref, bf_ref, gf_ref, qb_ref, kb_ref, vb_ref, bb_ref, gb_ref,
                      of_ref, ob_ref, s_ref):
    @pl.when(pl.program_id(1) == 0)
    def _():
        s_ref[...] = jnp.zeros_like(s_ref)

    row = lax.broadcasted_iota(jnp.int32, (DN_WIDTH, DN_WIDTH), 0)
    col = lax.broadcasted_iota(jnp.int32, (DN_WIDTH, DN_WIDTH), 1)
    blockmask = (row // DN_DV) == (col // DN_DV)
    i = lax.broadcasted_iota(jnp.int32, (DN_CHUNK, DN_WIDTH), 0)
    j = lax.broadcasted_iota(jnp.int32, (DN_CHUNK, DN_WIDTH), 1) % DN_DV
    eye_cat = i == j
    blk8 = (i // DN_BASE) == (j // DN_BASE)
    off_masks = tuple(((i // (2 * s)) == (j // (2 * s))) & ((i // s) != (j // s))
                      for s in (DN_BASE, 2 * DN_BASE, 4 * DN_BASE))
    chunks, where = [], []
    for c in range(DN_NCHUNK):
        rf = slice(c * DN_CHUNK, (c + 1) * DN_CHUNK)
        chunks.append((qf_ref[rf, :], kf_ref[rf, :], vf_ref[rf, :], bf_ref[rf, :], gf_ref[rf, :],
                       i >= j, i > j, DN_CHUNK - 1))
        where.append((0, of_ref, rf))
        cb = DN_NCHUNK - 1 - c
        rb = slice(cb * DN_CHUNK, (cb + 1) * DN_CHUNK)
        chunks.append((qb_ref[rb, :], kb_ref[rb, :], vb_ref[rb, :], bb_ref[rb, :], gb_ref[rb, :],
                       i <= j, i < j, 0))
        where.append((1, ob_ref, rb))
    prepared = _dn_prepare(chunks, blockmask, eye_cat, blk8, off_masks)

    states = [s_ref[0], s_ref[1]]
    for prep, (d, o_ref, rows) in zip(prepared, where):
        o, states[d] = _dn_state_step(prep, states[d], blockmask)
        o_ref[rows, :] = o
    s_ref[0] = states[0]
    s_ref[1] = states[1]


def _dn_chunks(qn, kn, v, beta_f, beta_b, gc_f, gc_b):
    ctx_tile0 = ROWS_X // DN_TILE

    def fwd_idx(b, t):
        return (jnp.where(t == 0, ctx_tile0 + b, b * DN_XTILES + t - 1), 0)

    def bwd_idx(b, t):
        return (jnp.where(t == 0, ctx_tile0 + b, b * DN_XTILES + DN_XTILES - t), 0)

    spec_f = pl.BlockSpec((DN_TILE, DN_WIDTH), fwd_idx)
    spec_b = pl.BlockSpec((DN_TILE, DN_WIDTH), bwd_idx)
    return pl.pallas_call(
        _dn_chunks_kernel,
        out_shape=[jax.ShapeDtypeStruct((ROWS, DN_WIDTH), F32)] * 2,
        grid=(BATCH, DN_XTILES + 1),
        in_specs=[spec_f] * 5 + [spec_b] * 5,
        out_specs=[spec_f, spec_b],
        scratch_shapes=[pltpu.VMEM((2, DN_WIDTH, DN_WIDTH), F32)],
        compiler_params=_cparams(("arbitrary", "arbitrary")),
        name="dn_chunks",
    )(qn, kn, v, beta_f, gc_f, qn, kn, v, beta_b, gc_b)


def _dn_out_kernel(of_ref, ob_ref, gt_ref, nw_ref, hsum_ref, o_ref):
    o = of_ref[...] + ob_ref[...]
    ms = _exact_rows_dot(o * o, hsum_ref[...]) * (1.0 / DN_DV)
    gt = gt_ref[...]
    o_ref[...] = (o * lax.rsqrt(ms + EPS) * nw_ref[...] * (gt * jax.nn.sigmoid(gt))).astype(BF16)


def _dn_out(o_f, o_b, gt, norm_w):
    head_of_lane = jnp.arange(DN_WIDTH) // DN_DV
    hsum = (head_of_lane[:, None] == head_of_lane[None, :]).astype(BF16)
    nw = jnp.tile(norm_w, DN_HEADS)[None]
    row = pl.BlockSpec((TM, DN_WIDTH), lambda i: (i, 0))
    return pl.pallas_call(
        _dn_out_kernel,
        out_shape=jax.ShapeDtypeStruct((ROWS, DN_WIDTH), BF16),
        grid=(ROWS // TM,),
        in_specs=[row, row, row, pl.BlockSpec((1, DN_WIDTH), lambda i: (0, 0)),
                  pl.BlockSpec((DN_WIDTH, DN_WIDTH), lambda i: (0, 0))],
        out_specs=row,
        compiler_params=_cparams(("arbitrary",)),
        name="dn_out",
    )(o_f, o_b, gt, nw, hsum)


def _dn_mixer(qkv, gt, ba, conv_w, a_log, dt_bias, norm_w):
    qn, kn, v, beta_f, beta_b, gc_f, gc_b = _dn_prep(qkv, ba, conv_w, a_log, dt_bias)
    o_f, o_b = _dn_chunks(qn, kn, v, beta_f, beta_b, gc_f, gc_b)
    return _dn_out(o_f, o_b, gt, norm_w)


def _pad_w_in(w):
    d = w.shape[0]
    s = [0]
    for n in IN_SIZES:
        s.append(s[-1] + n)
    u, cq, ckv, kr, qkv, gt, ba = (w[:, s[i]:s[i + 1]] for i in range(7))
    z = lambda n: jnp.zeros((d, n), w.dtype)
    return jnp.concatenate(
        [u, cq, ckv, z(KR_LANE), kr, z(LANES - KR_LANE - MLA_ROPE), qkv, gt, ba, z(LANES - 4 * DN_HEADS)], axis=1)


def _head_slots(w, width):
    kdim = w.shape[0]
    w = w.reshape(kdim, MLA_HEADS, width)
    return jnp.pad(w, ((0, 0), (0, 0), (0, HEAD_SLOT - width))).reshape(kdim, MLA_HEADS * HEAD_SLOT)


def _rope_tables():
    rows = SEQ // GRID_W
    row = jnp.repeat(jnp.arange(rows, dtype=F32), GRID_W)
    col = jnp.tile(jnp.arange(GRID_W, dtype=F32), rows)
    inv = ROPE_BASE ** (-jnp.arange(ROPE_FREQS, dtype=F32) / ROPE_FREQS)
    ang = jnp.concatenate([row[:, None] * inv, col[:, None] * inv], axis=-1)
    cos, sin = jnp.cos(ang), jnp.sin(ang)
    half = MLA_ROPE // 2
    ones = lambda n: jnp.ones((SEQ, n), F32)
    zeros = lambda n: jnp.zeros((SEQ, n), F32)
    cos_t = jnp.concatenate([ones(MLA_NOPE), cos, cos, ones(HEAD_SLOT - MLA_DH)], axis=1)
    sin_t = jnp.concatenate([zeros(MLA_NOPE), -sin, sin, zeros(HEAD_SLOT - MLA_DH)], axis=1)
    cos_t = jnp.concatenate([jnp.tile(cos_t, (BATCH, 1)), jnp.ones((ROWS_C, HEAD_SLOT), F32)], axis=0)
    sin_t = jnp.concatenate([jnp.tile(sin_t, (BATCH, 1)), jnp.zeros((ROWS_C, HEAD_SLOT), F32)], axis=0)
    del half
    return cos_t, sin_t


def _split_rows(a, width):
    return a[:ROWS_X].reshape(BATCH, SEQ, width), a[ROWS_X:].reshape(BATCH, CTX_LEN, width)


def _join_rows(ax, ac, width):
    return jnp.concatenate([ax.reshape(ROWS_X, width), ac.reshape(ROWS_C, width)], axis=0)


def kernel(x, c, ctx, c_ctx, w_ada, b_ada, norm1_w, w_in, s5_lambda_re, s5_lambda_im, s5_log_dt, s5_b_re, s5_b_im, s5_c_re, s5_c_im, s5_d, s5_glu_w, mla_q_norm_w, mla_kv_norm_w, mla_w_uq, mla_w_ukv, mla_q_gain, mla_k_gain, dn_conv_w, dn_a_log, dn_dt_bias, dn_norm_w, w_out, norm2_w, w_ff1, w_ff2):
    h = jnp.concatenate([x.reshape(ROWS_X, D_MODEL), ctx.reshape(ROWS_C, D_MODEL)], axis=0)
    c_pad = jnp.concatenate([c, c_ctx[None], jnp.zeros((3, D_MODEL), F32)], axis=0)
    mods_all = _ada_all(c_pad, w_ada, b_ada)
    cos_t, sin_t = _rope_tables()

    for l in range(DEPTH):
        ctx_out = l < DEPTH - 1
        mods = mods_all[l, :BATCH + 1].reshape(BATCH + 1, 6, 1, D_MODEL)
        u, mla_in, qkv, gt, ba = _in_proj(h, norm1_w[l][None], mods, _pad_w_in(w_in[l]).astype(BF16))

        s5 = _s5_mixer(u, s5_lambda_re[l], s5_lambda_im[l], s5_log_dt[l], s5_b_re[l], s5_b_im[l],
                       s5_c_re[l], s5_c_im[l], s5_d[l], s5_glu_w[l])

        wq = _head_slots(mla_w_uq[l], MLA_DH).astype(BF16)
        w_ukv = mla_w_ukv[l].reshape(MLA_KV_RANK, MLA_HEADS, MLA_NOPE + MLA_V)
        wk = _head_slots(w_ukv[:, :, :MLA_NOPE].reshape(MLA_KV_RANK, MLA_HEADS * MLA_NOPE), MLA_NOPE).astype(BF16)
        wvt = w_ukv[:, :, MLA_NOPE:].reshape(MLA_KV_RANK, MLA_WIDTH).T.astype(BF16)
        gain_pad = lambda g: jnp.pad(g, (0, HEAD_SLOT - MLA_DH))[None]
        q, k4, vt4 = _mla_proj(mla_in, cos_t, sin_t, mla_q_norm_w[l][None], mla_kv_norm_w[l][None],
                               wq, wk, wvt, gain_pad(mla_q_gain[l]), gain_pad(mla_k_gain[l]))
        ot_x, ot_c = _flash(q, k4, vt4, ctx_out)

        dn = _dn_mixer(qkv, gt, ba, dn_conv_w[l], dn_a_log[l], dn_dt_bias[l], dn_norm_w[l])

        wo = w_out[l].astype(BF16)
        n_rows = ROWS if ctx_out else ROWS_X
        h = _out_mlp(h, s5, ot_x, ot_c, dn, wo[:S5_WIDTH], wo[S5_WIDTH:S5_WIDTH + MLA_WIDTH], wo[S5_WIDTH + MLA_WIDTH:],
                     mods, norm2_w[l][None], w_ff1[l].astype(BF16), w_ff2[l].astype(BF16), n_rows)
    return h.reshape(BATCH, SEQ, D_MODEL)
```

```python
import functools
import math

import jax
import jax.numpy as jnp
from jax import lax
from jax.experimental import pallas as pl
from jax.experimental.pallas import tpu as pltpu

F32 = jnp.float32
BF16 = jnp.bfloat16

D_MODEL = 1024
BATCH = 4
SEQ = 4096
DEPTH = 4
GRID_W = 64
CTX_LEN = 256
EPS = 1e-6

S5_GROUP = 16
S5_GROUPS = 16
S5_WIDTH = 256
S5_STATE = 64

MLA_HEADS = 8
MLA_NOPE = 64
MLA_ROPE = 32
MLA_V = 64
MLA_VE = MLA_V + 16
MLA_DH = 96
MLA_Q_RANK = 256
MLA_KV_RANK = 128
MLA_WIDTH = 512
ROPE_BASE = 10000.0
ROPE_FREQS = 8

DN_HEADS = 4
DN_DK = 64
DN_DV = 64
DN_QKV = 768
DN_WIDTH = 256
DN_CONV = 5
DN_CHUNK = 64

D_FF = 4 * D_MODEL
IN_SIZES = (S5_WIDTH, MLA_Q_RANK, MLA_KV_RANK, MLA_ROPE, DN_QKV, DN_WIDTH, 4 * DN_HEADS)

ROWS_X = BATCH * SEQ
ROWS_C = BATCH * CTX_LEN
ROWS = ROWS_X + ROWS_C
LANES = 128
HEAD_SLOT = LANES
KEY_BLK = 256
N_XBLK = SEQ // KEY_BLK
CTX_BLK0 = ROWS_X // KEY_BLK

IN_PAD = 1920
COL_U, COL_CQ, COL_CKV, COL_KR, COL_QKV, COL_GT, COL_BA = 0, 256, 512, 640, 768, 1536, 1792
KR_LANE = MLA_NOPE

TM = 512
TF = D_FF
TQ = 4096
FLASH_KB = 2
VMEM_LIMIT = 56 * 1024 * 1024

LOG2E = 1.4426950408889634


def _cparams(sem):
    return pltpu.CompilerParams(dimension_semantics=sem, vmem_limit_bytes=VMEM_LIMIT)


def _split3_dot(a, w):
    a_hi = a.astype(BF16)
    a_lo = (a - a_hi.astype(F32)).astype(BF16)
    w_hi = w.astype(BF16)
    w_lo = (w - w_hi.astype(F32)).astype(BF16)
    out = jnp.dot(a_hi, w_hi, preferred_element_type=F32)
    out += jnp.dot(a_lo, w_hi, preferred_element_type=F32)
    out += jnp.dot(a_hi, w_lo, preferred_element_type=F32)
    return out


def _rms_rows(x, w):
    return x * lax.rsqrt(jnp.mean(x * x, axis=-1, keepdims=True) + EPS) * w


def _ada_kernel(c_ref, w_ref, b_ref, o_ref):
    c = c_ref[...]
    s = c * jax.nn.sigmoid(c)
    o_ref[...] = _split3_dot(s, w_ref[...]) + b_ref[...]


def _ada_all(c_pad, w_ada, b_ada):
    tn = 1536
    return pl.pallas_call(
        _ada_kernel,
        out_shape=jax.ShapeDtypeStruct((DEPTH, 8, 6 * D_MODEL), F32),
        grid=(DEPTH, 6 * D_MODEL // tn),
        in_specs=[
            pl.BlockSpec((8, D_MODEL), lambda l, j: (0, 0)),
            pl.BlockSpec((None, D_MODEL, tn), lambda l, j: (l, 0, j)),
            pl.BlockSpec((None, 1, tn), lambda l, j: (l, 0, j)),
        ],
        out_specs=pl.BlockSpec((None, 8, tn), lambda l, j: (l, 0, j)),
        compiler_params=_cparams(("arbitrary", "arbitrary")),
        name="ada_mod",
    )(c_pad, w_ada, b_ada.reshape(DEPTH, 1, 6 * D_MODEL))


def _mod_index(i):
    return jnp.minimum(i // (SEQ // TM), BATCH)


def _in_proj_kernel(h_ref, nw_ref, sh_ref, sc_ref, w_ref, u_ref, mla_ref, qkv_ref, gt_ref, ba_ref):
    y = _rms_rows(h_ref[...], nw_ref[...]) * (1.0 + sc_ref[...]) + sh_ref[...]
    z = jnp.dot(y.astype(BF16), w_ref[...], preferred_element_type=F32)
    u_ref[...] = z[:, COL_U:COL_CQ]
    mla_ref[...] = z[:, COL_CQ:COL_QKV]
    qkv_ref[...] = z[:, COL_QKV:COL_GT]
    gt_ref[...] = z[:, COL_GT:COL_BA]
    ba_ref[...] = z[:, COL_BA:IN_PAD]


def _in_proj(h, nw, mods, w_pad):
    widths = (256, 512, 768, 256, 128)
    mod_spec = lambda j: pl.BlockSpec((None, None, 1, D_MODEL), lambda i: (_mod_index(i), j, 0, 0))
    return pl.pallas_call(
        _in_proj_kernel,
        out_shape=[jax.ShapeDtypeStruct((ROWS, n), F32) for n in widths],
        grid=(ROWS // TM,),
        in_specs=[
            pl.BlockSpec((TM, D_MODEL), lambda i: (i, 0)),
            pl.BlockSpec((1, D_MODEL), lambda i: (0, 0)),
            mod_spec(0),
            mod_spec(1),
            pl.BlockSpec((D_MODEL, IN_PAD), lambda i: (0, 0)),
        ],
        out_specs=[pl.BlockSpec((TM, n), lambda i: (i, 0)) for n in widths],
        compiler_params=_cparams(("arbitrary",)),
        name="in_proj",
    )(h, nw, mods, mods, w_pad)


def _mla_proj_kernel(m_ref, cos_ref, sin_ref, qnw_ref, kvnw_ref, wq_ref, wqs_ref, wk_ref, wvt_ref,
                     qg_ref, qgs_ref, kg_ref, kgs_ref, perm_ref, q_ref, k_ref, vt_ref):
    m = m_ref[...]
    cqn = _rms_rows(m[:, 0:MLA_Q_RANK], qnw_ref[...]).astype(BF16)
    ckvn = _rms_rows(m[:, MLA_Q_RANK:MLA_Q_RANK + MLA_KV_RANK], kvnw_ref[...]).astype(BF16)
    kr = m[:, MLA_Q_RANK + MLA_KV_RANK:]
    qf = jnp.dot(cqn, wq_ref[...], preferred_element_type=F32)
    qf_sw = jnp.dot(cqn, wqs_ref[...], preferred_element_type=F32)
    kf = jnp.dot(ckvn, wk_ref[...], preferred_element_type=F32)
    vt = lax.dot_general(wvt_ref[...], ckvn, (((1,), (1,)), ((), ())), preferred_element_type=F32)
    kr_sw = _exact_rows_dot(kr, perm_ref[...])
    cos = cos_ref[...]
    sin = sin_ref[...]
    q_cos, q_sin = cos * qg_ref[...], sin * qgs_ref[...]
    k_cos, k_sin = cos * kg_ref[...], sin * kgs_ref[...]

    def head_norm_rope(xh, xh_sw, g_cos, g_sin, scale):
        ss = jnp.sum(xh * xh, axis=-1, keepdims=True) * (1.0 / MLA_DH)
        return (xh * g_cos + xh_sw * g_sin) * (lax.rsqrt(ss + EPS) * scale)

    n_sub = m.shape[0] // KEY_BLK
    ones = jnp.ones((MLA_VE - MLA_V, m.shape[0]), BF16)
    for h in range(MLA_HEADS):
        sl = slice(h * HEAD_SLOT, (h + 1) * HEAD_SLOT)
        q_ref[h] = head_norm_rope(qf[:, sl], qf_sw[:, sl], q_cos, q_sin, MLA_DH ** -0.5 * LOG2E).astype(BF16)
        kh = head_norm_rope(kf[:, sl] + kr, kr_sw, k_cos, k_sin, 1.0).astype(BF16)
        vh = jnp.concatenate([vt[h * MLA_V:(h + 1) * MLA_V, :].astype(BF16), ones], axis=0)
        for s in range(n_sub):
            k_ref[h, s] = kh[s * KEY_BLK:(s + 1) * KEY_BLK, :]
            vt_ref[h, s] = vh[:, s * KEY_BLK:(s + 1) * KEY_BLK]


def _mla_proj(mla_in, cos_t, sin_t, qnw, kvnw, wq, wq_sw, wk, wvt, qg, qg_sw, kg, kg_sw, perm):
    n_sub = TM // KEY_BLK
    full = lambda a: pl.BlockSpec(a.shape, lambda i: (0,) * a.ndim)
    return pl.pallas_call(
        _mla_proj_kernel,
        out_shape=[
            jax.ShapeDtypeStruct((MLA_HEADS, ROWS, HEAD_SLOT), BF16),
            jax.ShapeDtypeStruct((MLA_HEADS, ROWS // KEY_BLK, KEY_BLK, HEAD_SLOT), BF16),
            jax.ShapeDtypeStruct((MLA_HEADS, ROWS // KEY_BLK, MLA_VE, KEY_BLK), BF16),
        ],
        grid=(ROWS // TM,),
        in_specs=[
            pl.BlockSpec((TM, 512), lambda i: (i, 0)),
            pl.BlockSpec((TM, HEAD_SLOT), lambda i: (i, 0)),
            pl.BlockSpec((TM, HEAD_SLOT), lambda i: (i, 0)),
            full(qnw), full(kvnw), full(wq), full(wq_sw), full(wk), full(wvt),
            full(qg), full(qg_sw), full(kg), full(kg_sw), full(perm),
        ],
        out_specs=[
            pl.BlockSpec((MLA_HEADS, TM, HEAD_SLOT), lambda i: (0, i, 0)),
            pl.BlockSpec((MLA_HEADS, n_sub, KEY_BLK, HEAD_SLOT), lambda i: (0, i, 0, 0)),
            pl.BlockSpec((MLA_HEADS, n_sub, MLA_VE, KEY_BLK), lambda i: (0, i, 0, 0)),
        ],
        compiler_params=_cparams(("arbitrary",)),
        name="mla_proj",
    )(mla_in, cos_t, sin_t, qnw, kvnw, wq, wq_sw, wk, wvt, qg, qg_sw, kg, kg_sw, perm)


def _scores(k, q):
    return lax.dot_general(k, q, (((1,), (1,)), ((), ())), preferred_element_type=F32)


def _softmax_update(carry, s, vts):
    m, acc = carry
    m_new = jnp.maximum(m, jnp.max(s, axis=0, keepdims=True))
    pb = jnp.exp2(s - m_new).astype(BF16)
    acc = jnp.exp2(m - m_new) * acc
    for t, vt in enumerate(vts):
        acc += jnp.dot(vt, pb[t * KEY_BLK:(t + 1) * KEY_BLK], preferred_element_type=F32)
    return m_new, acc


def _softmax_attend(q, key_blocks):
    tq = q.shape[0]
    carry = (jnp.full((1, tq), -jnp.inf, F32), jnp.zeros((MLA_VE, tq), F32))
    s_next = _scores(key_blocks[0][0], q)
    for n, (_, vts) in enumerate(key_blocks):
        s = s_next
        if n + 1 < len(key_blocks):
            s_next = _scores(key_blocks[n + 1][0], q)
        carry = _softmax_update(carry, s, vts)
    _, acc = carry
    return (acc[:MLA_V] * (1.0 / acc[MLA_V:MLA_V + 1])).astype(BF16)


def _flash_x_kernel(q_ref, kx_ref, kc_ref, vtx_ref, vtc_ref, o_ref):
    blocks = [(kc_ref[0], [vtc_ref[0]])]
    for j in range(N_XBLK // FLASH_KB):
        k = kx_ref[j * FLASH_KB:(j + 1) * FLASH_KB].reshape(FLASH_KB * KEY_BLK, HEAD_SLOT)
        blocks.append((k, [vtx_ref[j * FLASH_KB + t] for t in range(FLASH_KB)]))
    o_ref[...] = _softmax_attend(q_ref[...], blocks)


def _flash_c_kernel(q_ref, kc_ref, vtc_ref, o_ref):
    o_ref[...] = _softmax_attend(q_ref[...], [(kc_ref[0], [vtc_ref[0]])])


def _flash(q, k4, vt4, ctx_out):
    n_qx = SEQ // TQ
    ot_x = pl.pallas_call(
        _flash_x_kernel,
        out_shape=jax.ShapeDtypeStruct((MLA_WIDTH, ROWS_X), BF16),
        grid=(BATCH, MLA_HEADS, n_qx),
        in_specs=[
            pl.BlockSpec((None, TQ, HEAD_SLOT), lambda b, h, qi: (h, b * n_qx + qi, 0)),
            pl.BlockSpec((None, N_XBLK, KEY_BLK, HEAD_SLOT), lambda b, h, qi: (h, b, 0, 0)),
            pl.BlockSpec((None, 1, KEY_BLK, HEAD_SLOT), lambda b, h, qi: (h, CTX_BLK0 + b, 0, 0)),
            pl.BlockSpec((None, N_XBLK, MLA_VE, KEY_BLK), lambda b, h, qi: (h, b, 0, 0)),
            pl.BlockSpec((None, 1, MLA_VE, KEY_BLK), lambda b, h, qi: (h, CTX_BLK0 + b, 0, 0)),
        ],
        out_specs=pl.BlockSpec((MLA_V, TQ), lambda b, h, qi: (h, b * n_qx + qi)),
        compiler_params=_cparams(("arbitrary", "arbitrary", "arbitrary")),
        name="mla_flash_x",
    )(q, k4, k4, vt4, vt4)
    if not ctx_out:
        return ot_x, None
    ot_c = pl.pallas_call(
        _flash_c_kernel,
        out_shape=jax.ShapeDtypeStruct((MLA_WIDTH, ROWS_C), BF16),
        grid=(BATCH, MLA_HEADS),
        in_specs=[
            pl.BlockSpec((None, CTX_LEN, HEAD_SLOT), lambda b, h: (h, CTX_BLK0 + b, 0)),
            pl.BlockSpec((None, 1, KEY_BLK, HEAD_SLOT), lambda b, h: (h, CTX_BLK0 + b, 0, 0)),
            pl.BlockSpec((None, 1, MLA_VE, KEY_BLK), lambda b, h: (h, CTX_BLK0 + b, 0, 0)),
        ],
        out_specs=pl.BlockSpec((MLA_V, CTX_LEN), lambda b, h: (h, b)),
        compiler_params=_cparams(("arbitrary", "arbitrary")),
        name="mla_flash_c",
    )(q, k4, vt4)
    return ot_x, ot_c


def _out_mlp_kernel(h_ref, s5_ref, otx_ref, otc_ref, dn_ref, wo_s5_ref, wo_mla_ref, wo_dn_ref,
                    g2_ref, sh_ref, sc_ref, g5_ref, nw_ref, w1_ref, w2_ref, o_ref,
                    x1_ref, xn_ref, acc_ref):
    k = pl.program_id(1)

    @pl.when(k == 0)
    def _():
        ot = jnp.where(pl.program_id(0) >= ROWS_X // TM, otc_ref[...], otx_ref[...])
        mix = jnp.dot(s5_ref[...], wo_s5_ref[...], preferred_element_type=F32)
        mix += lax.dot_general(ot, wo_mla_ref[...], (((0,), (0,)), ((), ())), preferred_element_type=F32)
        mix += jnp.dot(dn_ref[...], wo_dn_ref[...], preferred_element_type=F32)
        x1 = h_ref[...] + g2_ref[...] * mix
        x1_ref[...] = x1
        xn_ref[...] = (_rms_rows(x1, nw_ref[...]) * (1.0 + sc_ref[...]) + sh_ref[...]).astype(BF16)
        acc_ref[...] = jnp.zeros_like(acc_ref)

    hid = jnp.dot(xn_ref[...], w1_ref[...], preferred_element_type=F32)
    hid = jnp.maximum(hid, 0.0)
    acc_ref[...] += jnp.dot((hid * hid).astype(BF16), w2_ref[...], preferred_element_type=F32)

    @pl.when(k == pl.num_programs(1) - 1)
    def _():
        o_ref[...] = x1_ref[...] + g5_ref[...] * acc_ref[...]


def _out_mlp(h, s5, ot_x, ot_c, dn, wo_s5, wo_mla, wo_dn, mods, nw, w1, w2, n_rows):
    n_xtiles = ROWS_X // TM
    if ot_c is None:
        ot_c = ot_x
    mod_spec = lambda j: pl.BlockSpec((None, None, 1, D_MODEL), lambda i, k: (_mod_index(i), j, 0, 0))
    full = lambda a: pl.BlockSpec(a.shape, lambda i, k: (0,) * a.ndim)
    return pl.pallas_call(
        _out_mlp_kernel,
        out_shape=jax.ShapeDtypeStruct((n_rows, D_MODEL), F32),
        grid=(n_rows // TM, D_FF // TF),
        in_specs=[
            pl.BlockSpec((TM, D_MODEL), lambda i, k: (i, 0)),
            pl.BlockSpec((TM, S5_WIDTH), lambda i, k: (i, 0)),
            pl.BlockSpec((MLA_WIDTH, TM), lambda i, k: (0, jnp.minimum(i, n_xtiles - 1))),
            pl.BlockSpec((MLA_WIDTH, TM), lambda i, k: (0, jnp.maximum(i - n_xtiles, 0))),
            pl.BlockSpec((TM, DN_WIDTH), lambda i, k: (i, 0)),
            full(wo_s5), full(wo_mla), full(wo_dn),
            mod_spec(2), mod_spec(3), mod_spec(4), mod_spec(5),
            full(nw),
            pl.BlockSpec((D_MODEL, TF), lambda i, k: (0, k)),
            pl.BlockSpec((TF, D_MODEL), lambda i, k: (k, 0)),
        ],
        out_specs=pl.BlockSpec((TM, D_MODEL), lambda i, k: (i, 0)),
        scratch_shapes=[
            pltpu.VMEM((TM, D_MODEL), F32),
            pltpu.VMEM((TM, D_MODEL), BF16),
            pltpu.VMEM((TM, D_MODEL), F32),
        ],
        compiler_params=_cparams(("arbitrary", "arbitrary")),
        name="out_mlp",
    )(h, s5, ot_x, ot_c, dn, wo_s5, wo_mla, wo_dn, mods, mods, mods, mods, nw, w1, w2)


S5_NSTATE = 2 * S5_GROUPS * S5_STATE
S5_SLABS = S5_NSTATE // LANES
S5_CHAINS = 2 * BATCH
S5_STEPS = CTX_LEN + SEQ
S5_TB = 128
S5_PITCH = S5_TB + 4


def _s5_kernel(*refs):
    u_refs = refs[:S5_CHAINS]
    flip_ref, bmat_ref, cmat_ref, are_ref, aim_ref, yf_ref, yb_ref, st_ref, hs_ref, h_ref = refs[S5_CHAINS:]

    @pl.when(pl.program_id(0) == 0)
    def _():
        h_ref[...] = jnp.zeros_like(h_ref)

    half = S5_SLABS // 2
    flip = flip_ref[...]
    for d in range(2):
        blocks = [u_refs[d * BATCH + b][...].astype(BF16) for b in range(BATCH)]
        if d == 1:
            blocks = [jnp.dot(flip, blk, preferred_element_type=F32).astype(BF16) for blk in blocks]
        bu = jnp.dot(jnp.concatenate(blocks, axis=0), bmat_ref[d], preferred_element_type=F32)
        for b in range(BATCH):
            row0 = (d * BATCH + b) * S5_PITCH
            for j in range(S5_SLABS):
                st_ref[j, row0:row0 + S5_TB, :] = bu[b * S5_TB:(b + 1) * S5_TB, j * LANES:(j + 1) * LANES]

    a_re = [are_ref[j] for j in range(half)]
    a_im = [aim_ref[j] for j in range(half)]

    def step(s, h):
        new = []
        rows = pl.ds(s, S5_CHAINS, stride=S5_PITCH)
        for j in range(half):
            hr, hi = h[j], h[half + j]
            nr = a_re[j] * hr - a_im[j] * hi + st_ref[j, rows, :]
            ni = a_re[j] * hi + a_im[j] * hr + st_ref[half + j, rows, :]
            hs_ref[j, rows, :] = nr
            hs_ref[half + j, rows, :] = ni
            new.append((nr, ni))
        return tuple(n[0] for n in new) + tuple(n[1] for n in new)

    h = lax.fori_loop(0, S5_TB, step, tuple(h_ref[j] for j in range(S5_SLABS)), unroll=4)
    for j in range(S5_SLABS):
        h_ref[j] = h[j]

    for d, y_ref in ((0, yf_ref), (1, yb_ref)):
        parts = []
        for b in range(BATCH):
            row0 = (d * BATCH + b) * S5_PITCH
            parts.append(jnp.concatenate([hs_ref[j, row0:row0 + S5_TB, :] for j in range(S5_SLABS)], axis=1))
        hm = jnp.concatenate(parts, axis=0).astype(BF16)
        y = jnp.dot(hm, cmat_ref[d], preferred_element_type=F32)
        for b in range(BATCH):
            yb = y[b * S5_TB:(b + 1) * S5_TB]
            y_ref[b] = _exact_cols_dot(flip, yb) if d == 1 else yb


def _s5_step_block(i, b, reverse):
    n_ctx = CTX_LEN // S5_TB
    n_x = SEQ // S5_TB
    ctx0 = ROWS_X // S5_TB + b * n_ctx
    if reverse:
        return jnp.where(i < n_ctx, ctx0 + n_ctx - 1 - i, b * n_x + n_x - 1 - (i - n_ctx))
    return jnp.where(i < n_ctx, ctx0 + i, b * n_x + (i - n_ctx))


def _s5_time_block(i, reverse):
    n_ctx = CTX_LEN // S5_TB
    n_all = S5_STEPS // S5_TB
    if reverse:
        return jnp.where(i < n_ctx, n_ctx - 1 - i, n_all - 1 - (i - n_ctx))
    return i


def _s5_scan_chains(u, bmat, cmat, a_re, a_im):
    full = lambda a: pl.BlockSpec(a.shape, lambda i: (0,) * a.ndim)
    flip = jnp.eye(S5_TB, dtype=BF16)[::-1]
    u_specs = [pl.BlockSpec((S5_TB, S5_WIDTH), functools.partial(lambda i, b, rev: (_s5_step_block(i, b, rev), 0),
                                                                b=c % BATCH, rev=c >= BATCH))
               for c in range(S5_CHAINS)]
    y_shape = jax.ShapeDtypeStruct((BATCH, S5_STEPS, S5_WIDTH), F32)
    return pl.pallas_call(
        _s5_kernel,
        out_shape=[y_shape, y_shape],
        grid=(S5_STEPS // S5_TB,),
        in_specs=u_specs + [full(flip), full(bmat), full(cmat), full(a_re), full(a_im)],
        out_specs=[pl.BlockSpec((BATCH, S5_TB, S5_WIDTH), lambda i: (0, _s5_time_block(i, False), 0)),
                   pl.BlockSpec((BATCH, S5_TB, S5_WIDTH), lambda i: (0, _s5_time_block(i, True), 0))],
        scratch_shapes=[pltpu.VMEM((S5_SLABS, S5_CHAINS * S5_PITCH, LANES), F32),
                        pltpu.VMEM((S5_SLABS, S5_CHAINS * S5_PITCH, LANES), F32),
                        pltpu.VMEM((S5_SLABS, S5_CHAINS, LANES), F32)],
        compiler_params=_cparams(("arbitrary",)),
        name="s5_scan",
    )(*([u] * S5_CHAINS), flip, bmat, cmat, a_re, a_im)


def _s5_glu_kernel(u_ref, yf_ref, yb_ref, d_ref, w_ref, o_ref):
    y = d_ref[...] * u_ref[...] + yf_ref[...] + yb_ref[...]
    z = jnp.dot(jax.nn.gelu(y).astype(BF16), w_ref[...], preferred_element_type=F32)
    o_ref[...] = (z[:, :S5_WIDTH] * jax.nn.sigmoid(z[:, S5_WIDTH:])).astype(BF16)


def _s5_glu(u, yf, yb, d_skip, glu_w):
    tiles_x = SEQ // CTX_LEN

    def y_idx(i):
        is_x = i < BATCH * tiles_x
        return (jnp.where(is_x, i // tiles_x, i - BATCH * tiles_x), jnp.where(is_x, 1 + i % tiles_x, 0), 0)

    row = pl.BlockSpec((CTX_LEN, S5_WIDTH), lambda i: (i, 0))
    y_spec = pl.BlockSpec((None, CTX_LEN, S5_WIDTH), y_idx)
    return pl.pallas_call(
        _s5_glu_kernel,
        out_shape=jax.ShapeDtypeStruct((ROWS, S5_WIDTH), BF16),
        grid=(ROWS // CTX_LEN,),
        in_specs=[row, y_spec, y_spec,
                  pl.BlockSpec((1, S5_WIDTH), lambda i: (0, 0)),
                  pl.BlockSpec((S5_WIDTH, 2 * S5_WIDTH), lambda i: (0, 0))],
        out_specs=row,
        compiler_params=_cparams(("arbitrary",)),
        name="s5_glu",
    )(u, yf, yb, d_skip, glu_w)


def _s5_params(lam_re, lam_im, log_dt, b_re, b_im, c_re, c_im):
    dt = jnp.exp(log_dt)[..., None]
    mag = jnp.exp(lam_re * dt)
    a_re, a_im = mag * jnp.cos(lam_im * dt), mag * jnp.sin(lam_im * dt)
    den = lam_re * lam_re + lam_im * lam_im
    f_re = ((a_re - 1.0) * lam_re + a_im * lam_im) / den
    f_im = (a_im * lam_re - (a_re - 1.0) * lam_im) / den
    bb_re = f_re[..., None] * b_re - f_im[..., None] * b_im
    bb_im = f_re[..., None] * b_im + f_im[..., None] * b_re
    eye = jnp.eye(S5_GROUPS, dtype=F32)
    blk_b = lambda m: jnp.einsum('dgph,gk->dghkp', m, eye).reshape(2, S5_WIDTH, S5_GROUPS * S5_STATE)
    blk_c = lambda m: jnp.einsum('dghp,gk->dgpkh', m, eye).reshape(2, S5_GROUPS * S5_STATE, S5_WIDTH)
    bmat = jnp.concatenate([blk_b(bb_re), blk_b(bb_im)], axis=2).astype(BF16)
    cmat = jnp.concatenate([blk_c(c_re), -blk_c(c_im)], axis=1).astype(BF16)
    per_chain = lambda a: jnp.repeat(a.reshape(2, S5_SLABS // 2, LANES), BATCH, axis=0).transpose(1, 0, 2)
    return bmat, cmat, per_chain(a_re), per_chain(a_im)


def _s5_mixer(u, lam_re, lam_im, log_dt, b_re, b_im, c_re, c_im, d_skip, glu_w):
    bmat, cmat, a_re, a_im = _s5_params(lam_re, lam_im, log_dt, b_re, b_im, c_re, c_im)
    y_f, y_b = _s5_scan_chains(u, bmat, cmat, a_re, a_im)
    return _s5_glu(u, y_f, y_b, d_skip[None], glu_w.astype(BF16))


DN_TILE = 256
DN_NCHUNK = DN_TILE // DN_CHUNK
DN_HALO = 8
DN_XTILES = SEQ // DN_TILE
DN_BASE = 8


def _exact_rows_dot(a, e):
    a_hi = a.astype(BF16)
    r = a - a_hi.astype(F32)
    a_mid = r.astype(BF16)
    a_lo = (r - a_mid.astype(F32)).astype(BF16)
    out = jnp.dot(a_hi, e, preferred_element_type=F32)
    out += jnp.dot(a_mid, e, preferred_element_type=F32)
    out += jnp.dot(a_lo, e, preferred_element_type=F32)
    return out


def _exact_cols_dot(e, a):
    a_hi = a.astype(BF16)
    r = a - a_hi.astype(F32)
    a_mid = r.astype(BF16)
    a_lo = (r - a_mid.astype(F32)).astype(BF16)
    out = jnp.dot(e, a_hi, preferred_element_type=F32)
    out += jnp.dot(e, a_mid, preferred_element_type=F32)
    out += jnp.dot(e, a_lo, preferred_element_type=F32)
    return out


def _dn_prep_kernel(x_ref, prev_ref, next_ref, ba_ref, cw_ref, nega_ref, dtb_ref, hsum_ref, expand_ref,
                    ltri_ref, utri_ref, qn_ref, kn_ref, v_ref, bf_ref, bb_ref, gf_ref, gb_ref, xe_ref):
    i = pl.program_id(0)
    is_ctx = i >= ROWS_X // DN_TILE
    first = jnp.logical_or(is_ctx, i % DN_XTILES == 0)
    last = jnp.logical_or(is_ctx, i % DN_XTILES == DN_XTILES - 1)
    xe_ref[0:DN_HALO, :] = jnp.where(first, 0.0, prev_ref[...])
    xe_ref[DN_HALO:DN_HALO + DN_TILE, :] = x_ref[...]
    xe_ref[DN_HALO + DN_TILE:, :] = jnp.where(last, 0.0, next_ref[...])
    pad = DN_CONV // 2
    y = jnp.zeros((DN_TILE, DN_QKV), F32)
    for j in range(DN_CONV):
        off = DN_HALO + j - pad
        y += cw_ref[j:j + 1, :] * xe_ref[off:off + DN_TILE, :]
    y = y * jax.nn.sigmoid(y)
    qk_w = DN_HEADS * DN_DK
    q, k = y[:, :qk_w], y[:, qk_w:2 * qk_w]
    hsum = hsum_ref[...]
    qn_ref[...] = q * lax.rsqrt(_exact_rows_dot(q * q, hsum) + EPS) * (DN_DK ** -0.5)
    kn_ref[...] = k * lax.rsqrt(_exact_rows_dot(k * k, hsum) + EPS)
    v_ref[...] = y[:, 2 * qk_w:]

    ba = ba_ref[...]
    lane = lax.broadcasted_iota(jnp.int32, (1, LANES), 1)
    z = ba + dtb_ref[...]
    softplus = jnp.maximum(z, 0.0) + jnp.log(1.0 + jnp.exp(-jnp.abs(z)))
    bg = jnp.where(lane < 2 * DN_HEADS, jax.nn.sigmoid(ba), nega_ref[...] * softplus)
    ex = _exact_rows_dot(bg, expand_ref[...])
    w = DN_WIDTH
    bf_ref[...] = ex[:, 0:w]
    bb_ref[...] = ex[:, w:2 * w]
    gf_ref[...] = _exact_cols_dot(ltri_ref[...], ex[:, 2 * w:3 * w])
    gb_ref[...] = _exact_cols_dot(utri_ref[...], ex[:, 3 * w:4 * w])


def _dn_prep(qkv, ba, conv_w, a_log, dt_bias):
    n_halo_blocks = ROWS // DN_HALO
    per_tile = DN_TILE // DN_HALO
    cw = jnp.concatenate([conv_w, jnp.zeros((8 - DN_CONV, DN_QKV), F32)], axis=0)
    lanes_g = jnp.zeros((LANES,), F32)
    nega = lanes_g.at[2 * DN_HEADS:4 * DN_HEADS].set(-jnp.exp(a_log.reshape(-1)))[None]
    dtb = lanes_g.at[2 * DN_HEADS:4 * DN_HEADS].set(dt_bias.reshape(-1))[None]
    head_of_lane = jnp.arange(DN_WIDTH) // DN_DV
    hsum = (head_of_lane[:, None] == head_of_lane[None, :]).astype(BF16)
    src = jnp.arange(LANES)[:, None]
    expand = jnp.concatenate(
        [(src == (g * DN_HEADS + head_of_lane)[None, :]) for g in range(4)], axis=1).astype(BF16)
    r = jnp.arange(DN_TILE)
    same_chunk = (r[:, None] // DN_CHUNK) == (r[None, :] // DN_CHUNK)
    ltri = (same_chunk & (r[:, None] >= r[None, :])).astype(BF16)
    utri = (same_chunk & (r[:, None] <= r[None, :])).astype(BF16)
    full = lambda a: pl.BlockSpec(a.shape, lambda i: (0,) * a.ndim)
    row = lambda n: pl.BlockSpec((DN_TILE, n), lambda i: (i, 0))
    return pl.pallas_call(
        _dn_prep_kernel,
        out_shape=[jax.ShapeDtypeStruct((ROWS, DN_WIDTH), F32)] * 7,
        grid=(ROWS // DN_TILE,),
        in_specs=[
            row(DN_QKV),
            pl.BlockSpec((DN_HALO, DN_QKV), lambda i: (jnp.maximum(i * per_tile - 1, 0), 0)),
            pl.BlockSpec((DN_HALO, DN_QKV), lambda i: (jnp.minimum((i + 1) * per_tile, n_halo_blocks - 1), 0)),
            row(LANES),
            full(cw), full(nega), full(dtb), full(hsum), full(expand), full(ltri), full(utri),
        ],
        out_specs=[row(DN_WIDTH)] * 7,
        scratch_shapes=[pltpu.VMEM((DN_TILE + 2 * DN_HALO, DN_QKV), F32)],
        compiler_params=_cparams(("arbitrary",)),
        name="dn_prep",
    )(qkv, qkv, qkv, ba, cw, nega, dtb, hsum, expand, ltri, utri)


def _dn_bd(y, blockmask):
    yb = y.astype(BF16)
    return jnp.where(blockmask, jnp.concatenate([yb] * DN_HEADS, axis=0), jnp.zeros((), BF16))


def _dn_prepare(chunks, blockmask, eye_cat, blk8, off_masks):
    dot = functools.partial(jnp.dot, preferred_element_type=F32)
    bd = lambda y: _dn_bd(y, blockmask)

    def mm(x, y):
        x_hi = x.astype(BF16)
        x_lo = (x - x_hi.astype(F32)).astype(BF16)
        y_hi = y.astype(BF16)
        y_lo = y - y_hi.astype(F32)
        r = dot(jnp.concatenate([x_hi, x_lo], axis=0), bd(y_hi))
        return r[:DN_CHUNK] + r[DN_CHUNK:] + dot(x_hi, bd(y_lo))

    a_mats, attns, egs = [], [], []
    for qn, kn, v, beta, gc, incl, strict, last in chunks:
        gcrow = jnp.sum(jnp.where(eye_cat, gc, 0.0), axis=0, keepdims=True)
        decay = jnp.exp(jnp.where(incl, gc - gcrow, -jnp.inf))
        grams = lax.dot_general(jnp.concatenate([kn, qn], axis=0).astype(BF16), bd(kn),
                                (((1,), (1,)), ((), ())), preferred_element_type=F32)
        a_mats.append(jnp.where(strict, grams[:DN_CHUNK] * beta * decay, 0.0))
        attns.append((grams[DN_CHUNK:] * decay).astype(BF16))
        egs.append(jnp.exp(gc))

    n_pows = [jnp.where(blk8, -a, 0.0) for a in a_mats]
    t_mats = [jnp.where(eye_cat, 1.0, 0.0) + n for n in n_pows]
    for _ in range(2):
        n_pows = [mm(n, n) for n in n_pows]
        t_mats = [t + mm(t, n) for t, n in zip(t_mats, n_pows)]
    mm1 = lambda x, y: dot(x.astype(BF16), bd(y))
    for off_mask in off_masks:
        z = [mm1(t, jnp.where(off_mask, a, 0.0)) for t, a in zip(t_mats, a_mats)]
        t_mats = [t - mm1(zi, t) for t, zi in zip(t_mats, z)]

    out = []
    for (qn, kn, v, beta, gc, incl, strict, last), t, attn, eg in zip(chunks, t_mats, attns, egs):
        tb = t.astype(BF16)
        glast = gc[last:last + 1, :]
        w_v = dot(tb, bd(v * beta))
        w_k = dot(tb, bd(kn * beta * eg))
        lhs = jnp.concatenate([w_k, qn * eg], axis=0).astype(BF16)
        k_dec = (kn * jnp.exp(glast - gc)).astype(BF16)
        out.append((w_v, lhs, attn, k_dec, jnp.exp(glast)))
    return out


def _dn_state_step(prep, s_bd, blockmask):
    w_v, lhs, attn, k_dec, e_last = prep
    dot = functools.partial(jnp.dot, preferred_element_type=F32)
    ps = dot(lhs, s_bd.astype(BF16))
    v_new = w_v - ps[:DN_CHUNK]
    o = ps[DN_CHUNK:] + dot(attn, _dn_bd(v_new, blockmask))
    upd = lax.dot_general(k_dec, v_new.astype(BF16), (((0,), (0,)), ((), ())), preferred_element_type=F32)
    return o, s_bd * e_last + jnp.where(blockmask, upd, 0.0)


def _dn_chunks_kernel(qf_ref, kf_ref, vf_ref, bf_ref, gf_ref, qb_ref, kb_ref, vb_ref, bb_ref, gb_ref,
                      of_ref, ob_ref, s_ref):
    @pl.when(pl.program_id(1) == 0)
    def _():
        s_ref[...] = jnp.zeros_like(s_ref)

    row = lax.broadcasted_iota(jnp.int32, (DN_WIDTH, DN_WIDTH), 0)
    col = lax.broadcasted_iota(jnp.int32, (DN_WIDTH, DN_WIDTH), 1)
    blockmask = (row // DN_DV) == (col // DN_DV)
    i = lax.broadcasted_iota(jnp.int32, (DN_CHUNK, DN_WIDTH), 0)
    j = lax.broadcasted_iota(jnp.int32, (DN_CHUNK, DN_WIDTH), 1) % DN_DV
    eye_cat = i == j
    blk8 = (i // DN_BASE) == (j // DN_BASE)
    off_masks = tuple(((i // (2 * s)) == (j // (2 * s))) & ((i // s) != (j // s))
                      for s in (DN_BASE, 2 * DN_BASE, 4 * DN_BASE))
    chunks, where = [], []
    for c in range(DN_NCHUNK):
        rf = slice(c * DN_CHUNK, (c + 1) * DN_CHUNK)
        chunks.append((qf_ref[rf, :], kf_ref[rf, :], vf_ref[rf, :], bf_ref[rf, :], gf_ref[rf, :],
                       i >= j, i > j, DN_CHUNK - 1))
        where.append((0, of_ref, rf))
        cb = DN_NCHUNK - 1 - c
        rb = slice(cb * DN_CHUNK, (cb + 1) * DN_CHUNK)
        chunks.append((qb_ref[rb, :], kb_ref[rb, :], vb_ref[rb, :], bb_ref[rb, :], gb_ref[rb, :],
                       i <= j, i < j, 0))
        where.append((1, ob_ref, rb))
    prepared = _dn_prepare(chunks, blockmask, eye_cat, blk8, off_masks)

    states = [s_ref[0], s_ref[1]]
    for prep, (d, o_ref, rows) in zip(prepared, where):
        o, states[d] = _dn_state_step(prep, states[d], blockmask)
        o_ref[rows, :] = o
    s_ref[0] = states[0]
    s_ref[1] = states[1]


def _dn_chunks(qn, kn, v, beta_f, beta_b, gc_f, gc_b):
    ctx_tile0 = ROWS_X // DN_TILE

    def fwd_idx(b, t):
        return (jnp.where(t == 0, ctx_tile0 + b, b * DN_XTILES + t - 1), 0)

    def bwd_idx(b, t):
        return (jnp.where(t == 0, ctx_tile0 + b, b * DN_XTILES + DN_XTILES - t), 0)

    spec_f = pl.BlockSpec((DN_TILE, DN_WIDTH), fwd_idx)
    spec_b = pl.BlockSpec((DN_TILE, DN_WIDTH), bwd_idx)
    return pl.pallas_call(
        _dn_chunks_kernel,
        out_shape=[jax.ShapeDtypeStruct((ROWS, DN_WIDTH), F32)] * 2,
        grid=(BATCH, DN_XTILES + 1),
        in_specs=[spec_f] * 5 + [spec_b] * 5,
        out_specs=[spec_f, spec_b],
        scratch_shapes=[pltpu.VMEM((2, DN_WIDTH, DN_WIDTH), F32)],
        compiler_params=_cparams(("arbitrary", "arbitrary")),
        name="dn_chunks",
    )(qn, kn, v, beta_f, gc_f, qn, kn, v, beta_b, gc_b)


def _dn_out_kernel(of_ref, ob_ref, gt_ref, nw_ref, hsum_ref, o_ref):
    o = of_ref[...] + ob_ref[...]
    ms = _exact_rows_dot(o * o, hsum_ref[...]) * (1.0 / DN_DV)
    gt = gt_ref[...]
    o_ref[...] = (o * lax.rsqrt(ms + EPS) * nw_ref[...] * (gt * jax.nn.sigmoid(gt))).astype(BF16)


def _dn_out(o_f, o_b, gt, norm_w):
    head_of_lane = jnp.arange(DN_WIDTH) // DN_DV
    hsum = (head_of_lane[:, None] == head_of_lane[None, :]).astype(BF16)
    nw = jnp.tile(norm_w, DN_HEADS)[None]
    row = pl.BlockSpec((TM, DN_WIDTH), lambda i: (i, 0))
    return pl.pallas_call(
        _dn_out_kernel,
        out_shape=jax.ShapeDtypeStruct((ROWS, DN_WIDTH), BF16),
        grid=(ROWS // TM,),
        in_specs=[row, row, row, pl.BlockSpec((1, DN_WIDTH), lambda i: (0, 0)),
                  pl.BlockSpec((DN_WIDTH, DN_WIDTH), lambda i: (0, 0))],
        out_specs=row,
        compiler_params=_cparams(("arbitrary",)),
        name="dn_out",
    )(o_f, o_b, gt, nw, hsum)


def _dn_mixer(qkv, gt, ba, conv_w, a_log, dt_bias, norm_w):
    qn, kn, v, beta_f, beta_b, gc_f, gc_b = _dn_prep(qkv, ba, conv_w, a_log, dt_bias)
    o_f, o_b = _dn_chunks(qn, kn, v, beta_f, beta_b, gc_f, gc_b)
    return _dn_out(o_f, o_b, gt, norm_w)


def _pad_w_in(w):
    d = w.shape[0]
    s = [0]
    for n in IN_SIZES:
        s.append(s[-1] + n)
    u, cq, ckv, kr, qkv, gt, ba = (w[:, s[i]:s[i + 1]] for i in range(7))
    z = lambda n: jnp.zeros((d, n), w.dtype)
    return jnp.concatenate(
        [u, cq, ckv, z(KR_LANE), kr, z(LANES - KR_LANE - MLA_ROPE), qkv, gt, ba, z(LANES - 4 * DN_HEADS)], axis=1)


def _head_slots(w, width):
    kdim = w.shape[0]
    w = w.reshape(kdim, MLA_HEADS, width)
    return jnp.pad(w, ((0, 0), (0, 0), (0, HEAD_SLOT - width))).reshape(kdim, MLA_HEADS * HEAD_SLOT)


def _rope_swap(w):
    lane = jnp.arange(HEAD_SLOT)
    half = MLA_ROPE // 2
    src = jnp.where(lane < MLA_NOPE + half, lane + half, lane - half)
    in_rope = (lane >= MLA_NOPE) & (lane < MLA_DH)
    w3 = w.reshape(w.shape[0], -1, HEAD_SLOT)
    return jnp.where(in_rope, w3[:, :, jnp.clip(src, 0, HEAD_SLOT - 1)], jnp.zeros((), w.dtype)).reshape(w.shape)


def _rope_tables():
    rows = SEQ // GRID_W
    row = jnp.repeat(jnp.arange(rows, dtype=F32), GRID_W)
    col = jnp.tile(jnp.arange(GRID_W, dtype=F32), rows)
    inv = ROPE_BASE ** (-jnp.arange(ROPE_FREQS, dtype=F32) / ROPE_FREQS)
    ang = jnp.concatenate([row[:, None] * inv, col[:, None] * inv], axis=-1)
    cos, sin = jnp.cos(ang), jnp.sin(ang)
    half = MLA_ROPE // 2
    ones = lambda n: jnp.ones((SEQ, n), F32)
    zeros = lambda n: jnp.zeros((SEQ, n), F32)
    cos_t = jnp.concatenate([ones(MLA_NOPE), cos, cos, ones(HEAD_SLOT - MLA_DH)], axis=1)
    sin_t = jnp.concatenate([zeros(MLA_NOPE), -sin, sin, zeros(HEAD_SLOT - MLA_DH)], axis=1)
    cos_t = jnp.concatenate([jnp.tile(cos_t, (BATCH, 1)), jnp.ones((ROWS_C, HEAD_SLOT), F32)], axis=0)
    sin_t = jnp.concatenate([jnp.tile(sin_t, (BATCH, 1)), jnp.zeros((ROWS_C, HEAD_SLOT), F32)], axis=0)
    del half
    return cos_t, sin_t


def _split_rows(a, width):
    return a[:ROWS_X].reshape(BATCH, SEQ, width), a[ROWS_X:].reshape(BATCH, CTX_LEN, width)


def _join_rows(ax, ac, width):
    return jnp.concatenate([ax.reshape(ROWS_X, width), ac.reshape(ROWS_C, width)], axis=0)


def kernel(x, c, ctx, c_ctx, w_ada, b_ada, norm1_w, w_in, s5_lambda_re, s5_lambda_im, s5_log_dt, s5_b_re, s5_b_im, s5_c_re, s5_c_im, s5_d, s5_glu_w, mla_q_norm_w, mla_kv_norm_w, mla_w_uq, mla_w_ukv, mla_q_gain, mla_k_gain, dn_conv_w, dn_a_log, dn_dt_bias, dn_norm_w, w_out, norm2_w, w_ff1, w_ff2):
    h = jnp.concatenate([x.reshape(ROWS_X, D_MODEL), ctx.reshape(ROWS_C, D_MODEL)], axis=0)
    c_pad = jnp.concatenate([c, c_ctx[None], jnp.zeros((3, D_MODEL), F32)], axis=0)
    mods_all = _ada_all(c_pad, w_ada, b_ada)
    cos_t, sin_t = _rope_tables()

    for l in range(DEPTH):
        ctx_out = l < DEPTH - 1
        mods = mods_all[l, :BATCH + 1].reshape(BATCH + 1, 6, 1, D_MODEL)
        u, mla_in, qkv, gt, ba = _in_proj(h, norm1_w[l][None], mods, _pad_w_in(w_in[l]).astype(BF16))

        s5 = _s5_mixer(u, s5_lambda_re[l], s5_lambda_im[l], s5_log_dt[l], s5_b_re[l], s5_b_im[l],
                       s5_c_re[l], s5_c_im[l], s5_d[l], s5_glu_w[l])

        wq = _head_slots(mla_w_uq[l], MLA_DH)
        w_ukv = mla_w_ukv[l].reshape(MLA_KV_RANK, MLA_HEADS, MLA_NOPE + MLA_V)
        wk = _head_slots(w_ukv[:, :, :MLA_NOPE].reshape(MLA_KV_RANK, MLA_HEADS * MLA_NOPE), MLA_NOPE).astype(BF16)
        wvt = w_ukv[:, :, MLA_NOPE:].reshape(MLA_KV_RANK, MLA_WIDTH).T.astype(BF16)
        qg, kg = (jnp.pad(g, (0, HEAD_SLOT - MLA_DH))[None] for g in (mla_q_gain[l], mla_k_gain[l]))
        q, k4, vt4 = _mla_proj(mla_in, cos_t, sin_t, mla_q_norm_w[l][None], mla_kv_norm_w[l][None],
                               wq.astype(BF16), _rope_swap(wq).astype(BF16), wk, wvt,
                               qg, _rope_swap(qg), kg, _rope_swap(kg), _rope_swap(jnp.eye(HEAD_SLOT, dtype=BF16)))
        ot_x, ot_c = _flash(q, k4, vt4, ctx_out)

        dn = _dn_mixer(qkv, gt, ba, dn_conv_w[l], dn_a_log[l], dn_dt_bias[l], dn_norm_w[l])

        wo = w_out[l].astype(BF16)
        n_rows = ROWS if ctx_out else ROWS_X
        h = _out_mlp(h, s5, ot_x, ot_c, dn, wo[:S5_WIDTH], wo[S5_WIDTH:S5_WIDTH + MLA_WIDTH], wo[S5_WIDTH + MLA_WIDTH:],
                     mods, norm2_w[l][None], w_ff1[l].astype(BF16), w_ff2[l].astype(BF16), n_rows)
    return h.reshape(BATCH, SEQ, D_MODEL)
```

```python
import functools
import math

import jax
import jax.numpy as jnp
from jax import lax
from jax.experimental import pallas as pl
from jax.experimental.pallas import tpu as pltpu

F32 = jnp.float32
BF16 = jnp.bfloat16

D_MODEL = 1024
BATCH = 4
SEQ = 4096
DEPTH = 4
GRID_W = 64
CTX_LEN = 256
EPS = 1e-6

S5_GROUP = 16
S5_GROUPS = 16
S5_WIDTH = 256
S5_STATE = 64

MLA_HEADS = 8
MLA_NOPE = 64
MLA_ROPE = 32
MLA_V = 64
MLA_VE = MLA_V + 16
MLA_DH = 96
MLA_Q_RANK = 256
MLA_KV_RANK = 128
MLA_WIDTH = 512
ROPE_BASE = 10000.0
ROPE_FREQS = 8

DN_HEADS = 4
DN_DK = 64
DN_DV = 64
DN_QKV = 768
DN_WIDTH = 256
DN_CONV = 5
DN_CHUNK = 64

D_FF = 4 * D_MODEL
IN_SIZES = (S5_WIDTH, MLA_Q_RANK, MLA_KV_RANK, MLA_ROPE, DN_QKV, DN_WIDTH, 4 * DN_HEADS)

ROWS_X = BATCH * SEQ
ROWS_C = BATCH * CTX_LEN
ROWS = ROWS_X + ROWS_C
LANES = 128
HEAD_SLOT = LANES
KEY_BLK = 256
N_XBLK = SEQ // KEY_BLK
CTX_BLK0 = ROWS_X // KEY_BLK

IN_PAD = 1920
COL_U, COL_CQ, COL_CKV, COL_KR, COL_QKV, COL_GT, COL_BA = 0, 256, 512, 640, 768, 1536, 1792
KR_LANE = MLA_NOPE

TM = 512
TF = D_FF
TQ = 4096
FLASH_KB = 2
VMEM_LIMIT = 56 * 1024 * 1024

LOG2E = 1.4426950408889634


def _cparams(sem):
    return pltpu.CompilerParams(dimension_semantics=sem, vmem_limit_bytes=VMEM_LIMIT)


def _split3_dot(a, w):
    a_hi = a.astype(BF16)
    a_lo = (a - a_hi.astype(F32)).astype(BF16)
    w_hi = w.astype(BF16)
    w_lo = (w - w_hi.astype(F32)).astype(BF16)
    out = jnp.dot(a_hi, w_hi, preferred_element_type=F32)
    out += jnp.dot(a_lo, w_hi, preferred_element_type=F32)
    out += jnp.dot(a_hi, w_lo, preferred_element_type=F32)
    return out


def _rms_rows(x, w):
    return x * lax.rsqrt(jnp.mean(x * x, axis=-1, keepdims=True) + EPS) * w


def _ada_kernel(c_ref, w_ref, b_ref, o_ref):
    c = c_ref[...]
    s = c * jax.nn.sigmoid(c)
    o_ref[...] = _split3_dot(s, w_ref[...]) + b_ref[...]


def _ada_all(c_pad, w_ada, b_ada):
    tn = 1536
    return pl.pallas_call(
        _ada_kernel,
        out_shape=jax.ShapeDtypeStruct((DEPTH, 8, 6 * D_MODEL), F32),
        grid=(DEPTH, 6 * D_MODEL // tn),
        in_specs=[
            pl.BlockSpec((8, D_MODEL), lambda l, j: (0, 0)),
            pl.BlockSpec((None, D_MODEL, tn), lambda l, j: (l, 0, j)),
            pl.BlockSpec((None, 1, tn), lambda l, j: (l, 0, j)),
        ],
        out_specs=pl.BlockSpec((None, 8, tn), lambda l, j: (l, 0, j)),
        compiler_params=_cparams(("arbitrary", "arbitrary")),
        name="ada_mod",
    )(c_pad, w_ada, b_ada.reshape(DEPTH, 1, 6 * D_MODEL))


def _mod_index(i):
    return jnp.minimum(i // (SEQ // TM), BATCH)


def _in_proj_kernel(h_ref, nw_ref, sh_ref, sc_ref, w_ref, u_ref, mla_ref, qkv_ref, gt_ref, ba_ref):
    y = _rms_rows(h_ref[...], nw_ref[...]) * (1.0 + sc_ref[...]) + sh_ref[...]
    z = jnp.dot(y.astype(BF16), w_ref[...], preferred_element_type=F32)
    u_ref[...] = z[:, COL_U:COL_CQ]
    mla_ref[...] = z[:, COL_CQ:COL_QKV]
    qkv_ref[...] = z[:, COL_QKV:COL_GT]
    gt_ref[...] = z[:, COL_GT:COL_BA]
    ba_ref[...] = z[:, COL_BA:IN_PAD]


def _in_proj(h, nw, mods, w_pad):
    widths = (256, 512, 768, 256, 128)
    mod_spec = lambda j: pl.BlockSpec((None, None, 1, D_MODEL), lambda i: (_mod_index(i), j, 0, 0))
    return pl.pallas_call(
        _in_proj_kernel,
        out_shape=[jax.ShapeDtypeStruct((ROWS, n), F32) for n in widths],
        grid=(ROWS // TM,),
        in_specs=[
            pl.BlockSpec((TM, D_MODEL), lambda i: (i, 0)),
            pl.BlockSpec((1, D_MODEL), lambda i: (0, 0)),
            mod_spec(0),
            mod_spec(1),
            pl.BlockSpec((D_MODEL, IN_PAD), lambda i: (0, 0)),
        ],
        out_specs=[pl.BlockSpec((TM, n), lambda i: (i, 0)) for n in widths],
        compiler_params=_cparams(("arbitrary",)),
        name="in_proj",
    )(h, nw, mods, mods, w_pad)


def _mla_proj_kernel(m_ref, cos_ref, sin_ref, qnw_ref, kvnw_ref, wq_ref, wqs_ref, wk_ref, wvt_ref,
                     qg_ref, qgs_ref, kg_ref, kgs_ref, perm_ref, q_ref, k_ref, vt_ref):
    m = m_ref[...]
    cqn = _rms_rows(m[:, 0:MLA_Q_RANK], qnw_ref[...]).astype(BF16)
    ckvn = _rms_rows(m[:, MLA_Q_RANK:MLA_Q_RANK + MLA_KV_RANK], kvnw_ref[...]).astype(BF16)
    kr = m[:, MLA_Q_RANK + MLA_KV_RANK:]
    qf = jnp.dot(cqn, wq_ref[...], preferred_element_type=F32)
    qf_sw = jnp.dot(cqn, wqs_ref[...], preferred_element_type=F32)
    kf = jnp.dot(ckvn, wk_ref[...], preferred_element_type=F32)
    vt = lax.dot_general(wvt_ref[...], ckvn, (((1,), (1,)), ((), ())), preferred_element_type=F32)
    kr_sw = _exact_rows_dot(kr, perm_ref[...])
    cos = cos_ref[...]
    sin = sin_ref[...]
    q_cos, q_sin = cos * qg_ref[...], sin * qgs_ref[...]
    k_cos, k_sin = cos * kg_ref[...], sin * kgs_ref[...]

    def head_norm_rope(xh, xh_sw, g_cos, g_sin, scale):
        ss = jnp.sum(xh * xh, axis=-1, keepdims=True) * (1.0 / MLA_DH)
        return (xh * g_cos + xh_sw * g_sin) * (lax.rsqrt(ss + EPS) * scale)

    n_sub = m.shape[0] // KEY_BLK
    ones = jnp.ones((MLA_VE - MLA_V, m.shape[0]), BF16)
    for h in range(MLA_HEADS):
        sl = slice(h * HEAD_SLOT, (h + 1) * HEAD_SLOT)
        q_ref[h] = head_norm_rope(qf[:, sl], qf_sw[:, sl], q_cos, q_sin, MLA_DH ** -0.5 * LOG2E).astype(BF16)
        kh = head_norm_rope(kf[:, sl] + kr, kr_sw, k_cos, k_sin, 1.0).astype(BF16)
        vh = jnp.concatenate([vt[h * MLA_V:(h + 1) * MLA_V, :].astype(BF16), ones], axis=0)
        for s in range(n_sub):
            k_ref[h, s] = kh[s * KEY_BLK:(s + 1) * KEY_BLK, :]
            vt_ref[h, s] = vh[:, s * KEY_BLK:(s + 1) * KEY_BLK]


def _mla_proj(mla_in, cos_t, sin_t, qnw, kvnw, wq, wq_sw, wk, wvt, qg, qg_sw, kg, kg_sw, perm):
    n_sub = TM // KEY_BLK
    full = lambda a: pl.BlockSpec(a.shape, lambda i: (0,) * a.ndim)
    return pl.pallas_call(
        _mla_proj_kernel,
        out_shape=[
            jax.ShapeDtypeStruct((MLA_HEADS, ROWS, HEAD_SLOT), BF16),
            jax.ShapeDtypeStruct((MLA_HEADS, ROWS // KEY_BLK, KEY_BLK, HEAD_SLOT), BF16),
            jax.ShapeDtypeStruct((MLA_HEADS, ROWS // KEY_BLK, MLA_VE, KEY_BLK), BF16),
        ],
        grid=(ROWS // TM,),
        in_specs=[
            pl.BlockSpec((TM, 512), lambda i: (i, 0)),
            pl.BlockSpec((TM, HEAD_SLOT), lambda i: (i, 0)),
            pl.BlockSpec((TM, HEAD_SLOT), lambda i: (i, 0)),
            full(qnw), full(kvnw), full(wq), full(wq_sw), full(wk), full(wvt),
            full(qg), full(qg_sw), full(kg), full(kg_sw), full(perm),
        ],
        out_specs=[
            pl.BlockSpec((MLA_HEADS, TM, HEAD_SLOT), lambda i: (0, i, 0)),
            pl.BlockSpec((MLA_HEADS, n_sub, KEY_BLK, HEAD_SLOT), lambda i: (0, i, 0, 0)),
            pl.BlockSpec((MLA_HEADS, n_sub, MLA_VE, KEY_BLK), lambda i: (0, i, 0, 0)),
        ],
        compiler_params=_cparams(("arbitrary",)),
        name="mla_proj",
    )(mla_in, cos_t, sin_t, qnw, kvnw, wq, wq_sw, wk, wvt, qg, qg_sw, kg, kg_sw, perm)


def _scores(k, q):
    return lax.dot_general(k, q, (((1,), (1,)), ((), ())), preferred_element_type=F32)


def _softmax_update(carry, s, vts):
    m, acc = carry
    m_new = jnp.maximum(m, jnp.max(s, axis=0, keepdims=True))
    pb = jnp.exp2(s - m_new).astype(BF16)
    acc = jnp.exp2(m - m_new) * acc
    for t, vt in enumerate(vts):
        acc += jnp.dot(vt, pb[t * KEY_BLK:(t + 1) * KEY_BLK], preferred_element_type=F32)
    return m_new, acc


def _softmax_attend(q, key_blocks):
    tq = q.shape[0]
    carry = (jnp.full((1, tq), -jnp.inf, F32), jnp.zeros((MLA_VE, tq), F32))
    s_next = _scores(key_blocks[0][0], q)
    for n, (_, vts) in enumerate(key_blocks):
        s = s_next
        if n + 1 < len(key_blocks):
            s_next = _scores(key_blocks[n + 1][0], q)
        carry = _softmax_update(carry, s, vts)
    _, acc = carry
    return (acc[:MLA_V] * (1.0 / acc[MLA_V:MLA_V + 1])).astype(BF16)


def _flash_x_kernel(q_ref, kx_ref, kc_ref, vtx_ref, vtc_ref, o_ref):
    blocks = [(kc_ref[0], [vtc_ref[0]])]
    for j in range(N_XBLK // FLASH_KB):
        k = kx_ref[j * FLASH_KB:(j + 1) * FLASH_KB].reshape(FLASH_KB * KEY_BLK, HEAD_SLOT)
        blocks.append((k, [vtx_ref[j * FLASH_KB + t] for t in range(FLASH_KB)]))
    o_ref[...] = _softmax_attend(q_ref[...], blocks)


def _flash_c_kernel(q_ref, kc_ref, vtc_ref, o_ref):
    o_ref[...] = _softmax_attend(q_ref[...], [(kc_ref[0], [vtc_ref[0]])])


def _flash(q, k4, vt4, ctx_out):
    n_qx = SEQ // TQ
    ot_x = pl.pallas_call(
        _flash_x_kernel,
        out_shape=jax.ShapeDtypeStruct((MLA_WIDTH, ROWS_X), BF16),
        grid=(BATCH, MLA_HEADS, n_qx),
        in_specs=[
            pl.BlockSpec((None, TQ, HEAD_SLOT), lambda b, h, qi: (h, b * n_qx + qi, 0)),
            pl.BlockSpec((None, N_XBLK, KEY_BLK, HEAD_SLOT), lambda b, h, qi: (h, b, 0, 0)),
            pl.BlockSpec((None, 1, KEY_BLK, HEAD_SLOT), lambda b, h, qi: (h, CTX_BLK0 + b, 0, 0)),
            pl.BlockSpec((None, N_XBLK, MLA_VE, KEY_BLK), lambda b, h, qi: (h, b, 0, 0)),
            pl.BlockSpec((None, 1, MLA_VE, KEY_BLK), lambda b, h, qi: (h, CTX_BLK0 + b, 0, 0)),
        ],
        out_specs=pl.BlockSpec((MLA_V, TQ), lambda b, h, qi: (h, b * n_qx + qi)),
        compiler_params=_cparams(("arbitrary", "arbitrary", "arbitrary")),
        name="mla_flash_x",
    )(q, k4, k4, vt4, vt4)
    if not ctx_out:
        return ot_x, None
    ot_c = pl.pallas_call(
        _flash_c_kernel,
        out_shape=jax.ShapeDtypeStruct((MLA_WIDTH, ROWS_C), BF16),
        grid=(BATCH, MLA_HEADS),
        in_specs=[
            pl.BlockSpec((None, CTX_LEN, HEAD_SLOT), lambda b, h: (h, CTX_BLK0 + b, 0)),
            pl.BlockSpec((None, 1, KEY_BLK, HEAD_SLOT), lambda b, h: (h, CTX_BLK0 + b, 0, 0)),
            pl.BlockSpec((None, 1, MLA_VE, KEY_BLK), lambda b, h: (h, CTX_BLK0 + b, 0, 0)),
        ],
        out_specs=pl.BlockSpec((MLA_V, CTX_LEN), lambda b, h: (h, b)),
        compiler_params=_cparams(("arbitrary", "arbitrary")),
        name="mla_flash_c",
    )(q, k4, vt4)
    return ot_x, ot_c


def _out_mlp_kernel(h_ref, s5_ref, otx_ref, otc_ref, dn_ref, wo_s5_ref, wo_mla_ref, wo_dn_ref,
                    g2_ref, sh_ref, sc_ref, g5_ref, nw_ref, w1_ref, w2_ref, o_ref,
                    x1_ref, xn_ref, acc_ref):
    k = pl.program_id(1)

    @pl.when(k == 0)
    def _():
        ot = jnp.where(pl.program_id(0) >= ROWS_X // TM, otc_ref[...], otx_ref[...])
        mix = jnp.dot(s5_ref[...], wo_s5_ref[...], preferred_element_type=F32)
        mix += lax.dot_general(ot, wo_mla_ref[...], (((0,), (0,)), ((), ())), preferred_element_type=F32)
        mix += jnp.dot(dn_ref[...], wo_dn_ref[...], preferred_element_type=F32)
        x1 = h_ref[...] + g2_ref[...] * mix
        x1_ref[...] = x1
        xn_ref[...] = (_rms_rows(x1, nw_ref[...]) * (1.0 + sc_ref[...]) + sh_ref[...]).astype(BF16)
        acc_ref[...] = jnp.zeros_like(acc_ref)

    hid = jnp.dot(xn_ref[...], w1_ref[...], preferred_element_type=F32)
    hid = jnp.maximum(hid, 0.0)
    acc_ref[...] += jnp.dot((hid * hid).astype(BF16), w2_ref[...], preferred_element_type=F32)

    @pl.when(k == pl.num_programs(1) - 1)
    def _():
        o_ref[...] = x1_ref[...] + g5_ref[...] * acc_ref[...]


def _out_mlp(h, s5, ot_x, ot_c, dn, wo_s5, wo_mla, wo_dn, mods, nw, w1, w2, n_rows):
    n_xtiles = ROWS_X // TM
    if ot_c is None:
        ot_c = ot_x
    mod_spec = lambda j: pl.BlockSpec((None, None, 1, D_MODEL), lambda i, k: (_mod_index(i), j, 0, 0))
    full = lambda a: pl.BlockSpec(a.shape, lambda i, k: (0,) * a.ndim)
    return pl.pallas_call(
        _out_mlp_kernel,
        out_shape=jax.ShapeDtypeStruct((n_rows, D_MODEL), F32),
        grid=(n_rows // TM, D_FF // TF),
        in_specs=[
            pl.BlockSpec((TM, D_MODEL), lambda i, k: (i, 0)),
            pl.BlockSpec((TM, S5_WIDTH), lambda i, k: (i, 0)),
            pl.BlockSpec((MLA_WIDTH, TM), lambda i, k: (0, jnp.minimum(i, n_xtiles - 1))),
            pl.BlockSpec((MLA_WIDTH, TM), lambda i, k: (0, jnp.maximum(i - n_xtiles, 0))),
            pl.BlockSpec((TM, DN_WIDTH), lambda i, k: (i, 0)),
            full(wo_s5), full(wo_mla), full(wo_dn),
            mod_spec(2), mod_spec(3), mod_spec(4), mod_spec(5),
            full(nw),
            pl.BlockSpec((D_MODEL, TF), lambda i, k: (0, k)),
            pl.BlockSpec((TF, D_MODEL), lambda i, k: (k, 0)),
        ],
        out_specs=pl.BlockSpec((TM, D_MODEL), lambda i, k: (i, 0)),
        scratch_shapes=[
            pltpu.VMEM((TM, D_MODEL), F32),
            pltpu.VMEM((TM, D_MODEL), BF16),
            pltpu.VMEM((TM, D_MODEL), F32),
        ],
        compiler_params=_cparams(("arbitrary", "arbitrary")),
        name="out_mlp",
    )(h, s5, ot_x, ot_c, dn, wo_s5, wo_mla, wo_dn, mods, mods, mods, mods, nw, w1, w2)


S5_NSTATE = 2 * S5_GROUPS * S5_STATE
S5_SLABS = S5_NSTATE // LANES
S5_CHAINS = 2 * BATCH
S5_STEPS = CTX_LEN + SEQ
S5_TB = 128
S5_PITCH = S5_TB + 4


def _s5_kernel(*refs):
    u_refs = refs[:S5_CHAINS]
    flip_ref, bmat_ref, cmat_ref, are_ref, aim_ref, yf_ref, yb_ref, st_ref, hs_ref, h_ref = refs[S5_CHAINS:]

    @pl.when(pl.program_id(0) == 0)
    def _():
        h_ref[...] = jnp.zeros_like(h_ref)

    half = S5_SLABS // 2
    flip = flip_ref[...]
    for d in range(2):
        blocks = [u_refs[d * BATCH + b][...].astype(BF16) for b in range(BATCH)]
        if d == 1:
            blocks = [jnp.dot(flip, blk, preferred_element_type=F32).astype(BF16) for blk in blocks]
        bu = jnp.dot(jnp.concatenate(blocks, axis=0), bmat_ref[d], preferred_element_type=F32)
        for b in range(BATCH):
            row0 = (d * BATCH + b) * S5_PITCH
            for j in range(S5_SLABS):
                st_ref[j, row0:row0 + S5_TB, :] = bu[b * S5_TB:(b + 1) * S5_TB, j * LANES:(j + 1) * LANES]

    a_re = [are_ref[j] for j in range(half)]
    a_im = [aim_ref[j] for j in range(half)]

    def step(s, h):
        new = []
        rows = pl.ds(s, S5_CHAINS, stride=S5_PITCH)
        for j in range(half):
            hr, hi = h[j], h[half + j]
            nr = a_re[j] * hr - a_im[j] * hi + st_ref[j, rows, :]
            ni = a_re[j] * hi + a_im[j] * hr + st_ref[half + j, rows, :]
            hs_ref[j, rows, :] = nr
            hs_ref[half + j, rows, :] = ni
            new.append((nr, ni))
        return tuple(n[0] for n in new) + tuple(n[1] for n in new)

    h = lax.fori_loop(0, S5_TB, step, tuple(h_ref[j] for j in range(S5_SLABS)), unroll=4)
    for j in range(S5_SLABS):
        h_ref[j] = h[j]

    for d, y_ref in ((0, yf_ref), (1, yb_ref)):
        parts = []
        for b in range(BATCH):
            row0 = (d * BATCH + b) * S5_PITCH
            parts.append(jnp.concatenate([hs_ref[j, row0:row0 + S5_TB, :] for j in range(S5_SLABS)], axis=1))
        hm = jnp.concatenate(parts, axis=0).astype(BF16)
        y = jnp.dot(hm, cmat_ref[d], preferred_element_type=F32)
        for b in range(BATCH):
            yb = y[b * S5_TB:(b + 1) * S5_TB]
            y_ref[b] = _exact_cols_dot(flip, yb) if d == 1 else yb


def _s5_step_block(i, b, reverse):
    n_ctx = CTX_LEN // S5_TB
    n_x = SEQ // S5_TB
    ctx0 = ROWS_X // S5_TB + b * n_ctx
    if reverse:
        return jnp.where(i < n_ctx, ctx0 + n_ctx - 1 - i, b * n_x + n_x - 1 - (i - n_ctx))
    return jnp.where(i < n_ctx, ctx0 + i, b * n_x + (i - n_ctx))


def _s5_time_block(i, reverse):
    n_ctx = CTX_LEN // S5_TB
    n_all = S5_STEPS // S5_TB
    if reverse:
        return jnp.where(i < n_ctx, n_ctx - 1 - i, n_all - 1 - (i - n_ctx))
    return i


def _s5_scan_chains(u, bmat, cmat, a_re, a_im):
    full = lambda a: pl.BlockSpec(a.shape, lambda i: (0,) * a.ndim)
    flip = jnp.eye(S5_TB, dtype=BF16)[::-1]
    u_specs = [pl.BlockSpec((S5_TB, S5_WIDTH), functools.partial(lambda i, b, rev: (_s5_step_block(i, b, rev), 0),
                                                                b=c % BATCH, rev=c >= BATCH))
               for c in range(S5_CHAINS)]
    y_shape = jax.ShapeDtypeStruct((BATCH, S5_STEPS, S5_WIDTH), F32)
    return pl.pallas_call(
        _s5_kernel,
        out_shape=[y_shape, y_shape],
        grid=(S5_STEPS // S5_TB,),
        in_specs=u_specs + [full(flip), full(bmat), full(cmat), full(a_re), full(a_im)],
        out_specs=[pl.BlockSpec((BATCH, S5_TB, S5_WIDTH), lambda i: (0, _s5_time_block(i, False), 0)),
                   pl.BlockSpec((BATCH, S5_TB, S5_WIDTH), lambda i: (0, _s5_time_block(i, True), 0))],
        scratch_shapes=[pltpu.VMEM((S5_SLABS, S5_CHAINS * S5_PITCH, LANES), F32),
                        pltpu.VMEM((S5_SLABS, S5_CHAINS * S5_PITCH, LANES), F32),
                        pltpu.VMEM((S5_SLABS, S5_CHAINS, LANES), F32)],
        compiler_params=_cparams(("arbitrary",)),
        name="s5_scan",
    )(*([u] * S5_CHAINS), flip, bmat, cmat, a_re, a_im)


def _s5_glu_kernel(u_ref, yf_ref, yb_ref, d_ref, w_ref, o_ref):
    y = d_ref[...] * u_ref[...] + yf_ref[...] + yb_ref[...]
    z = jnp.dot(jax.nn.gelu(y).astype(BF16), w_ref[...], preferred_element_type=F32)
    o_ref[...] = (z[:, :S5_WIDTH] * jax.nn.sigmoid(z[:, S5_WIDTH:])).astype(BF16)


def _s5_glu(u, yf, yb, d_skip, glu_w):
    tiles_x = SEQ // CTX_LEN

    def y_idx(i):
        is_x = i < BATCH * tiles_x
        return (jnp.where(is_x, i // tiles_x, i - BATCH * tiles_x), jnp.where(is_x, 1 + i % tiles_x, 0), 0)

    row = pl.BlockSpec((CTX_LEN, S5_WIDTH), lambda i: (i, 0))
    y_spec = pl.BlockSpec((None, CTX_LEN, S5_WIDTH), y_idx)
    return pl.pallas_call(
        _s5_glu_kernel,
        out_shape=jax.ShapeDtypeStruct((ROWS, S5_WIDTH), BF16),
        grid=(ROWS // CTX_LEN,),
        in_specs=[row, y_spec, y_spec,
                  pl.BlockSpec((1, S5_WIDTH), lambda i: (0, 0)),
                  pl.BlockSpec((S5_WIDTH, 2 * S5_WIDTH), lambda i: (0, 0))],
        out_specs=row,
        compiler_params=_cparams(("arbitrary",)),
        name="s5_glu",
    )(u, yf, yb, d_skip, glu_w)


def _s5_params(lam_re, lam_im, log_dt, b_re, b_im, c_re, c_im):
    dt = jnp.exp(log_dt)[..., None]
    mag = jnp.exp(lam_re * dt)
    a_re, a_im = mag * jnp.cos(lam_im * dt), mag * jnp.sin(lam_im * dt)
    den = lam_re * lam_re + lam_im * lam_im
    f_re = ((a_re - 1.0) * lam_re + a_im * lam_im) / den
    f_im = (a_im * lam_re - (a_re - 1.0) * lam_im) / den
    bb_re = f_re[..., None] * b_re - f_im[..., None] * b_im
    bb_im = f_re[..., None] * b_im + f_im[..., None] * b_re
    eye = jnp.eye(S5_GROUPS, dtype=F32)
    blk_b = lambda m: jnp.einsum('dgph,gk->dghkp', m, eye).reshape(2, S5_WIDTH, S5_GROUPS * S5_STATE)
    blk_c = lambda m: jnp.einsum('dghp,gk->dgpkh', m, eye).reshape(2, S5_GROUPS * S5_STATE, S5_WIDTH)
    bmat = jnp.concatenate([blk_b(bb_re), blk_b(bb_im)], axis=2).astype(BF16)
    cmat = jnp.concatenate([blk_c(c_re), -blk_c(c_im)], axis=1).astype(BF16)
    per_chain = lambda a: jnp.repeat(a.reshape(2, S5_SLABS // 2, LANES), BATCH, axis=0).transpose(1, 0, 2)
    return bmat, cmat, per_chain(a_re), per_chain(a_im)


def _s5_mixer(u, lam_re, lam_im, log_dt, b_re, b_im, c_re, c_im, d_skip, glu_w):
    bmat, cmat, a_re, a_im = _s5_params(lam_re, lam_im, log_dt, b_re, b_im, c_re, c_im)
    y_f, y_b = _s5_scan_chains(u, bmat, cmat, a_re, a_im)
    return _s5_glu(u, y_f, y_b, d_skip[None], glu_w.astype(BF16))


DN_TILE = 256
DN_HALO = 8
DN_XTILES = SEQ // DN_TILE
DN_XTILE = 512
DN_BASE = 2
DN_MERGES = (2, 4, 8, 16, 32)


def _exact_rows_dot(a, e):
    a_hi = a.astype(BF16)
    r = a - a_hi.astype(F32)
    a_mid = r.astype(BF16)
    a_lo = (r - a_mid.astype(F32)).astype(BF16)
    out = jnp.dot(a_hi, e, preferred_element_type=F32)
    out += jnp.dot(a_mid, e, preferred_element_type=F32)
    out += jnp.dot(a_lo, e, preferred_element_type=F32)
    return out


def _exact_cols_dot(e, a):
    a_hi = a.astype(BF16)
    r = a - a_hi.astype(F32)
    a_mid = r.astype(BF16)
    a_lo = (r - a_mid.astype(F32)).astype(BF16)
    out = jnp.dot(e, a_hi, preferred_element_type=F32)
    out += jnp.dot(e, a_mid, preferred_element_type=F32)
    out += jnp.dot(e, a_lo, preferred_element_type=F32)
    return out


def _dn_prep_kernel(x_ref, prev_ref, next_ref, ba_ref, cw_ref, nega_ref, dtb_ref, hsum_ref, expand_ref,
                    ltri_ref, utri_ref, qn_ref, kn_ref, v_ref, bf_ref, bb_ref, gf_ref, gb_ref, xe_ref):
    i = pl.program_id(0)
    is_ctx = i >= ROWS_X // DN_TILE
    first = jnp.logical_or(is_ctx, i % DN_XTILES == 0)
    last = jnp.logical_or(is_ctx, i % DN_XTILES == DN_XTILES - 1)
    xe_ref[0:DN_HALO, :] = jnp.where(first, 0.0, prev_ref[...])
    xe_ref[DN_HALO:DN_HALO + DN_TILE, :] = x_ref[...]
    xe_ref[DN_HALO + DN_TILE:, :] = jnp.where(last, 0.0, next_ref[...])
    pad = DN_CONV // 2
    y = jnp.zeros((DN_TILE, DN_QKV), F32)
    for j in range(DN_CONV):
        off = DN_HALO + j - pad
        y += cw_ref[j:j + 1, :] * xe_ref[off:off + DN_TILE, :]
    y = y * jax.nn.sigmoid(y)
    qk_w = DN_HEADS * DN_DK
    q, k = y[:, :qk_w], y[:, qk_w:2 * qk_w]
    hsum = hsum_ref[...]
    qn_ref[...] = q * lax.rsqrt(_exact_rows_dot(q * q, hsum) + EPS) * (DN_DK ** -0.5)
    kn_ref[...] = k * lax.rsqrt(_exact_rows_dot(k * k, hsum) + EPS)
    v_ref[...] = y[:, 2 * qk_w:]

    ba = ba_ref[...]
    lane = lax.broadcasted_iota(jnp.int32, (1, LANES), 1)
    z = ba + dtb_ref[...]
    softplus = jnp.maximum(z, 0.0) + jnp.log(1.0 + jnp.exp(-jnp.abs(z)))
    bg = jnp.where(lane < 2 * DN_HEADS, jax.nn.sigmoid(ba), nega_ref[...] * softplus)
    ex = _exact_rows_dot(bg, expand_ref[...])
    w = DN_WIDTH
    bf_ref[...] = ex[:, 0:w]
    bb_ref[...] = ex[:, w:2 * w]
    gf_ref[...] = _exact_cols_dot(ltri_ref[...], ex[:, 2 * w:3 * w])
    gb_ref[...] = _exact_cols_dot(utri_ref[...], ex[:, 3 * w:4 * w])


def _dn_prep(qkv, ba, conv_w, a_log, dt_bias):
    n_halo_blocks = ROWS // DN_HALO
    per_tile = DN_TILE // DN_HALO
    cw = jnp.concatenate([conv_w, jnp.zeros((8 - DN_CONV, DN_QKV), F32)], axis=0)
    lanes_g = jnp.zeros((LANES,), F32)
    nega = lanes_g.at[2 * DN_HEADS:4 * DN_HEADS].set(-jnp.exp(a_log.reshape(-1)))[None]
    dtb = lanes_g.at[2 * DN_HEADS:4 * DN_HEADS].set(dt_bias.reshape(-1))[None]
    head_of_lane = jnp.arange(DN_WIDTH) // DN_DV
    hsum = (head_of_lane[:, None] == head_of_lane[None, :]).astype(BF16)
    src = jnp.arange(LANES)[:, None]
    expand = jnp.concatenate(
        [(src == (g * DN_HEADS + head_of_lane)[None, :]) for g in range(4)], axis=1).astype(BF16)
    r = jnp.arange(DN_TILE)
    same_chunk = (r[:, None] // DN_CHUNK) == (r[None, :] // DN_CHUNK)
    ltri = (same_chunk & (r[:, None] >= r[None, :])).astype(BF16)
    utri = (same_chunk & (r[:, None] <= r[None, :])).astype(BF16)
    full = lambda a: pl.BlockSpec(a.shape, lambda i: (0,) * a.ndim)
    row = lambda n: pl.BlockSpec((DN_TILE, n), lambda i: (i, 0))
    return pl.pallas_call(
        _dn_prep_kernel,
        out_shape=[jax.ShapeDtypeStruct((ROWS, DN_WIDTH), F32)] * 7,
        grid=(ROWS // DN_TILE,),
        in_specs=[
            row(DN_QKV),
            pl.BlockSpec((DN_HALO, DN_QKV), lambda i: (jnp.maximum(i * per_tile - 1, 0), 0)),
            pl.BlockSpec((DN_HALO, DN_QKV), lambda i: (jnp.minimum((i + 1) * per_tile, n_halo_blocks - 1), 0)),
            row(LANES),
            full(cw), full(nega), full(dtb), full(hsum), full(expand), full(ltri), full(utri),
        ],
        out_specs=[row(DN_WIDTH)] * 7,
        scratch_shapes=[pltpu.VMEM((DN_TILE + 2 * DN_HALO, DN_QKV), F32)],
        compiler_params=_cparams(("arbitrary",)),
        name="dn_prep",
    )(qkv, qkv, qkv, ba, cw, nega, dtb, hsum, expand, ltri, utri)


def _dn_bd(y, blockmask):
    yb = y.astype(BF16)
    return jnp.where(blockmask, jnp.concatenate([yb] * DN_HEADS, axis=0), jnp.zeros((), BF16))


def _dn_prepare(chunks, blockmask, eye_cat, blk_base, off_masks):
    dot = functools.partial(jnp.dot, preferred_element_type=F32)
    bd = lambda y: _dn_bd(y, blockmask)

    a_mats, attns, egs = [], [], []
    for qn, kn, v, beta, gc, incl, strict, last in chunks:
        gcrow = jnp.sum(jnp.where(eye_cat, gc, 0.0), axis=0, keepdims=True)
        decay = jnp.exp(jnp.where(incl, gc - gcrow, -jnp.inf))
        grams = lax.dot_general(jnp.concatenate([kn, qn], axis=0).astype(BF16), bd(kn),
                                (((1,), (1,)), ((), ())), preferred_element_type=F32)
        a_mats.append(jnp.where(strict, grams[:DN_CHUNK] * beta * decay, 0.0))
        attns.append((grams[DN_CHUNK:] * decay).astype(BF16))
        egs.append(jnp.exp(gc))

    t_mats = [jnp.where(eye_cat, 1.0, 0.0) - jnp.where(blk_base, a, 0.0) for a in a_mats]
    mm1 = lambda x, y: dot(x.astype(BF16), bd(y))
    for off_mask in off_masks:
        z = [mm1(t, jnp.where(off_mask, a, 0.0)) for t, a in zip(t_mats, a_mats)]
        t_mats = [t - mm1(zi, t) for t, zi in zip(t_mats, z)]

    out = []
    for (qn, kn, v, beta, gc, incl, strict, last), t, attn, eg in zip(chunks, t_mats, attns, egs):
        tb = t.astype(BF16)
        glast = gc[last:last + 1, :]
        w_v = dot(tb, bd(v * beta))
        w_k = dot(tb, bd(kn * beta * eg))
        lhs = jnp.concatenate([w_k, qn * eg], axis=0).astype(BF16)
        k_dec = (kn * jnp.exp(glast - gc)).astype(BF16)
        out.append((w_v, lhs, attn, k_dec, jnp.exp(glast)))
    return out


def _dn_state_step(prep, s_bd, blockmask):
    w_v, lhs, attn, k_dec, e_last = prep
    dot = functools.partial(jnp.dot, preferred_element_type=F32)
    ps = dot(lhs, s_bd.astype(BF16))
    v_new = w_v - ps[:DN_CHUNK]
    o = ps[DN_CHUNK:] + dot(attn, _dn_bd(v_new, blockmask))
    upd = lax.dot_general(k_dec, v_new.astype(BF16), (((0,), (0,)), ((), ())), preferred_element_type=F32)
    return o, s_bd * e_last + jnp.where(blockmask, upd, 0.0)


def _dn_chunks_kernel(*refs, n_chunk, has_init):
    qf_ref, kf_ref, vf_ref, bf_ref, gf_ref, qb_ref, kb_ref, vb_ref, bb_ref, gb_ref = refs[:10]
    if has_init:
        s0_ref, of_ref, ob_ref, s_ref = refs[10:]
    else:
        of_ref, ob_ref, sout_ref, s_ref = refs[10:]

    @pl.when(pl.program_id(1) == 0)
    def _():
        s_ref[...] = s0_ref[...] if has_init else jnp.zeros_like(s_ref)

    row = lax.broadcasted_iota(jnp.int32, (DN_WIDTH, DN_WIDTH), 0)
    col = lax.broadcasted_iota(jnp.int32, (DN_WIDTH, DN_WIDTH), 1)
    blockmask = (row // DN_DV) == (col // DN_DV)
    i = lax.broadcasted_iota(jnp.int32, (DN_CHUNK, DN_WIDTH), 0)
    j = lax.broadcasted_iota(jnp.int32, (DN_CHUNK, DN_WIDTH), 1) % DN_DV
    eye_cat = i == j
    blk_base = (i // DN_BASE) == (j // DN_BASE)
    off_masks = tuple(((i // (2 * s)) == (j // (2 * s))) & ((i // s) != (j // s))
                      for s in DN_MERGES)
    chunks, where = [], []
    for c in range(n_chunk):
        rf = slice(c * DN_CHUNK, (c + 1) * DN_CHUNK)
        chunks.append((qf_ref[rf, :], kf_ref[rf, :], vf_ref[rf, :], bf_ref[rf, :], gf_ref[rf, :],
                       i >= j, i > j, DN_CHUNK - 1))
        where.append((0, of_ref, rf))
        cb = n_chunk - 1 - c
        rb = slice(cb * DN_CHUNK, (cb + 1) * DN_CHUNK)
        chunks.append((qb_ref[rb, :], kb_ref[rb, :], vb_ref[rb, :], bb_ref[rb, :], gb_ref[rb, :],
                       i <= j, i < j, 0))
        where.append((1, ob_ref, rb))
    prepared = _dn_prepare(chunks, blockmask, eye_cat, blk_base, off_masks)

    states = [s_ref[0], s_ref[1]]
    for prep, (d, o_ref, rows) in zip(prepared, where):
        o, states[d] = _dn_state_step(prep, states[d], blockmask)
        o_ref[rows, :] = o
    s_ref[0] = states[0]
    s_ref[1] = states[1]
    if not has_init:
        sout_ref[0] = states[0]
        sout_ref[1] = states[1]


def _dn_chunks(qn, kn, v, beta_f, beta_b, gc_f, gc_b):
    state_shape = jax.ShapeDtypeStruct((BATCH, 2, DN_WIDTH, DN_WIDTH), F32)
    state_spec = pl.BlockSpec((None, 2, DN_WIDTH, DN_WIDTH), lambda b, t: (b, 0, 0, 0))
    scratch = [pltpu.VMEM((2, DN_WIDTH, DN_WIDTH), F32)]
    args = (qn, kn, v, beta_f, gc_f, qn, kn, v, beta_b, gc_b)

    ctx_in = pl.BlockSpec((CTX_LEN, DN_WIDTH), lambda b, t: (ROWS_X // CTX_LEN + b, 0))
    ctx_out = pl.BlockSpec((CTX_LEN, DN_WIDTH), lambda b, t: (b, 0))
    o_fc, o_bc, states = pl.pallas_call(
        functools.partial(_dn_chunks_kernel, n_chunk=CTX_LEN // DN_CHUNK, has_init=False),
        out_shape=[jax.ShapeDtypeStruct((ROWS_C, DN_WIDTH), F32)] * 2 + [state_shape],
        grid=(BATCH, 1),
        in_specs=[ctx_in] * 10,
        out_specs=[ctx_out, ctx_out, state_spec],
        scratch_shapes=scratch,
        compiler_params=_cparams(("arbitrary", "arbitrary")),
        name="dn_chunks_ctx",
    )(*args)

    n_x = SEQ // DN_XTILE
    spec_f = pl.BlockSpec((DN_XTILE, DN_WIDTH), lambda b, t: (b * n_x + t, 0))
    spec_b = pl.BlockSpec((DN_XTILE, DN_WIDTH), lambda b, t: (b * n_x + n_x - 1 - t, 0))
    o_fx, o_bx = pl.pallas_call(
        functools.partial(_dn_chunks_kernel, n_chunk=DN_XTILE // DN_CHUNK, has_init=True),
        out_shape=[jax.ShapeDtypeStruct((ROWS_X, DN_WIDTH), F32)] * 2,
        grid=(BATCH, n_x),
        in_specs=[spec_f] * 5 + [spec_b] * 5 + [state_spec],
        out_specs=[spec_f, spec_b],
        scratch_shapes=scratch,
        compiler_params=_cparams(("arbitrary", "arbitrary")),
        name="dn_chunks",
    )(*args, states)
    return o_fx, o_bx, o_fc, o_bc


def _dn_out_kernel(ofx_ref, obx_ref, ofc_ref, obc_ref, gt_ref, nw_ref, hsum_ref, o_ref):
    is_ctx = pl.program_id(0) >= ROWS_X // TM
    o = jnp.where(is_ctx, ofc_ref[...] + obc_ref[...], ofx_ref[...] + obx_ref[...])
    ms = _exact_rows_dot(o * o, hsum_ref[...]) * (1.0 / DN_DV)
    gt = gt_ref[...]
    o_ref[...] = (o * lax.rsqrt(ms + EPS) * nw_ref[...] * (gt * jax.nn.sigmoid(gt))).astype(BF16)


def _dn_out(o_fx, o_bx, o_fc, o_bc, gt, norm_w):
    head_of_lane = jnp.arange(DN_WIDTH) // DN_DV
    hsum = (head_of_lane[:, None] == head_of_lane[None, :]).astype(BF16)
    nw = jnp.tile(norm_w, DN_HEADS)[None]
    n_xtiles = ROWS_X // TM
    row = pl.BlockSpec((TM, DN_WIDTH), lambda i: (i, 0))
    row_x = pl.BlockSpec((TM, DN_WIDTH), lambda i: (jnp.minimum(i, n_xtiles - 1), 0))
    row_c = pl.BlockSpec((TM, DN_WIDTH), lambda i: (jnp.maximum(i - n_xtiles, 0), 0))
    return pl.pallas_call(
        _dn_out_kernel,
        out_shape=jax.ShapeDtypeStruct((ROWS, DN_WIDTH), BF16),
        grid=(ROWS // TM,),
        in_specs=[row_x, row_x, row_c, row_c, row, pl.BlockSpec((1, DN_WIDTH), lambda i: (0, 0)),
                  pl.BlockSpec((DN_WIDTH, DN_WIDTH), lambda i: (0, 0))],
        out_specs=row,
        compiler_params=_cparams(("arbitrary",)),
        name="dn_out",
    )(o_fx, o_bx, o_fc, o_bc, gt, nw, hsum)


def _dn_mixer(qkv, gt, ba, conv_w, a_log, dt_bias, norm_w):
    qn, kn, v, beta_f, beta_b, gc_f, gc_b = _dn_prep(qkv, ba, conv_w, a_log, dt_bias)
    return _dn_out(*_dn_chunks(qn, kn, v, beta_f, beta_b, gc_f, gc_b), gt, norm_w)


def _pad_w_in(w):
    d = w.shape[0]
    s = [0]
    for n in IN_SIZES:
        s.append(s[-1] + n)
    u, cq, ckv, kr, qkv, gt, ba = (w[:, s[i]:s[i + 1]] for i in range(7))
    z = lambda n: jnp.zeros((d, n), w.dtype)
    return jnp.concatenate(
        [u, cq, ckv, z(KR_LANE), kr, z(LANES - KR_LANE - MLA_ROPE), qkv, gt, ba, z(LANES - 4 * DN_HEADS)], axis=1)


def _head_slots(w, width):
    kdim = w.shape[0]
    w = w.reshape(kdim, MLA_HEADS, width)
    return jnp.pad(w, ((0, 0), (0, 0), (0, HEAD_SLOT - width))).reshape(kdim, MLA_HEADS * HEAD_SLOT)


def _rope_swap(w):
    lane = jnp.arange(HEAD_SLOT)
    half = MLA_ROPE // 2
    src = jnp.where(lane < MLA_NOPE + half, lane + half, lane - half)
    in_rope = (lane >= MLA_NOPE) & (lane < MLA_DH)
    w3 = w.reshape(w.shape[0], -1, HEAD_SLOT)
    return jnp.where(in_rope, w3[:, :, jnp.clip(src, 0, HEAD_SLOT - 1)], jnp.zeros((), w.dtype)).reshape(w.shape)


def _rope_tables():
    rows = SEQ // GRID_W
    row = jnp.repeat(jnp.arange(rows, dtype=F32), GRID_W)
    col = jnp.tile(jnp.arange(GRID_W, dtype=F32), rows)
    inv = ROPE_BASE ** (-jnp.arange(ROPE_FREQS, dtype=F32) / ROPE_FREQS)
    ang = jnp.concatenate([row[:, None] * inv, col[:, None] * inv], axis=-1)
    cos, sin = jnp.cos(ang), jnp.sin(ang)
    half = MLA_ROPE // 2
    ones = lambda n: jnp.ones((SEQ, n), F32)
    zeros = lambda n: jnp.zeros((SEQ, n), F32)
    cos_t = jnp.concatenate([ones(MLA_NOPE), cos, cos, ones(HEAD_SLOT - MLA_DH)], axis=1)
    sin_t = jnp.concatenate([zeros(MLA_NOPE), -sin, sin, zeros(HEAD_SLOT - MLA_DH)], axis=1)
    cos_t = jnp.concatenate([jnp.tile(cos_t, (BATCH, 1)), jnp.ones((ROWS_C, HEAD_SLOT), F32)], axis=0)
    sin_t = jnp.concatenate([jnp.tile(sin_t, (BATCH, 1)), jnp.zeros((ROWS_C, HEAD_SLOT), F32)], axis=0)
    del half
    return cos_t, sin_t


def _split_rows(a, width):
    return a[:ROWS_X].reshape(BATCH, SEQ, width), a[ROWS_X:].reshape(BATCH, CTX_LEN, width)


def _join_rows(ax, ac, width):
    return jnp.concatenate([ax.reshape(ROWS_X, width), ac.reshape(ROWS_C, width)], axis=0)


def kernel(x, c, ctx, c_ctx, w_ada, b_ada, norm1_w, w_in, s5_lambda_re, s5_lambda_im, s5_log_dt, s5_b_re, s5_b_im, s5_c_re, s5_c_im, s5_d, s5_glu_w, mla_q_norm_w, mla_kv_norm_w, mla_w_uq, mla_w_ukv, mla_q_gain, mla_k_gain, dn_conv_w, dn_a_log, dn_dt_bias, dn_norm_w, w_out, norm2_w, w_ff1, w_ff2):
    h = jnp.concatenate([x.reshape(ROWS_X, D_MODEL), ctx.reshape(ROWS_C, D_MODEL)], axis=0)
    c_pad = jnp.concatenate([c, c_ctx[None], jnp.zeros((3, D_MODEL), F32)], axis=0)
    mods_all = _ada_all(c_pad, w_ada, b_ada)
    cos_t, sin_t = _rope_tables()

    for l in range(DEPTH):
        ctx_out = l < DEPTH - 1
        mods = mods_all[l, :BATCH + 1].reshape(BATCH + 1, 6, 1, D_MODEL)
        u, mla_in, qkv, gt, ba = _in_proj(h, norm1_w[l][None], mods, _pad_w_in(w_in[l]).astype(BF16))

        s5 = _s5_mixer(u, s5_lambda_re[l], s5_lambda_im[l], s5_log_dt[l], s5_b_re[l], s5_b_im[l],
                       s5_c_re[l], s5_c_im[l], s5_d[l], s5_glu_w[l])

        wq = _head_slots(mla_w_uq[l], MLA_DH)
        w_ukv = mla_w_ukv[l].reshape(MLA_KV_RANK, MLA_HEADS, MLA_NOPE + MLA_V)
        wk = _head_slots(w_ukv[:, :, :MLA_NOPE].reshape(MLA_KV_RANK, MLA_HEADS * MLA_NOPE), MLA_NOPE).astype(BF16)
        wvt = w_ukv[:, :, MLA_NOPE:].reshape(MLA_KV_RANK, MLA_WIDTH).T.astype(BF16)
        qg, kg = (jnp.pad(g, (0, HEAD_SLOT - MLA_DH))[None] for g in (mla_q_gain[l], mla_k_gain[l]))
        q, k4, vt4 = _mla_proj(mla_in, cos_t, sin_t, mla_q_norm_w[l][None], mla_kv_norm_w[l][None],
                               wq.astype(BF16), _rope_swap(wq).astype(BF16), wk, wvt,
                               qg, _rope_swap(qg), kg, _rope_swap(kg), _rope_swap(jnp.eye(HEAD_SLOT, dtype=BF16)))
        ot_x, ot_c = _flash(q, k4, vt4, ctx_out)

        dn = _dn_mixer(qkv, gt, ba, dn_conv_w[l], dn_a_log[l], dn_dt_bias[l], dn_norm_w[l])

        wo = w_out[l].astype(BF16)
        n_rows = ROWS if ctx_out else ROWS_X
        h = _out_mlp(h, s5, ot_x, ot_c, dn, wo[:S5_WIDTH], wo[S5_WIDTH:S5_WIDTH + MLA_WIDTH], wo[S5_WIDTH + MLA_WIDTH:],
                     mods, norm2_w[l][None], w_ff1[l].astype(BF16), w_ff2[l].astype(BF16), n_rows)
    return h.reshape(BATCH, SEQ, D_MODEL)
```

```python
import functools

import jax
import jax.numpy as jnp
from jax import lax
from jax.experimental import pallas as pl
from jax.experimental.pallas import tpu as pltpu

F32 = jnp.float32
BF16 = jnp.bfloat16

D_MODEL = 1024
BATCH = 4
SEQ = 4096
DEPTH = 4
GRID_W = 64
CTX_LEN = 256
EPS = 1e-6

S5_GROUP = 16
S5_GROUPS = 16
S5_WIDTH = 256
S5_STATE = 64

MLA_HEADS = 8
MLA_NOPE = 64
MLA_ROPE = 32
MLA_V = 64
MLA_VE = MLA_V + 16
MLA_DH = 96
MLA_Q_RANK = 256
MLA_KV_RANK = 128
MLA_WIDTH = 512
ROPE_BASE = 10000.0
ROPE_FREQS = 8

DN_HEADS = 4
DN_DK = 64
DN_DV = 64
DN_QKV = 768
DN_WIDTH = 256
DN_CONV = 5
DN_CHUNK = 64

D_FF = 4 * D_MODEL
IN_SIZES = (S5_WIDTH, MLA_Q_RANK, MLA_KV_RANK, MLA_ROPE, DN_QKV, DN_WIDTH, 4 * DN_HEADS)

ROWS_X = BATCH * SEQ
ROWS_C = BATCH * CTX_LEN
ROWS = ROWS_X + ROWS_C
LANES = 128
HEAD_SLOT = LANES
KEY_BLK = 256
N_XBLK = SEQ // KEY_BLK
CTX_BLK0 = ROWS_X // KEY_BLK

IN_PAD = 1920
COL_U, COL_CQ, COL_CKV, COL_KR, COL_QKV, COL_GT, COL_BA = 0, 256, 512, 640, 768, 1536, 1792
KR_LANE = MLA_NOPE

TM = 512
TQ = 4096
FLASH_KB = 2
ADA_TN = 1536
VMEM_LIMIT = 56 * 1024 * 1024

LOG2E = 1.4426950408889634


def _cparams(sem):
    return pltpu.CompilerParams(dimension_semantics=sem, vmem_limit_bytes=VMEM_LIMIT)


def _split3_dot(a, w):
    a_hi = a.astype(BF16)
    a_lo = (a - a_hi.astype(F32)).astype(BF16)
    w_hi = w.astype(BF16)
    w_lo = (w - w_hi.astype(F32)).astype(BF16)
    out = jnp.dot(a_hi, w_hi, preferred_element_type=F32)
    out += jnp.dot(a_lo, w_hi, preferred_element_type=F32)
    out += jnp.dot(a_hi, w_lo, preferred_element_type=F32)
    return out


def _rms_rows(x, w):
    return x * lax.rsqrt(jnp.mean(x * x, axis=-1, keepdims=True) + EPS) * w


def _ada_kernel(c_ref, w_ref, b_ref, o_ref):
    c = c_ref[...]
    s = c * jax.nn.sigmoid(c)
    o_ref[...] = _split3_dot(s, w_ref[...]) + b_ref[...]


def _ada_all(c_pad, w_ada, b_ada):
    tn = ADA_TN
    return pl.pallas_call(
        _ada_kernel,
        out_shape=jax.ShapeDtypeStruct((DEPTH, 8, 6 * D_MODEL), F32),
        grid=(DEPTH, 6 * D_MODEL // tn),
        in_specs=[
            pl.BlockSpec((8, D_MODEL), lambda l, j: (0, 0)),
            pl.BlockSpec((None, D_MODEL, tn), lambda l, j: (l, 0, j)),
            pl.BlockSpec((None, 1, tn), lambda l, j: (l, 0, j)),
        ],
        out_specs=pl.BlockSpec((None, 8, tn), lambda l, j: (l, 0, j)),
        compiler_params=_cparams(("arbitrary", "arbitrary")),
        name="ada_mod",
    )(c_pad, w_ada, b_ada.reshape(DEPTH, 1, 6 * D_MODEL))


def _mod_index(i):
    return jnp.minimum(i // (SEQ // TM), BATCH)


def _mla_heads(m, cos, sin, qnw, kvnw, wq, wq_sw, wk, wvt, qg, qg_sw, kg, kg_sw, perm, q_ref, k_ref, vt_ref):
    cqn = _rms_rows(m[:, 0:MLA_Q_RANK], qnw).astype(BF16)
    ckvn = _rms_rows(m[:, MLA_Q_RANK:MLA_Q_RANK + MLA_KV_RANK], kvnw).astype(BF16)
    kr = m[:, MLA_Q_RANK + MLA_KV_RANK:]
    qf = jnp.dot(cqn, wq, preferred_element_type=F32)
    qf_sw = jnp.dot(cqn, wq_sw, preferred_element_type=F32)
    kf = jnp.dot(ckvn, wk, preferred_element_type=F32)
    vt = lax.dot_general(wvt, ckvn, (((1,), (1,)), ((), ())), preferred_element_type=F32)
    kr_sw = _exact_rows_dot(kr, perm)
    q_cos, q_sin = cos * qg, sin * qg_sw
    k_cos, k_sin = cos * kg, sin * kg_sw

    def head_norm_rope(xh, xh_sw, g_cos, g_sin, scale):
        ss = jnp.sum(xh * xh, axis=-1, keepdims=True) * (1.0 / MLA_DH)
        return (xh * g_cos + xh_sw * g_sin) * (lax.rsqrt(ss + EPS) * scale)

    n_sub = m.shape[0] // KEY_BLK
    ones = jnp.ones((MLA_VE - MLA_V, m.shape[0]), BF16)
    for h in range(MLA_HEADS):
        sl = slice(h * HEAD_SLOT, (h + 1) * HEAD_SLOT)
        q_ref[h] = head_norm_rope(qf[:, sl], qf_sw[:, sl], q_cos, q_sin, MLA_DH ** -0.5 * LOG2E).astype(BF16)
        kh = head_norm_rope(kf[:, sl] + kr, kr_sw, k_cos, k_sin, 1.0).astype(BF16)
        vh = jnp.concatenate([vt[h * MLA_V:(h + 1) * MLA_V, :].astype(BF16), ones], axis=0)
        for s in range(n_sub):
            k_ref[h, s] = kh[s * KEY_BLK:(s + 1) * KEY_BLK, :]
            vt_ref[h, s] = vh[:, s * KEY_BLK:(s + 1) * KEY_BLK]


def _in_proj_kernel(h_ref, nw_ref, sh_ref, sc_ref, w_ref, cos_ref, sin_ref, qnw_ref, kvnw_ref, wq_ref, wqs_ref,
                    wk_ref, wvt_ref, qg_ref, qgs_ref, kg_ref, kgs_ref, perm_ref,
                    u_ref, qkv_ref, gt_ref, ba_ref, q_ref, k_ref, vt_ref):
    y = _rms_rows(h_ref[...], nw_ref[...]) * (1.0 + sc_ref[...]) + sh_ref[...]
    z = jnp.dot(y.astype(BF16), w_ref[...], preferred_element_type=F32)
    u_ref[...] = z[:, COL_U:COL_CQ]
    qkv_ref[...] = z[:, COL_QKV:COL_GT]
    gt_ref[...] = z[:, COL_GT:COL_BA]
    ba_ref[...] = z[:, COL_BA:IN_PAD]
    _mla_heads(z[:, COL_CQ:COL_QKV], cos_ref[...], sin_ref[...], qnw_ref[...], kvnw_ref[...], wq_ref[...],
               wqs_ref[...], wk_ref[...], wvt_ref[...], qg_ref[...], qgs_ref[...], kg_ref[...], kgs_ref[...],
               perm_ref[...], q_ref, k_ref, vt_ref)


def _in_proj(h, nw, mods, w_pad, cos_t, sin_t, mla_params):
    n_sub = TM // KEY_BLK
    mod_spec = lambda j: pl.BlockSpec((None, None, 1, D_MODEL), lambda i: (_mod_index(i), j, 0, 0))
    full = lambda a: pl.BlockSpec(a.shape, lambda i: (0,) * a.ndim)
    row = lambda n: pl.BlockSpec((TM, n), lambda i: (i, 0))
    row_widths = (COL_CQ - COL_U, COL_GT - COL_QKV, COL_BA - COL_GT, IN_PAD - COL_BA)
    return pl.pallas_call(
        _in_proj_kernel,
        out_shape=[jax.ShapeDtypeStruct((ROWS, n), F32) for n in row_widths] + [
            jax.ShapeDtypeStruct((MLA_HEADS, ROWS, HEAD_SLOT), BF16),
            jax.ShapeDtypeStruct((MLA_HEADS, ROWS // KEY_BLK, KEY_BLK, HEAD_SLOT), BF16),
            jax.ShapeDtypeStruct((MLA_HEADS, ROWS // KEY_BLK, MLA_VE, KEY_BLK), BF16),
        ],
        grid=(ROWS // TM,),
        in_specs=[row(D_MODEL), full(nw), mod_spec(0), mod_spec(1), full(w_pad), row(HEAD_SLOT), row(HEAD_SLOT)]
        + [full(p) for p in mla_params],
        out_specs=[row(n) for n in row_widths] + [
            pl.BlockSpec((MLA_HEADS, TM, HEAD_SLOT), lambda i: (0, i, 0)),
            pl.BlockSpec((MLA_HEADS, n_sub, KEY_BLK, HEAD_SLOT), lambda i: (0, i, 0, 0)),
            pl.BlockSpec((MLA_HEADS, n_sub, MLA_VE, KEY_BLK), lambda i: (0, i, 0, 0)),
        ],
        compiler_params=_cparams(("arbitrary",)),
        name="in_proj",
    )(h, nw, mods, mods, w_pad, cos_t, sin_t, *mla_params)


def _scores(k, q):
    return lax.dot_general(k, q, (((1,), (1,)), ((), ())), preferred_element_type=F32)


def _softmax_update(carry, s, vts):
    m, acc = carry
    m_new = jnp.maximum(m, jnp.max(s, axis=0, keepdims=True))
    pb = jnp.exp2(s - m_new).astype(BF16)
    acc = jnp.exp2(m - m_new) * acc
    for t, vt in enumerate(vts):
        acc += jnp.dot(vt, pb[t * KEY_BLK:(t + 1) * KEY_BLK], preferred_element_type=F32)
    return m_new, acc


def _softmax_attend(q, key_blocks):
    tq = q.shape[0]
    carry = (jnp.full((1, tq), -jnp.inf, F32), jnp.zeros((MLA_VE, tq), F32))
    s_next = _scores(key_blocks[0][0], q)
    for n, (_, vts) in enumerate(key_blocks):
        s = s_next
        if n + 1 < len(key_blocks):
            s_next = _scores(key_blocks[n + 1][0], q)
        carry = _softmax_update(carry, s, vts)
    _, acc = carry
    return (acc[:MLA_V] * (1.0 / acc[MLA_V:MLA_V + 1])).astype(BF16)


def _flash_x_kernel(q_ref, kx_ref, kc_ref, vtx_ref, vtc_ref, o_ref):
    blocks = [(kc_ref[0], [vtc_ref[0]])]
    for j in range(N_XBLK // FLASH_KB):
        k = kx_ref[j * FLASH_KB:(j + 1) * FLASH_KB].reshape(FLASH_KB * KEY_BLK, HEAD_SLOT)
        blocks.append((k, [vtx_ref[j * FLASH_KB + t] for t in range(FLASH_KB)]))
    o_ref[...] = _softmax_attend(q_ref[...], blocks)


def _flash_c_kernel(q_ref, kc_ref, vtc_ref, o_ref):
    o_ref[...] = _softmax_attend(q_ref[...], [(kc_ref[0], [vtc_ref[0]])])


def _flash(q, k4, vt4, ctx_out):
    n_qx = SEQ // TQ
    ot_x = pl.pallas_call(
        _flash_x_kernel,
        out_shape=jax.ShapeDtypeStruct((MLA_WIDTH, ROWS_X), BF16),
        grid=(BATCH, MLA_HEADS, n_qx),
        in_specs=[
            pl.BlockSpec((None, TQ, HEAD_SLOT), lambda b, h, qi: (h, b * n_qx + qi, 0)),
            pl.BlockSpec((None, N_XBLK, KEY_BLK, HEAD_SLOT), lambda b, h, qi: (h, b, 0, 0)),
            pl.BlockSpec((None, 1, KEY_BLK, HEAD_SLOT), lambda b, h, qi: (h, CTX_BLK0 + b, 0, 0)),
            pl.BlockSpec((None, N_XBLK, MLA_VE, KEY_BLK), lambda b, h, qi: (h, b, 0, 0)),
            pl.BlockSpec((None, 1, MLA_VE, KEY_BLK), lambda b, h, qi: (h, CTX_BLK0 + b, 0, 0)),
        ],
        out_specs=pl.BlockSpec((MLA_V, TQ), lambda b, h, qi: (h, b * n_qx + qi)),
        compiler_params=_cparams(("arbitrary", "arbitrary", "arbitrary")),
        name="mla_flash_x",
    )(q, k4, k4, vt4, vt4)
    if not ctx_out:
        return ot_x, None
    ot_c = pl.pallas_call(
        _flash_c_kernel,
        out_shape=jax.ShapeDtypeStruct((MLA_WIDTH, ROWS_C), BF16),
        grid=(BATCH, MLA_HEADS),
        in_specs=[
            pl.BlockSpec((None, CTX_LEN, HEAD_SLOT), lambda b, h: (h, CTX_BLK0 + b, 0)),
            pl.BlockSpec((None, 1, KEY_BLK, HEAD_SLOT), lambda b, h: (h, CTX_BLK0 + b, 0, 0)),
            pl.BlockSpec((None, 1, MLA_VE, KEY_BLK), lambda b, h: (h, CTX_BLK0 + b, 0, 0)),
        ],
        out_specs=pl.BlockSpec((MLA_V, CTX_LEN), lambda b, h: (h, b)),
        compiler_params=_cparams(("arbitrary", "arbitrary")),
        name="mla_flash_c",
    )(q, k4, vt4)
    return ot_x, ot_c


def _out_mlp_kernel(h_ref, s5_ref, otx_ref, otc_ref, dn_ref, wo_s5_ref, wo_mla_ref, wo_dn_ref,
                    g2_ref, sh_ref, sc_ref, g5_ref, nw_ref, w1_ref, w2_ref, o_ref):
    ot = jnp.where(pl.program_id(0) >= ROWS_X // TM, otc_ref[...], otx_ref[...])
    mix = jnp.dot(s5_ref[...], wo_s5_ref[...], preferred_element_type=F32)
    mix += lax.dot_general(ot, wo_mla_ref[...], (((0,), (0,)), ((), ())), preferred_element_type=F32)
    mix += jnp.dot(dn_ref[...], wo_dn_ref[...], preferred_element_type=F32)
    x1 = h_ref[...] + g2_ref[...] * mix
    xn = (_rms_rows(x1, nw_ref[...]) * (1.0 + sc_ref[...]) + sh_ref[...]).astype(BF16)
    hid = jnp.maximum(jnp.dot(xn, w1_ref[...], preferred_element_type=F32), 0.0)
    o_ref[...] = x1 + g5_ref[...] * jnp.dot((hid * hid).astype(BF16), w2_ref[...], preferred_element_type=F32)


def _out_mlp(h, s5, ot_x, ot_c, dn, wo_s5, wo_mla, wo_dn, mods, nw, w1, w2, n_rows):
    n_xtiles = ROWS_X // TM
    if ot_c is None:
        ot_c = ot_x
    mod_spec = lambda j: pl.BlockSpec((None, None, 1, D_MODEL), lambda i: (_mod_index(i), j, 0, 0))
    full = lambda a: pl.BlockSpec(a.shape, lambda i: (0,) * a.ndim)
    row = lambda n: pl.BlockSpec((TM, n), lambda i: (i, 0))
    return pl.pallas_call(
        _out_mlp_kernel,
        out_shape=jax.ShapeDtypeStruct((n_rows, D_MODEL), F32),
        grid=(n_rows // TM,),
        in_specs=[
            row(D_MODEL), row(S5_WIDTH),
            pl.BlockSpec((MLA_WIDTH, TM), lambda i: (0, jnp.minimum(i, n_xtiles - 1))),
            pl.BlockSpec((MLA_WIDTH, TM), lambda i: (0, jnp.maximum(i - n_xtiles, 0))),
            row(DN_WIDTH),
            full(wo_s5), full(wo_mla), full(wo_dn),
            mod_spec(2), mod_spec(3), mod_spec(4), mod_spec(5),
            full(nw), full(w1), full(w2),
        ],
        out_specs=row(D_MODEL),
        compiler_params=_cparams(("arbitrary",)),
        name="out_mlp",
    )(h, s5, ot_x, ot_c, dn, wo_s5, wo_mla, wo_dn, mods, mods, mods, mods, nw, w1, w2)


S5_NSTATE = 2 * S5_GROUPS * S5_STATE
S5_SLABS = S5_NSTATE // LANES
S5_CHAINS = 2 * BATCH
S5_STEPS = CTX_LEN + SEQ
S5_TB = 128
S5_PITCH = S5_TB + 4


def _s5_kernel(*refs):
    u_refs = refs[:S5_CHAINS]
    flip_ref, bmat_ref, cmat_ref, are_ref, aim_ref, yf_ref, yb_ref, st_ref, hs_ref, h_ref = refs[S5_CHAINS:]

    @pl.when(pl.program_id(0) == 0)
    def _():
        h_ref[...] = jnp.zeros_like(h_ref)

    half = S5_SLABS // 2
    flip = flip_ref[...]
    for d in range(2):
        blocks = [u_refs[d * BATCH + b][...].astype(BF16) for b in range(BATCH)]
        if d == 1:
            blocks = [jnp.dot(flip, blk, preferred_element_type=F32).astype(BF16) for blk in blocks]
        bu = jnp.dot(jnp.concatenate(blocks, axis=0), bmat_ref[d], preferred_element_type=F32)
        for b in range(BATCH):
            row0 = (d * BATCH + b) * S5_PITCH
            for j in range(S5_SLABS):
                st_ref[j, row0:row0 + S5_TB, :] = bu[b * S5_TB:(b + 1) * S5_TB, j * LANES:(j + 1) * LANES]

    a_re = [are_ref[j] for j in range(half)]
    a_im = [aim_ref[j] for j in range(half)]

    def step(s, h):
        new = []
        rows = pl.ds(s, S5_CHAINS, stride=S5_PITCH)
        for j in range(half):
            hr, hi = h[j], h[half + j]
            nr = a_re[j] * hr - a_im[j] * hi + st_ref[j, rows, :]
            ni = a_re[j] * hi + a_im[j] * hr + st_ref[half + j, rows, :]
            hs_ref[j, rows, :] = nr
            hs_ref[half + j, rows, :] = ni
            new.append((nr, ni))
        return tuple(n[0] for n in new) + tuple(n[1] for n in new)

    h = lax.fori_loop(0, S5_TB, step, tuple(h_ref[j] for j in range(S5_SLABS)), unroll=4)
    for j in range(S5_SLABS):
        h_ref[j] = h[j]

    for d, y_ref in ((0, yf_ref), (1, yb_ref)):
        parts = []
        for b in range(BATCH):
            row0 = (d * BATCH + b) * S5_PITCH
            parts.append(jnp.concatenate([hs_ref[j, row0:row0 + S5_TB, :] for j in range(S5_SLABS)], axis=1))
        hm = jnp.concatenate(parts, axis=0).astype(BF16)
        y = jnp.dot(hm, cmat_ref[d], preferred_element_type=F32)
        for b in range(BATCH):
            yb = y[b * S5_TB:(b + 1) * S5_TB]
            y_ref[b] = _exact_cols_dot(flip, yb) if d == 1 else yb


def _s5_step_block(i, b, reverse):
    n_ctx = CTX_LEN // S5_TB
    n_x = SEQ // S5_TB
    ctx0 = ROWS_X // S5_TB + b * n_ctx
    if reverse:
        return jnp.where(i < n_ctx, ctx0 + n_ctx - 1 - i, b * n_x + n_x - 1 - (i - n_ctx))
    return jnp.where(i < n_ctx, ctx0 + i, b * n_x + (i - n_ctx))


def _s5_time_block(i, reverse):
    n_ctx = CTX_LEN // S5_TB
    n_all = S5_STEPS // S5_TB
    if reverse:
        return jnp.where(i < n_ctx, n_ctx - 1 - i, n_all - 1 - (i - n_ctx))
    return i


def _s5_scan_chains(u, bmat, cmat, a_re, a_im):
    full = lambda a: pl.BlockSpec(a.shape, lambda i: (0,) * a.ndim)
    flip = jnp.eye(S5_TB, dtype=BF16)[::-1]
    u_specs = [pl.BlockSpec((S5_TB, S5_WIDTH), functools.partial(lambda i, b, rev: (_s5_step_block(i, b, rev), 0),
                                                                b=c % BATCH, rev=c >= BATCH))
               for c in range(S5_CHAINS)]
    y_shape = jax.ShapeDtypeStruct((BATCH, S5_STEPS, S5_WIDTH), F32)
    return pl.pallas_call(
        _s5_kernel,
        out_shape=[y_shape, y_shape],
        grid=(S5_STEPS // S5_TB,),
        in_specs=u_specs + [full(flip), full(bmat), full(cmat), full(a_re), full(a_im)],
        out_specs=[pl.BlockSpec((BATCH, S5_TB, S5_WIDTH), lambda i: (0, _s5_time_block(i, False), 0)),
                   pl.BlockSpec((BATCH, S5_TB, S5_WIDTH), lambda i: (0, _s5_time_block(i, True), 0))],
        scratch_shapes=[pltpu.VMEM((S5_SLABS, S5_CHAINS * S5_PITCH, LANES), F32),
                        pltpu.VMEM((S5_SLABS, S5_CHAINS * S5_PITCH, LANES), F32),
                        pltpu.VMEM((S5_SLABS, S5_CHAINS, LANES), F32)],
        compiler_params=_cparams(("arbitrary",)),
        name="s5_scan",
    )(*([u] * S5_CHAINS), flip, bmat, cmat, a_re, a_im)


def _s5_glu_kernel(u_ref, yf_ref, yb_ref, d_ref, w_ref, o_ref):
    y = d_ref[...] * u_ref[...] + yf_ref[...] + yb_ref[...]
    z = jnp.dot(jax.nn.gelu(y).astype(BF16), w_ref[...], preferred_element_type=F32)
    o_ref[...] = (z[:, :S5_WIDTH] * jax.nn.sigmoid(z[:, S5_WIDTH:])).astype(BF16)


def _s5_glu(u, yf, yb, d_skip, glu_w):
    tiles_x = SEQ // CTX_LEN

    def y_idx(i):
        is_x = i < BATCH * tiles_x
        return (jnp.where(is_x, i // tiles_x, i - BATCH * tiles_x), jnp.where(is_x, 1 + i % tiles_x, 0), 0)

    row = pl.BlockSpec((CTX_LEN, S5_WIDTH), lambda i: (i, 0))
    y_spec = pl.BlockSpec((None, CTX_LEN, S5_WIDTH), y_idx)
    return pl.pallas_call(
        _s5_glu_kernel,
        out_shape=jax.ShapeDtypeStruct((ROWS, S5_WIDTH), BF16),
        grid=(ROWS // CTX_LEN,),
        in_specs=[row, y_spec, y_spec,
                  pl.BlockSpec((1, S5_WIDTH), lambda i: (0, 0)),
                  pl.BlockSpec((S5_WIDTH, 2 * S5_WIDTH), lambda i: (0, 0))],
        out_specs=row,
        compiler_params=_cparams(("arbitrary",)),
        name="s5_glu",
    )(u, yf, yb, d_skip, glu_w)


def _s5_params(lam_re, lam_im, log_dt, b_re, b_im, c_re, c_im):
    dt = jnp.exp(log_dt)[..., None]
    mag = jnp.exp(lam_re * dt)
    a_re, a_im = mag * jnp.cos(lam_im * dt), mag * jnp.sin(lam_im * dt)
    den = lam_re * lam_re + lam_im * lam_im
    f_re = ((a_re - 1.0) * lam_re + a_im * lam_im) / den
    f_im = (a_im * lam_re - (a_re - 1.0) * lam_im) / den
    bb_re = f_re[..., None] * b_re - f_im[..., None] * b_im
    bb_im = f_re[..., None] * b_im + f_im[..., None] * b_re
    eye = jnp.eye(S5_GROUPS, dtype=F32)
    blk_b = lambda m: jnp.einsum('dgph,gk->dghkp', m, eye).reshape(2, S5_WIDTH, S5_GROUPS * S5_STATE)
    blk_c = lambda m: jnp.einsum('dghp,gk->dgpkh', m, eye).reshape(2, S5_GROUPS * S5_STATE, S5_WIDTH)
    bmat = jnp.concatenate([blk_b(bb_re), blk_b(bb_im)], axis=2).astype(BF16)
    cmat = jnp.concatenate([blk_c(c_re), -blk_c(c_im)], axis=1).astype(BF16)
    per_chain = lambda a: jnp.repeat(a.reshape(2, S5_SLABS // 2, LANES), BATCH, axis=0).transpose(1, 0, 2)
    return bmat, cmat, per_chain(a_re), per_chain(a_im)


def _s5_mixer(u, lam_re, lam_im, log_dt, b_re, b_im, c_re, c_im, d_skip, glu_w):
    bmat, cmat, a_re, a_im = _s5_params(lam_re, lam_im, log_dt, b_re, b_im, c_re, c_im)
    y_f, y_b = _s5_scan_chains(u, bmat, cmat, a_re, a_im)
    return _s5_glu(u, y_f, y_b, d_skip[None], glu_w.astype(BF16))


DN_TILE = 256
DN_HALO = 8
DN_XTILES = SEQ // DN_TILE
DN_XTILE = 512
DN_BASE = 2
DN_MERGES = (2, 4, 8, 16, 32)


def _exact_rows_dot(a, e):
    a_hi = a.astype(BF16)
    r = a - a_hi.astype(F32)
    a_mid = r.astype(BF16)
    a_lo = (r - a_mid.astype(F32)).astype(BF16)
    out = jnp.dot(a_hi, e, preferred_element_type=F32)
    out += jnp.dot(a_mid, e, preferred_element_type=F32)
    out += jnp.dot(a_lo, e, preferred_element_type=F32)
    return out


def _exact_cols_dot(e, a):
    a_hi = a.astype(BF16)
    r = a - a_hi.astype(F32)
    a_mid = r.astype(BF16)
    a_lo = (r - a_mid.astype(F32)).astype(BF16)
    out = jnp.dot(e, a_hi, preferred_element_type=F32)
    out += jnp.dot(e, a_mid, preferred_element_type=F32)
    out += jnp.dot(e, a_lo, preferred_element_type=F32)
    return out


def _dn_prep_kernel(x_ref, prev_ref, next_ref, ba_ref, cw_ref, nega_ref, dtb_ref, hsum_ref, expand_ref,
                    ltri_ref, utri_ref, qn_ref, kn_ref, v_ref, bf_ref, bb_ref, gf_ref, gb_ref, xe_ref):
    i = pl.program_id(0)
    is_ctx = i >= ROWS_X // DN_TILE
    first = jnp.logical_or(is_ctx, i % DN_XTILES == 0)
    last = jnp.logical_or(is_ctx, i % DN_XTILES == DN_XTILES - 1)
    xe_ref[0:DN_HALO, :] = jnp.where(first, 0.0, prev_ref[...])
    xe_ref[DN_HALO:DN_HALO + DN_TILE, :] = x_ref[...]
    xe_ref[DN_HALO + DN_TILE:, :] = jnp.where(last, 0.0, next_ref[...])
    pad = DN_CONV // 2
    y = jnp.zeros((DN_TILE, DN_QKV), F32)
    for j in range(DN_CONV):
        off = DN_HALO + j - pad
        y += cw_ref[j:j + 1, :] * xe_ref[off:off + DN_TILE, :]
    y = y * jax.nn.sigmoid(y)
    qk_w = DN_HEADS * DN_DK
    q, k = y[:, :qk_w], y[:, qk_w:2 * qk_w]
    hsum = hsum_ref[...]
    qn_ref[...] = q * lax.rsqrt(_exact_rows_dot(q * q, hsum) + EPS) * (DN_DK ** -0.5)
    kn_ref[...] = k * lax.rsqrt(_exact_rows_dot(k * k, hsum) + EPS)
    v_ref[...] = y[:, 2 * qk_w:]

    ba = ba_ref[...]
    lane = lax.broadcasted_iota(jnp.int32, (1, LANES), 1)
    z = ba + dtb_ref[...]
    softplus = jnp.maximum(z, 0.0) + jnp.log(1.0 + jnp.exp(-jnp.abs(z)))
    bg = jnp.where(lane < 2 * DN_HEADS, jax.nn.sigmoid(ba), nega_ref[...] * softplus)
    ex = _exact_rows_dot(bg, expand_ref[...])
    w = DN_WIDTH
    bf_ref[...] = ex[:, 0:w]
    bb_ref[...] = ex[:, w:2 * w]
    gf_ref[...] = _exact_cols_dot(ltri_ref[...], ex[:, 2 * w:3 * w])
    gb_ref[...] = _exact_cols_dot(utri_ref[...], ex[:, 3 * w:4 * w])


def _dn_prep(qkv, ba, conv_w, a_log, dt_bias):
    n_halo_blocks = ROWS // DN_HALO
    per_tile = DN_TILE // DN_HALO
    cw = jnp.concatenate([conv_w, jnp.zeros((8 - DN_CONV, DN_QKV), F32)], axis=0)
    lanes_g = jnp.zeros((LANES,), F32)
    nega = lanes_g.at[2 * DN_HEADS:4 * DN_HEADS].set(-jnp.exp(a_log.reshape(-1)))[None]
    dtb = lanes_g.at[2 * DN_HEADS:4 * DN_HEADS].set(dt_bias.reshape(-1))[None]
    head_of_lane = jnp.arange(DN_WIDTH) // DN_DV
    hsum = (head_of_lane[:, None] == head_of_lane[None, :]).astype(BF16)
    src = jnp.arange(LANES)[:, None]
    expand = jnp.concatenate(
        [(src == (g * DN_HEADS + head_of_lane)[None, :]) for g in range(4)], axis=1).astype(BF16)
    r = jnp.arange(DN_TILE)
    same_chunk = (r[:, None] // DN_CHUNK) == (r[None, :] // DN_CHUNK)
    ltri = (same_chunk & (r[:, None] >= r[None, :])).astype(BF16)
    utri = (same_chunk & (r[:, None] <= r[None, :])).astype(BF16)
    full = lambda a: pl.BlockSpec(a.shape, lambda i: (0,) * a.ndim)
    row = lambda n: pl.BlockSpec((DN_TILE, n), lambda i: (i, 0))
    return pl.pallas_call(
        _dn_prep_kernel,
        out_shape=[jax.ShapeDtypeStruct((ROWS, DN_WIDTH), F32)] * 7,
        grid=(ROWS // DN_TILE,),
        in_specs=[
            row(DN_QKV),
            pl.BlockSpec((DN_HALO, DN_QKV), lambda i: (jnp.maximum(i * per_tile - 1, 0), 0)),
            pl.BlockSpec((DN_HALO, DN_QKV), lambda i: (jnp.minimum((i + 1) * per_tile, n_halo_blocks - 1), 0)),
            row(LANES),
            full(cw), full(nega), full(dtb), full(hsum), full(expand), full(ltri), full(utri),
        ],
        out_specs=[row(DN_WIDTH)] * 7,
        scratch_shapes=[pltpu.VMEM((DN_TILE + 2 * DN_HALO, DN_QKV), F32)],
        compiler_params=_cparams(("arbitrary",)),
        name="dn_prep",
    )(qkv, qkv, qkv, ba, cw, nega, dtb, hsum, expand, ltri, utri)


def _dn_bd(y, blockmask):
    yb = y.astype(BF16)
    return jnp.where(blockmask, jnp.concatenate([yb] * DN_HEADS, axis=0), jnp.zeros((), BF16))


def _dn_prepare(chunks, blockmask, eye_cat, blk_base, off_masks):
    dot = functools.partial(jnp.dot, preferred_element_type=F32)
    bd = lambda y: _dn_bd(y, blockmask)

    a_mats, attns, egs = [], [], []
    for qn, kn, v, beta, gc, incl, strict, last in chunks:
        gcrow = jnp.sum(jnp.where(eye_cat, gc, 0.0), axis=0, keepdims=True)
        decay = jnp.exp(jnp.where(incl, gc - gcrow, -jnp.inf))
        grams = lax.dot_general(jnp.concatenate([kn, qn], axis=0).astype(BF16), bd(kn),
                                (((1,), (1,)), ((), ())), preferred_element_type=F32)
        a_mats.append(jnp.where(strict, grams[:DN_CHUNK] * beta * decay, 0.0))
        attns.append((grams[DN_CHUNK:] * decay).astype(BF16))
        egs.append(jnp.exp(gc))

    t_mats = [jnp.where(eye_cat, 1.0, 0.0) - jnp.where(blk_base, a, 0.0) for a in a_mats]
    mm1 = lambda x, y: dot(x.astype(BF16), bd(y))
    for off_mask in off_masks:
        z = [mm1(t, jnp.where(off_mask, a, 0.0)) for t, a in zip(t_mats, a_mats)]
        t_mats = [t - mm1(zi, t) for t, zi in zip(t_mats, z)]

    out = []
    for (qn, kn, v, beta, gc, incl, strict, last), t, attn, eg in zip(chunks, t_mats, attns, egs):
        tb = t.astype(BF16)
        glast = gc[last:last + 1, :]
        w_v = dot(tb, bd(v * beta))
        w_k = dot(tb, bd(kn * beta * eg))
        lhs = jnp.concatenate([w_k, qn * eg], axis=0).astype(BF16)
        k_dec = (kn * jnp.exp(glast - gc)).astype(BF16)
        out.append((w_v, lhs, attn, k_dec, jnp.exp(glast)))
    return out


def _dn_state_step(prep, s_bd, blockmask):
    w_v, lhs, attn, k_dec, e_last = prep
    dot = functools.partial(jnp.dot, preferred_element_type=F32)
    ps = dot(lhs, s_bd.astype(BF16))
    v_new = w_v - ps[:DN_CHUNK]
    o = ps[DN_CHUNK:] + dot(attn, _dn_bd(v_new, blockmask))
    upd = lax.dot_general(k_dec, v_new.astype(BF16), (((0,), (0,)), ((), ())), preferred_element_type=F32)
    return o, s_bd * e_last + jnp.where(blockmask, upd, 0.0)


def _dn_chunks_kernel(*refs, n_chunk, has_init):
    qf_ref, kf_ref, vf_ref, bf_ref, gf_ref, qb_ref, kb_ref, vb_ref, bb_ref, gb_ref = refs[:10]
    if has_init:
        s0_ref, of_ref, ob_ref, s_ref = refs[10:]
    else:
        of_ref, ob_ref, sout_ref, s_ref = refs[10:]

    @pl.when(pl.program_id(1) == 0)
    def _():
        s_ref[...] = s0_ref[...] if has_init else jnp.zeros_like(s_ref)

    row = lax.broadcasted_iota(jnp.int32, (DN_WIDTH, DN_WIDTH), 0)
    col = lax.broadcasted_iota(jnp.int32, (DN_WIDTH, DN_WIDTH), 1)
    blockmask = (row // DN_DV) == (col // DN_DV)
    i = lax.broadcasted_iota(jnp.int32, (DN_CHUNK, DN_WIDTH), 0)
    j = lax.broadcasted_iota(jnp.int32, (DN_CHUNK, DN_WIDTH), 1) % DN_DV
    eye_cat = i == j
    blk_base = (i // DN_BASE) == (j // DN_BASE)
    off_masks = tuple(((i // (2 * s)) == (j // (2 * s))) & ((i // s) != (j // s))
                      for s in DN_MERGES)
    chunks, where = [], []
    for c in range(n_chunk):
        rf = slice(c * DN_CHUNK, (c + 1) * DN_CHUNK)
        chunks.append((qf_ref[rf, :], kf_ref[rf, :], vf_ref[rf, :], bf_ref[rf, :], gf_ref[rf, :],
                       i >= j, i > j, DN_CHUNK - 1))
        where.append((0, of_ref, rf))
        cb = n_chunk - 1 - c
        rb = slice(cb * DN_CHUNK, (cb + 1) * DN_CHUNK)
        chunks.append((qb_ref[rb, :], kb_ref[rb, :], vb_ref[rb, :], bb_ref[rb, :], gb_ref[rb, :],
                       i <= j, i < j, 0))
        where.append((1, ob_ref, rb))
    prepared = _dn_prepare(chunks, blockmask, eye_cat, blk_base, off_masks)

    states = [s_ref[0], s_ref[1]]
    for prep, (d, o_ref, rows) in zip(prepared, where):
        o, states[d] = _dn_state_step(prep, states[d], blockmask)
        o_ref[rows, :] = o
    s_ref[0] = states[0]
    s_ref[1] = states[1]
    if not has_init:
        sout_ref[0] = states[0]
        sout_ref[1] = states[1]


def _dn_chunks(qn, kn, v, beta_f, beta_b, gc_f, gc_b):
    state_shape = jax.ShapeDtypeStruct((BATCH, 2, DN_WIDTH, DN_WIDTH), F32)
    state_spec = pl.BlockSpec((None, 2, DN_WIDTH, DN_WIDTH), lambda b, t: (b, 0, 0, 0))
    scratch = [pltpu.VMEM((2, DN_WIDTH, DN_WIDTH), F32)]
    args = (qn, kn, v, beta_f, gc_f, qn, kn, v, beta_b, gc_b)

    ctx_in = pl.BlockSpec((CTX_LEN, DN_WIDTH), lambda b, t: (ROWS_X // CTX_LEN + b, 0))
    ctx_out = pl.BlockSpec((CTX_LEN, DN_WIDTH), lambda b, t: (b, 0))
    o_fc, o_bc, states = pl.pallas_call(
        functools.partial(_dn_chunks_kernel, n_chunk=CTX_LEN // DN_CHUNK, has_init=False),
        out_shape=[jax.ShapeDtypeStruct((ROWS_C, DN_WIDTH), F32)] * 2 + [state_shape],
        grid=(BATCH, 1),
        in_specs=[ctx_in] * 10,
        out_specs=[ctx_out, ctx_out, state_spec],
        scratch_shapes=scratch,
        compiler_params=_cparams(("arbitrary", "arbitrary")),
        name="dn_chunks_ctx",
    )(*args)

    n_x = SEQ // DN_XTILE
    spec_f = pl.BlockSpec((DN_XTILE, DN_WIDTH), lambda b, t: (b * n_x + t, 0))
    spec_b = pl.BlockSpec((DN_XTILE, DN_WIDTH), lambda b, t: (b * n_x + n_x - 1 - t, 0))
    o_fx, o_bx = pl.pallas_call(
        functools.partial(_dn_chunks_kernel, n_chunk=DN_XTILE // DN_CHUNK, has_init=True),
        out_shape=[jax.ShapeDtypeStruct((ROWS_X, DN_WIDTH), F32)] * 2,
        grid=(BATCH, n_x),
        in_specs=[spec_f] * 5 + [spec_b] * 5 + [state_spec],
        out_specs=[spec_f, spec_b],
        scratch_shapes=scratch,
        compiler_params=_cparams(("arbitrary", "arbitrary")),
        name="dn_chunks",
    )(*args, states)
    return o_fx, o_bx, o_fc, o_bc


def _dn_out_kernel(ofx_ref, obx_ref, ofc_ref, obc_ref, gt_ref, nw_ref, hsum_ref, o_ref):
    is_ctx = pl.program_id(0) >= ROWS_X // TM
    o = jnp.where(is_ctx, ofc_ref[...] + obc_ref[...], ofx_ref[...] + obx_ref[...])
    ms = _exact_rows_dot(o * o, hsum_ref[...]) * (1.0 / DN_DV)
    gt = gt_ref[...]
    o_ref[...] = (o * lax.rsqrt(ms + EPS) * nw_ref[...] * (gt * jax.nn.sigmoid(gt))).astype(BF16)


def _dn_out(o_fx, o_bx, o_fc, o_bc, gt, norm_w):
    head_of_lane = jnp.arange(DN_WIDTH) // DN_DV
    hsum = (head_of_lane[:, None] == head_of_lane[None, :]).astype(BF16)
    nw = jnp.tile(norm_w, DN_HEADS)[None]
    n_xtiles = ROWS_X // TM
    row = pl.BlockSpec((TM, DN_WIDTH), lambda i: (i, 0))
    row_x = pl.BlockSpec((TM, DN_WIDTH), lambda i: (jnp.minimum(i, n_xtiles - 1), 0))
    row_c = pl.BlockSpec((TM, DN_WIDTH), lambda i: (jnp.maximum(i - n_xtiles, 0), 0))
    return pl.pallas_call(
        _dn_out_kernel,
        out_shape=jax.ShapeDtypeStruct((ROWS, DN_WIDTH), BF16),
        grid=(ROWS // TM,),
        in_specs=[row_x, row_x, row_c, row_c, row, pl.BlockSpec((1, DN_WIDTH), lambda i: (0, 0)),
                  pl.BlockSpec((DN_WIDTH, DN_WIDTH), lambda i: (0, 0))],
        out_specs=row,
        compiler_params=_cparams(("arbitrary",)),
        name="dn_out",
    )(o_fx, o_bx, o_fc, o_bc, gt, nw, hsum)


def _dn_mixer(qkv, gt, ba, conv_w, a_log, dt_bias, norm_w):
    qn, kn, v, beta_f, beta_b, gc_f, gc_b = _dn_prep(qkv, ba, conv_w, a_log, dt_bias)
    return _dn_out(*_dn_chunks(qn, kn, v, beta_f, beta_b, gc_f, gc_b), gt, norm_w)


def _pad_w_in(w):
    d = w.shape[0]
    s = [0]
    for n in IN_SIZES:
        s.append(s[-1] + n)
    u, cq, ckv, kr, qkv, gt, ba = (w[:, s[i]:s[i + 1]] for i in range(7))
    z = lambda n: jnp.zeros((d, n), w.dtype)
    return jnp.concatenate(
        [u, cq, ckv, z(KR_LANE), kr, z(LANES - KR_LANE - MLA_ROPE), qkv, gt, ba, z(LANES - 4 * DN_HEADS)], axis=1)


def _head_slots(w, width):
    kdim = w.shape[0]
    w = w.reshape(kdim, MLA_HEADS, width)
    return jnp.pad(w, ((0, 0), (0, 0), (0, HEAD_SLOT - width))).reshape(kdim, MLA_HEADS * HEAD_SLOT)


def _rope_swap(w):
    lane = jnp.arange(HEAD_SLOT)
    half = MLA_ROPE // 2
    src = jnp.where(lane < MLA_NOPE + half, lane + half, lane - half)
    in_rope = (lane >= MLA_NOPE) & (lane < MLA_DH)
    w3 = w.reshape(w.shape[0], -1, HEAD_SLOT)
    return jnp.where(in_rope, w3[:, :, jnp.clip(src, 0, HEAD_SLOT - 1)], jnp.zeros((), w.dtype)).reshape(w.shape)


def _rope_tables():
    rows = SEQ // GRID_W
    row = jnp.repeat(jnp.arange(rows, dtype=F32), GRID_W)
    col = jnp.tile(jnp.arange(GRID_W, dtype=F32), rows)
    inv = ROPE_BASE ** (-jnp.arange(ROPE_FREQS, dtype=F32) / ROPE_FREQS)
    ang = jnp.concatenate([row[:, None] * inv, col[:, None] * inv], axis=-1)
    cos, sin = jnp.cos(ang), jnp.sin(ang)
    ones = lambda n: jnp.ones((SEQ, n), F32)
    zeros = lambda n: jnp.zeros((SEQ, n), F32)
    cos_t = jnp.concatenate([ones(MLA_NOPE), cos, cos, ones(HEAD_SLOT - MLA_DH)], axis=1)
    sin_t = jnp.concatenate([zeros(MLA_NOPE), -sin, sin, zeros(HEAD_SLOT - MLA_DH)], axis=1)
    cos_t = jnp.concatenate([jnp.tile(cos_t, (BATCH, 1)), jnp.ones((ROWS_C, HEAD_SLOT), F32)], axis=0)
    sin_t = jnp.concatenate([jnp.tile(sin_t, (BATCH, 1)), jnp.zeros((ROWS_C, HEAD_SLOT), F32)], axis=0)
    return cos_t, sin_t


def kernel(x, c, ctx, c_ctx, w_ada, b_ada, norm1_w, w_in, s5_lambda_re, s5_lambda_im, s5_log_dt, s5_b_re, s5_b_im, s5_c_re, s5_c_im, s5_d, s5_glu_w, mla_q_norm_w, mla_kv_norm_w, mla_w_uq, mla_w_ukv, mla_q_gain, mla_k_gain, dn_conv_w, dn_a_log, dn_dt_bias, dn_norm_w, w_out, norm2_w, w_ff1, w_ff2):
    h = jnp.concatenate([x.reshape(ROWS_X, D_MODEL), ctx.reshape(ROWS_C, D_MODEL)], axis=0)
    c_pad = jnp.concatenate([c, c_ctx[None], jnp.zeros((3, D_MODEL), F32)], axis=0)
    mods_all = _ada_all(c_pad, w_ada, b_ada)
    cos_t, sin_t = _rope_tables()

    for l in range(DEPTH):
        ctx_out = l < DEPTH - 1
        mods = mods_all[l, :BATCH + 1].reshape(BATCH + 1, 6, 1, D_MODEL)
        wq = _head_slots(mla_w_uq[l], MLA_DH)
        w_ukv = mla_w_ukv[l].reshape(MLA_KV_RANK, MLA_HEADS, MLA_NOPE + MLA_V)
        wk = _head_slots(w_ukv[:, :, :MLA_NOPE].reshape(MLA_KV_RANK, MLA_HEADS * MLA_NOPE), MLA_NOPE).astype(BF16)
        wvt = w_ukv[:, :, MLA_NOPE:].reshape(MLA_KV_RANK, MLA_WIDTH).T.astype(BF16)
        qg, kg = (jnp.pad(g, (0, HEAD_SLOT - MLA_DH))[None] for g in (mla_q_gain[l], mla_k_gain[l]))
        mla_params = (mla_q_norm_w[l][None], mla_kv_norm_w[l][None], wq.astype(BF16), _rope_swap(wq).astype(BF16),
                      wk, wvt, qg, _rope_swap(qg), kg, _rope_swap(kg), _rope_swap(jnp.eye(HEAD_SLOT, dtype=BF16)))
        u, qkv, gt, ba, q, k4, vt4 = _in_proj(h, norm1_w[l][None], mods, _pad_w_in(w_in[l]).astype(BF16),
                                              cos_t, sin_t, mla_params)

        s5 = _s5_mixer(u, s5_lambda_re[l], s5_lambda_im[l], s5_log_dt[l], s5_b_re[l], s5_b_im[l],
                       s5_c_re[l], s5_c_im[l], s5_d[l], s5_glu_w[l])
        ot_x, ot_c = _flash(q, k4, vt4, ctx_out)
        dn = _dn_mixer(qkv, gt, ba, dn_conv_w[l], dn_a_log[l], dn_dt_bias[l], dn_norm_w[l])

        wo = w_out[l].astype(BF16)
        n_rows = ROWS if ctx_out else ROWS_X
        h = _out_mlp(h, s5, ot_x, ot_c, dn, wo[:S5_WIDTH], wo[S5_WIDTH:S5_WIDTH + MLA_WIDTH], wo[S5_WIDTH + MLA_WIDTH:],
                     mods, norm2_w[l][None], w_ff1[l].astype(BF16), w_ff2[l].astype(BF16), n_rows)
    return h.reshape(BATCH, SEQ, D_MODEL)
```

```python
import functools

import jax
import jax.numpy as jnp
from jax import lax
from jax.experimental import pallas as pl
from jax.experimental.pallas import tpu as pltpu

F32 = jnp.float32
BF16 = jnp.bfloat16

D_MODEL = 1024
BATCH = 4
SEQ = 4096
DEPTH = 4
GRID_W = 64
CTX_LEN = 256
EPS = 1e-6

S5_GROUP = 16
S5_GROUPS = 16
S5_WIDTH = 256
S5_STATE = 64

MLA_HEADS = 8
MLA_NOPE = 64
MLA_ROPE = 32
MLA_V = 64
MLA_VE = MLA_V + 16
MLA_DH = 96
MLA_Q_RANK = 256
MLA_KV_RANK = 128
MLA_WIDTH = 512
ROPE_BASE = 10000.0
ROPE_FREQS = 8

DN_HEADS = 4
DN_DK = 64
DN_DV = 64
DN_QKV = 768
DN_WIDTH = 256
DN_CONV = 5
DN_CHUNK = 64

D_FF = 4 * D_MODEL
IN_SIZES = (S5_WIDTH, MLA_Q_RANK, MLA_KV_RANK, MLA_ROPE, DN_QKV, DN_WIDTH, 4 * DN_HEADS)

ROWS_X = BATCH * SEQ
ROWS_C = BATCH * CTX_LEN
ROWS = ROWS_X + ROWS_C
LANES = 128
HEAD_SLOT = LANES
KEY_BLK = 256
N_XBLK = SEQ // KEY_BLK
CTX_BLK0 = ROWS_X // KEY_BLK

IN_PAD = 1920
COL_U, COL_CQ, COL_CKV, COL_KR, COL_QKV, COL_GT, COL_BA = 0, 256, 512, 640, 768, 1536, 1792
KR_LANE = MLA_NOPE

TM = 512
TG = 1024
TQ = 4096
FLASH_KB = 2
ADA_TN = 1536
VMEM_LIMIT = 56 * 1024 * 1024

LOG2E = 1.4426950408889634


def _cparams(sem):
    return pltpu.CompilerParams(dimension_semantics=sem, vmem_limit_bytes=VMEM_LIMIT)


def _split3_dot(a, w):
    a_hi = a.astype(BF16)
    a_lo = (a - a_hi.astype(F32)).astype(BF16)
    w_hi = w.astype(BF16)
    w_lo = (w - w_hi.astype(F32)).astype(BF16)
    out = jnp.dot(a_hi, w_hi, preferred_element_type=F32)
    out += jnp.dot(a_lo, w_hi, preferred_element_type=F32)
    out += jnp.dot(a_hi, w_lo, preferred_element_type=F32)
    return out


def _rms_rows(x, w):
    return x * lax.rsqrt(jnp.mean(x * x, axis=-1, keepdims=True) + EPS) * w


def _ada_kernel(c_ref, w_ref, b_ref, o_ref):
    c = c_ref[...]
    s = c * jax.nn.sigmoid(c)
    o_ref[...] = _split3_dot(s, w_ref[...]) + b_ref[...]


def _ada_all(c_pad, w_ada, b_ada):
    tn = ADA_TN
    return pl.pallas_call(
        _ada_kernel,
        out_shape=jax.ShapeDtypeStruct((DEPTH, 8, 6 * D_MODEL), F32),
        grid=(DEPTH, 6 * D_MODEL // tn),
        in_specs=[
            pl.BlockSpec((8, D_MODEL), lambda l, j: (0, 0)),
            pl.BlockSpec((None, D_MODEL, tn), lambda l, j: (l, 0, j)),
            pl.BlockSpec((None, 1, tn), lambda l, j: (l, 0, j)),
        ],
        out_specs=pl.BlockSpec((None, 8, tn), lambda l, j: (l, 0, j)),
        compiler_params=_cparams(("arbitrary", "arbitrary")),
        name="ada_mod",
    )(c_pad, w_ada, b_ada.reshape(DEPTH, 1, 6 * D_MODEL))


def _mod_index(i):
    return jnp.minimum(i // (SEQ // TM), BATCH)


def _mla_heads(m, cos, sin, qnw, kvnw, wq, wq_sw, wk, wvt, qg, qg_sw, kg, kg_sw, perm, q_ref, k_ref, vt_ref):
    cqn = _rms_rows(m[:, 0:MLA_Q_RANK], qnw).astype(BF16)
    ckvn = _rms_rows(m[:, MLA_Q_RANK:MLA_Q_RANK + MLA_KV_RANK], kvnw).astype(BF16)
    kr = m[:, MLA_Q_RANK + MLA_KV_RANK:]
    qf = jnp.dot(cqn, wq, preferred_element_type=F32)
    qf_sw = jnp.dot(cqn, wq_sw, preferred_element_type=F32)
    kf = jnp.dot(ckvn, wk, preferred_element_type=F32)
    vt = lax.dot_general(wvt, ckvn, (((1,), (1,)), ((), ())), preferred_element_type=F32)
    kr_sw = _exact_rows_dot(kr, perm)
    q_cos, q_sin = cos * qg, sin * qg_sw
    k_cos, k_sin = cos * kg, sin * kg_sw

    def head_norm_rope(xh, xh_sw, g_cos, g_sin, scale):
        ss = jnp.sum(xh * xh, axis=-1, keepdims=True) * (1.0 / MLA_DH)
        return (xh * g_cos + xh_sw * g_sin) * (lax.rsqrt(ss + EPS) * scale)

    n_sub = m.shape[0] // KEY_BLK
    ones = jnp.ones((MLA_VE - MLA_V, m.shape[0]), BF16)
    for h in range(MLA_HEADS):
        sl = slice(h * HEAD_SLOT, (h + 1) * HEAD_SLOT)
        q_ref[h] = head_norm_rope(qf[:, sl], qf_sw[:, sl], q_cos, q_sin, MLA_DH ** -0.5 * LOG2E).astype(BF16)
        kh = head_norm_rope(kf[:, sl] + kr, kr_sw, k_cos, k_sin, 1.0).astype(BF16)
        vh = jnp.concatenate([vt[h * MLA_V:(h + 1) * MLA_V, :].astype(BF16), ones], axis=0)
        for s in range(n_sub):
            k_ref[h, s] = kh[s * KEY_BLK:(s + 1) * KEY_BLK, :]
            vt_ref[h, s] = vh[:, s * KEY_BLK:(s + 1) * KEY_BLK]


def _in_proj_kernel(h_ref, nw_ref, sh_ref, sc_ref, w_ref, cos_ref, sin_ref, qnw_ref, kvnw_ref, wq_ref, wqs_ref,
                    wk_ref, wvt_ref, qg_ref, qgs_ref, kg_ref, kgs_ref, perm_ref,
                    u_ref, qkv_ref, gt_ref, ba_ref, q_ref, k_ref, vt_ref):
    y = _rms_rows(h_ref[...], nw_ref[...]) * (1.0 + sc_ref[...]) + sh_ref[...]
    z = jnp.dot(y.astype(BF16), w_ref[...], preferred_element_type=F32)
    u_ref[...] = z[:, COL_U:COL_CQ]
    qkv_ref[...] = z[:, COL_QKV:COL_GT]
    gt_ref[...] = z[:, COL_GT:COL_BA]
    ba_ref[...] = z[:, COL_BA:IN_PAD]
    _mla_heads(z[:, COL_CQ:COL_QKV], cos_ref[...], sin_ref[...], qnw_ref[...], kvnw_ref[...], wq_ref[...],
               wqs_ref[...], wk_ref[...], wvt_ref[...], qg_ref[...], qgs_ref[...], kg_ref[...], kgs_ref[...],
               perm_ref[...], q_ref, k_ref, vt_ref)


def _in_proj(h, nw, mods, w_pad, cos_t, sin_t, mla_params):
    n_sub = TM // KEY_BLK
    mod_spec = lambda j: pl.BlockSpec((None, None, 1, D_MODEL), lambda i: (_mod_index(i), j, 0, 0))
    full = lambda a: pl.BlockSpec(a.shape, lambda i: (0,) * a.ndim)
    row = lambda n: pl.BlockSpec((TM, n), lambda i: (i, 0))
    row_widths = (COL_CQ - COL_U, COL_GT - COL_QKV, COL_BA - COL_GT, IN_PAD - COL_BA)
    return pl.pallas_call(
        _in_proj_kernel,
        out_shape=[jax.ShapeDtypeStruct((ROWS, n), F32) for n in row_widths] + [
            jax.ShapeDtypeStruct((MLA_HEADS, ROWS, HEAD_SLOT), BF16),
            jax.ShapeDtypeStruct((MLA_HEADS, ROWS // KEY_BLK, KEY_BLK, HEAD_SLOT), BF16),
            jax.ShapeDtypeStruct((MLA_HEADS, ROWS // KEY_BLK, MLA_VE, KEY_BLK), BF16),
        ],
        grid=(ROWS // TM,),
        in_specs=[row(D_MODEL), full(nw), mod_spec(0), mod_spec(1), full(w_pad), row(HEAD_SLOT), row(HEAD_SLOT)]
        + [full(p) for p in mla_params],
        out_specs=[row(n) for n in row_widths] + [
            pl.BlockSpec((MLA_HEADS, TM, HEAD_SLOT), lambda i: (0, i, 0)),
            pl.BlockSpec((MLA_HEADS, n_sub, KEY_BLK, HEAD_SLOT), lambda i: (0, i, 0, 0)),
            pl.BlockSpec((MLA_HEADS, n_sub, MLA_VE, KEY_BLK), lambda i: (0, i, 0, 0)),
        ],
        compiler_params=_cparams(("arbitrary",)),
        name="in_proj",
    )(h, nw, mods, mods, w_pad, cos_t, sin_t, *mla_params)


def _scores(k, q):
    return lax.dot_general(k, q, (((1,), (1,)), ((), ())), preferred_element_type=F32)


def _softmax_update(carry, s, vts):
    m, acc = carry
    m_new = jnp.maximum(m, jnp.max(s, axis=0, keepdims=True))
    pb = jnp.exp2(s - m_new).astype(BF16)
    acc = jnp.exp2(m - m_new) * acc
    for t, vt in enumerate(vts):
        acc += jnp.dot(vt, pb[t * KEY_BLK:(t + 1) * KEY_BLK], preferred_element_type=F32)
    return m_new, acc


def _softmax_attend(q, key_blocks):
    tq = q.shape[0]
    carry = (jnp.full((1, tq), -jnp.inf, F32), jnp.zeros((MLA_VE, tq), F32))
    s_next = _scores(key_blocks[0][0], q)
    for n, (_, vts) in enumerate(key_blocks):
        s = s_next
        if n + 1 < len(key_blocks):
            s_next = _scores(key_blocks[n + 1][0], q)
        carry = _softmax_update(carry, s, vts)
    _, acc = carry
    return (acc[:MLA_V] * (1.0 / acc[MLA_V:MLA_V + 1])).astype(BF16)


def _flash_x_kernel(q_ref, kx_ref, kc_ref, vtx_ref, vtc_ref, o_ref):
    blocks = [(kc_ref[0], [vtc_ref[0]])]
    for j in range(N_XBLK // FLASH_KB):
        k = kx_ref[j * FLASH_KB:(j + 1) * FLASH_KB].reshape(FLASH_KB * KEY_BLK, HEAD_SLOT)
        blocks.append((k, [vtx_ref[j * FLASH_KB + t] for t in range(FLASH_KB)]))
    o_ref[...] = _softmax_attend(q_ref[...], blocks)


def _flash_c_kernel(q_ref, kc_ref, vtc_ref, o_ref):
    o_ref[...] = _softmax_attend(q_ref[...], [(kc_ref[0], [vtc_ref[0]])])


def _flash(q, k4, vt4, ctx_out):
    n_qx = SEQ // TQ
    ot_x = pl.pallas_call(
        _flash_x_kernel,
        out_shape=jax.ShapeDtypeStruct((MLA_WIDTH, ROWS_X), BF16),
        grid=(BATCH, MLA_HEADS, n_qx),
        in_specs=[
            pl.BlockSpec((None, TQ, HEAD_SLOT), lambda b, h, qi: (h, b * n_qx + qi, 0)),
            pl.BlockSpec((None, N_XBLK, KEY_BLK, HEAD_SLOT), lambda b, h, qi: (h, b, 0, 0)),
            pl.BlockSpec((None, 1, KEY_BLK, HEAD_SLOT), lambda b, h, qi: (h, CTX_BLK0 + b, 0, 0)),
            pl.BlockSpec((None, N_XBLK, MLA_VE, KEY_BLK), lambda b, h, qi: (h, b, 0, 0)),
            pl.BlockSpec((None, 1, MLA_VE, KEY_BLK), lambda b, h, qi: (h, CTX_BLK0 + b, 0, 0)),
        ],
        out_specs=pl.BlockSpec((MLA_V, TQ), lambda b, h, qi: (h, b * n_qx + qi)),
        compiler_params=_cparams(("arbitrary", "arbitrary", "arbitrary")),
        name="mla_flash_x",
    )(q, k4, k4, vt4, vt4)
    if not ctx_out:
        return ot_x, None
    ot_c = pl.pallas_call(
        _flash_c_kernel,
        out_shape=jax.ShapeDtypeStruct((MLA_WIDTH, ROWS_C), BF16),
        grid=(BATCH, MLA_HEADS),
        in_specs=[
            pl.BlockSpec((None, CTX_LEN, HEAD_SLOT), lambda b, h: (h, CTX_BLK0 + b, 0)),
            pl.BlockSpec((None, 1, KEY_BLK, HEAD_SLOT), lambda b, h: (h, CTX_BLK0 + b, 0, 0)),
            pl.BlockSpec((None, 1, MLA_VE, KEY_BLK), lambda b, h: (h, CTX_BLK0 + b, 0, 0)),
        ],
        out_specs=pl.BlockSpec((MLA_V, CTX_LEN), lambda b, h: (h, b)),
        compiler_params=_cparams(("arbitrary", "arbitrary")),
        name="mla_flash_c",
    )(q, k4, vt4)
    return ot_x, ot_c


def _out_mlp_kernel(h_ref, s5x_ref, s5c_ref, otx_ref, otc_ref, dnx_ref, dnc_ref, wo_s5_ref, wo_mla_ref, wo_dn_ref,
                    g2_ref, sh_ref, sc_ref, g5_ref, nw_ref, w1_ref, w2_ref, o_ref):
    is_ctx = pl.program_id(0) >= ROWS_X // TM
    s5 = jnp.where(is_ctx, s5c_ref[...], s5x_ref[...])
    ot = jnp.where(is_ctx, otc_ref[...], otx_ref[...])
    dn = jnp.where(is_ctx, dnc_ref[...], dnx_ref[...])
    mix = jnp.dot(s5, wo_s5_ref[...], preferred_element_type=F32)
    mix += lax.dot_general(ot, wo_mla_ref[...], (((0,), (0,)), ((), ())), preferred_element_type=F32)
    mix += jnp.dot(dn, wo_dn_ref[...], preferred_element_type=F32)
    x1 = h_ref[...] + g2_ref[...] * mix
    xn = (_rms_rows(x1, nw_ref[...]) * (1.0 + sc_ref[...]) + sh_ref[...]).astype(BF16)
    hid = jnp.maximum(jnp.dot(xn, w1_ref[...], preferred_element_type=F32), 0.0)
    o_ref[...] = x1 + g5_ref[...] * jnp.dot((hid * hid).astype(BF16), w2_ref[...], preferred_element_type=F32)


def _out_mlp(h, s5, ot, dn, wo_s5, wo_mla, wo_dn, mods, nw, w1, w2, n_rows):
    n_xtiles = ROWS_X // TM
    (s5_x, s5_c), (ot_x, ot_c), (dn_x, dn_c) = ((a, a if n_rows == ROWS_X else c) for a, c in (s5, ot, dn))
    x_rows = lambda n: pl.BlockSpec((TM, n), lambda i: (jnp.minimum(i, n_xtiles - 1), 0))
    c_rows = lambda n: pl.BlockSpec((TM, n), lambda i: (jnp.maximum(i - n_xtiles, 0), 0))
    mod_spec = lambda j: pl.BlockSpec((None, None, 1, D_MODEL), lambda i: (_mod_index(i), j, 0, 0))
    full = lambda a: pl.BlockSpec(a.shape, lambda i: (0,) * a.ndim)
    row = lambda n: pl.BlockSpec((TM, n), lambda i: (i, 0))
    return pl.pallas_call(
        _out_mlp_kernel,
        out_shape=jax.ShapeDtypeStruct((n_rows, D_MODEL), F32),
        grid=(n_rows // TM,),
        in_specs=[
            row(D_MODEL), x_rows(S5_WIDTH), c_rows(S5_WIDTH),
            pl.BlockSpec((MLA_WIDTH, TM), lambda i: (0, jnp.minimum(i, n_xtiles - 1))),
            pl.BlockSpec((MLA_WIDTH, TM), lambda i: (0, jnp.maximum(i - n_xtiles, 0))),
            x_rows(DN_WIDTH), c_rows(DN_WIDTH),
            full(wo_s5), full(wo_mla), full(wo_dn),
            mod_spec(2), mod_spec(3), mod_spec(4), mod_spec(5),
            full(nw), full(w1), full(w2),
        ],
        out_specs=row(D_MODEL),
        compiler_params=_cparams(("arbitrary",)),
        name="out_mlp",
    )(h, s5_x, s5_c, ot_x, ot_c, dn_x, dn_c, wo_s5, wo_mla, wo_dn, mods, mods, mods, mods, nw, w1, w2)


S5_NSTATE = 2 * S5_GROUPS * S5_STATE
S5_SLABS = S5_NSTATE // LANES
S5_CHAINS = 2 * BATCH
S5_STEPS = CTX_LEN + SEQ
S5_TB = 128
S5_PITCH = S5_TB + 4


def _s5_kernel(*refs):
    u_refs = refs[:S5_CHAINS]
    flip_ref, bmat_ref, cmat_ref, are_ref, aim_ref, yf_ref, yb_ref, st_ref, hs_ref, h_ref = refs[S5_CHAINS:]

    @pl.when(pl.program_id(0) == 0)
    def _():
        h_ref[...] = jnp.zeros_like(h_ref)

    half = S5_SLABS // 2
    flip = flip_ref[...]
    for d in range(2):
        blocks = [u_refs[d * BATCH + b][...].astype(BF16) for b in range(BATCH)]
        if d == 1:
            blocks = [jnp.dot(flip, blk, preferred_element_type=F32).astype(BF16) for blk in blocks]
        bu = jnp.dot(jnp.concatenate(blocks, axis=0), bmat_ref[d], preferred_element_type=F32)
        for b in range(BATCH):
            row0 = (d * BATCH + b) * S5_PITCH
            for j in range(S5_SLABS):
                st_ref[j, row0:row0 + S5_TB, :] = bu[b * S5_TB:(b + 1) * S5_TB, j * LANES:(j + 1) * LANES]

    a_re = [are_ref[j] for j in range(half)]
    a_im = [aim_ref[j] for j in range(half)]

    def step(s, h):
        new = []
        rows = pl.ds(s, S5_CHAINS, stride=S5_PITCH)
        for j in range(half):
            hr, hi = h[j], h[half + j]
            nr = a_re[j] * hr - a_im[j] * hi + st_ref[j, rows, :]
            ni = a_re[j] * hi + a_im[j] * hr + st_ref[half + j, rows, :]
            hs_ref[j, rows, :] = nr
            hs_ref[half + j, rows, :] = ni
            new.append((nr, ni))
        return tuple(n[0] for n in new) + tuple(n[1] for n in new)

    h = lax.fori_loop(0, S5_TB, step, tuple(h_ref[j] for j in range(S5_SLABS)), unroll=4)
    for j in range(S5_SLABS):
        h_ref[j] = h[j]

    for d, y_ref in ((0, yf_ref), (1, yb_ref)):
        parts = []
        for b in range(BATCH):
            row0 = (d * BATCH + b) * S5_PITCH
            parts.append(jnp.concatenate([hs_ref[j, row0:row0 + S5_TB, :] for j in range(S5_SLABS)], axis=1))
        hm = jnp.concatenate(parts, axis=0).astype(BF16)
        y = jnp.dot(hm, cmat_ref[d], preferred_element_type=F32)
        for b in range(BATCH):
            yb = y[b * S5_TB:(b + 1) * S5_TB]
            y_ref[b] = _exact_cols_dot(flip, yb) if d == 1 else yb


def _s5_step_block(i, b, reverse):
    n_ctx = CTX_LEN // S5_TB
    n_x = SEQ // S5_TB
    ctx0 = ROWS_X // S5_TB + b * n_ctx
    if reverse:
        return jnp.where(i < n_ctx, ctx0 + n_ctx - 1 - i, b * n_x + n_x - 1 - (i - n_ctx))
    return jnp.where(i < n_ctx, ctx0 + i, b * n_x + (i - n_ctx))


def _s5_time_block(i, reverse):
    n_ctx = CTX_LEN // S5_TB
    n_x = SEQ // S5_TB
    if reverse:
        return jnp.where(i < n_ctx, n_x + n_ctx - 1 - i, n_x - 1 - (i - n_ctx))
    return jnp.where(i < n_ctx, n_x + i, i - n_ctx)


def _s5_scan_chains(u, bmat, cmat, a_re, a_im):
    full = lambda a: pl.BlockSpec(a.shape, lambda i: (0,) * a.ndim)
    flip = jnp.eye(S5_TB, dtype=BF16)[::-1]
    u_specs = [pl.BlockSpec((S5_TB, S5_WIDTH), functools.partial(lambda i, b, rev: (_s5_step_block(i, b, rev), 0),
                                                                b=c % BATCH, rev=c >= BATCH))
               for c in range(S5_CHAINS)]
    y_shape = jax.ShapeDtypeStruct((BATCH, S5_STEPS, S5_WIDTH), F32)
    return pl.pallas_call(
        _s5_kernel,
        out_shape=[y_shape, y_shape],
        grid=(S5_STEPS // S5_TB,),
        in_specs=u_specs + [full(flip), full(bmat), full(cmat), full(a_re), full(a_im)],
        out_specs=[pl.BlockSpec((BATCH, S5_TB, S5_WIDTH), lambda i: (0, _s5_time_block(i, False), 0)),
                   pl.BlockSpec((BATCH, S5_TB, S5_WIDTH), lambda i: (0, _s5_time_block(i, True), 0))],
        scratch_shapes=[pltpu.VMEM((S5_SLABS, S5_CHAINS * S5_PITCH, LANES), F32),
                        pltpu.VMEM((S5_SLABS, S5_CHAINS * S5_PITCH, LANES), F32),
                        pltpu.VMEM((S5_SLABS, S5_CHAINS, LANES), F32)],
        compiler_params=_cparams(("arbitrary",)),
        name="s5_scan",
    )(*([u] * S5_CHAINS), flip, bmat, cmat, a_re, a_im)


def _s5_glu_kernel(u_ref, yf_ref, yb_ref, d_ref, w_ref, o_ref):
    y = d_ref[...] * u_ref[...] + yf_ref[...] + yb_ref[...]
    z = jnp.dot(jax.nn.gelu(y).astype(BF16), w_ref[...], preferred_element_type=F32)
    o_ref[...] = (z[:, :S5_WIDTH] * jax.nn.sigmoid(z[:, S5_WIDTH:])).astype(BF16)


def _s5_glu(u, yf, yb, d_skip, glu_w):
    tiles_b = SEQ // TG
    consts = [pl.BlockSpec((1, S5_WIDTH), lambda i: (0, 0)), pl.BlockSpec((S5_WIDTH, 2 * S5_WIDTH), lambda i: (0, 0))]
    y_x = pl.BlockSpec((None, TG, S5_WIDTH), lambda i: (i // tiles_b, i % tiles_b, 0))
    s5_x = pl.pallas_call(
        _s5_glu_kernel,
        out_shape=jax.ShapeDtypeStruct((ROWS_X, S5_WIDTH), BF16),
        grid=(ROWS_X // TG,),
        in_specs=[pl.BlockSpec((TG, S5_WIDTH), lambda i: (i, 0)), y_x, y_x] + consts,
        out_specs=pl.BlockSpec((TG, S5_WIDTH), lambda i: (i, 0)),
        compiler_params=_cparams(("arbitrary",)),
        name="s5_glu",
    )(u, yf, yb, d_skip, glu_w)
    y_c = pl.BlockSpec((None, CTX_LEN, S5_WIDTH), lambda b: (b, SEQ // CTX_LEN, 0))
    s5_c = pl.pallas_call(
        _s5_glu_kernel,
        out_shape=jax.ShapeDtypeStruct((ROWS_C, S5_WIDTH), BF16),
        grid=(BATCH,),
        in_specs=[pl.BlockSpec((CTX_LEN, S5_WIDTH), lambda b: (ROWS_X // CTX_LEN + b, 0)), y_c, y_c] + consts,
        out_specs=pl.BlockSpec((CTX_LEN, S5_WIDTH), lambda b: (b, 0)),
        compiler_params=_cparams(("arbitrary",)),
        name="s5_glu_ctx",
    )(u, yf, yb, d_skip, glu_w)
    return s5_x, s5_c


def _s5_params(lam_re, lam_im, log_dt, b_re, b_im, c_re, c_im):
    dt = jnp.exp(log_dt)[..., None]
    mag = jnp.exp(lam_re * dt)
    a_re, a_im = mag * jnp.cos(lam_im * dt), mag * jnp.sin(lam_im * dt)
    den = lam_re * lam_re + lam_im * lam_im
    f_re = ((a_re - 1.0) * lam_re + a_im * lam_im) / den
    f_im = (a_im * lam_re - (a_re - 1.0) * lam_im) / den
    bb_re = f_re[..., None] * b_re - f_im[..., None] * b_im
    bb_im = f_re[..., None] * b_im + f_im[..., None] * b_re
    eye = jnp.eye(S5_GROUPS, dtype=F32)
    blk_b = lambda m: jnp.einsum('dgph,gk->dghkp', m, eye).reshape(2, S5_WIDTH, S5_GROUPS * S5_STATE)
    blk_c = lambda m: jnp.einsum('dghp,gk->dgpkh', m, eye).reshape(2, S5_GROUPS * S5_STATE, S5_WIDTH)
    bmat = jnp.concatenate([blk_b(bb_re), blk_b(bb_im)], axis=2).astype(BF16)
    cmat = jnp.concatenate([blk_c(c_re), -blk_c(c_im)], axis=1).astype(BF16)
    per_chain = lambda a: jnp.repeat(a.reshape(2, S5_SLABS // 2, LANES), BATCH, axis=0).transpose(1, 0, 2)
    return bmat, cmat, per_chain(a_re), per_chain(a_im)


def _s5_mixer(u, lam_re, lam_im, log_dt, b_re, b_im, c_re, c_im, d_skip, glu_w):
    bmat, cmat, a_re, a_im = _s5_params(lam_re, lam_im, log_dt, b_re, b_im, c_re, c_im)
    y_f, y_b = _s5_scan_chains(u, bmat, cmat, a_re, a_im)
    return _s5_glu(u, y_f, y_b, d_skip[None], glu_w.astype(BF16))


DN_TILE = 256
DN_HALO = 8
DN_XTILES = SEQ // DN_TILE
DN_XTILE = 512
DN_BASE = 2
DN_MERGES = (2, 4, 8, 16, 32)


def _exact_rows_dot(a, e):
    a_hi = a.astype(BF16)
    r = a - a_hi.astype(F32)
    a_mid = r.astype(BF16)
    a_lo = (r - a_mid.astype(F32)).astype(BF16)
    out = jnp.dot(a_hi, e, preferred_element_type=F32)
    out += jnp.dot(a_mid, e, preferred_element_type=F32)
    out += jnp.dot(a_lo, e, preferred_element_type=F32)
    return out


def _exact_cols_dot(e, a):
    a_hi = a.astype(BF16)
    r = a - a_hi.astype(F32)
    a_mid = r.astype(BF16)
    a_lo = (r - a_mid.astype(F32)).astype(BF16)
    out = jnp.dot(e, a_hi, preferred_element_type=F32)
    out += jnp.dot(e, a_mid, preferred_element_type=F32)
    out += jnp.dot(e, a_lo, preferred_element_type=F32)
    return out


def _dn_prep_kernel(x_ref, prev_ref, next_ref, ba_ref, cw_ref, nega_ref, dtb_ref, hsum_ref, expand_ref,
                    ltri_ref, utri_ref, qn_ref, kn_ref, v_ref, bf_ref, bb_ref, gf_ref, gb_ref, xe_ref):
    i = pl.program_id(0)
    is_ctx = i >= ROWS_X // DN_TILE
    first = jnp.logical_or(is_ctx, i % DN_XTILES == 0)
    last = jnp.logical_or(is_ctx, i % DN_XTILES == DN_XTILES - 1)
    xe_ref[0:DN_HALO, :] = jnp.where(first, 0.0, prev_ref[...])
    xe_ref[DN_HALO:DN_HALO + DN_TILE, :] = x_ref[...]
    xe_ref[DN_HALO + DN_TILE:, :] = jnp.where(last, 0.0, next_ref[...])
    pad = DN_CONV // 2
    y = jnp.zeros((DN_TILE, DN_QKV), F32)
    for j in range(DN_CONV):
        off = DN_HALO + j - pad
        y += cw_ref[j:j + 1, :] * xe_ref[off:off + DN_TILE, :]
    y = y * jax.nn.sigmoid(y)
    qk_w = DN_HEADS * DN_DK
    q, k = y[:, :qk_w], y[:, qk_w:2 * qk_w]
    hsum = hsum_ref[...]
    qn_ref[...] = q * lax.rsqrt(_exact_rows_dot(q * q, hsum) + EPS) * (DN_DK ** -0.5)
    kn_ref[...] = k * lax.rsqrt(_exact_rows_dot(k * k, hsum) + EPS)
    v_ref[...] = y[:, 2 * qk_w:]

    ba = ba_ref[...]
    lane = lax.broadcasted_iota(jnp.int32, (1, LANES), 1)
    z = ba + dtb_ref[...]
    softplus = jnp.maximum(z, 0.0) + jnp.log(1.0 + jnp.exp(-jnp.abs(z)))
    bg = jnp.where(lane < 2 * DN_HEADS, jax.nn.sigmoid(ba), nega_ref[...] * softplus)
    ex = _exact_rows_dot(bg, expand_ref[...])
    w = DN_WIDTH
    bf_ref[...] = ex[:, 0:w]
    bb_ref[...] = ex[:, w:2 * w]
    gf_ref[...] = _exact_cols_dot(ltri_ref[...], ex[:, 2 * w:3 * w])
    gb_ref[...] = _exact_cols_dot(utri_ref[...], ex[:, 3 * w:4 * w])


def _dn_prep(qkv, ba, conv_w, a_log, dt_bias):
    n_halo_blocks = ROWS // DN_HALO
    per_tile = DN_TILE // DN_HALO
    cw = jnp.concatenate([conv_w, jnp.zeros((8 - DN_CONV, DN_QKV), F32)], axis=0)
    lanes_g = jnp.zeros((LANES,), F32)
    nega = lanes_g.at[2 * DN_HEADS:4 * DN_HEADS].set(-jnp.exp(a_log.reshape(-1)))[None]
    dtb = lanes_g.at[2 * DN_HEADS:4 * DN_HEADS].set(dt_bias.reshape(-1))[None]
    head_of_lane = jnp.arange(DN_WIDTH) // DN_DV
    hsum = (head_of_lane[:, None] == head_of_lane[None, :]).astype(BF16)
    src = jnp.arange(LANES)[:, None]
    expand = jnp.concatenate(
        [(src == (g * DN_HEADS + head_of_lane)[None, :]) for g in range(4)], axis=1).astype(BF16)
    r = jnp.arange(DN_TILE)
    same_chunk = (r[:, None] // DN_CHUNK) == (r[None, :] // DN_CHUNK)
    ltri = (same_chunk & (r[:, None] >= r[None, :])).astype(BF16)
    utri = (same_chunk & (r[:, None] <= r[None, :])).astype(BF16)
    full = lambda a: pl.BlockSpec(a.shape, lambda i: (0,) * a.ndim)
    row = lambda n: pl.BlockSpec((DN_TILE, n), lambda i: (i, 0))
    return pl.pallas_call(
        _dn_prep_kernel,
        out_shape=[jax.ShapeDtypeStruct((ROWS, DN_WIDTH), F32)] * 7,
        grid=(ROWS // DN_TILE,),
        in_specs=[
            row(DN_QKV),
            pl.BlockSpec((DN_HALO, DN_QKV), lambda i: (jnp.maximum(i * per_tile - 1, 0), 0)),
            pl.BlockSpec((DN_HALO, DN_QKV), lambda i: (jnp.minimum((i + 1) * per_tile, n_halo_blocks - 1), 0)),
            row(LANES),
            full(cw), full(nega), full(dtb), full(hsum), full(expand), full(ltri), full(utri),
        ],
        out_specs=[row(DN_WIDTH)] * 7,
        scratch_shapes=[pltpu.VMEM((DN_TILE + 2 * DN_HALO, DN_QKV), F32)],
        compiler_params=_cparams(("arbitrary",)),
        name="dn_prep",
    )(qkv, qkv, qkv, ba, cw, nega, dtb, hsum, expand, ltri, utri)


def _dn_bd(y, blockmask):
    yb = y.astype(BF16)
    return jnp.where(blockmask, jnp.concatenate([yb] * DN_HEADS, axis=0), jnp.zeros((), BF16))


def _dn_prepare(chunks, blockmask, eye_cat, blk_base, off_masks):
    dot = functools.partial(jnp.dot, preferred_element_type=F32)
    bd = lambda y: _dn_bd(y, blockmask)

    a_mats, attns, egs = [], [], []
    for qn, kn, v, beta, gc, incl, strict, last in chunks:
        gcrow = jnp.sum(jnp.where(eye_cat, gc, 0.0), axis=0, keepdims=True)
        decay = jnp.exp(jnp.where(incl, gc - gcrow, -jnp.inf))
        grams = lax.dot_general(jnp.concatenate([kn, qn], axis=0).astype(BF16), bd(kn),
                                (((1,), (1,)), ((), ())), preferred_element_type=F32)
        a_mats.append(jnp.where(strict, grams[:DN_CHUNK] * beta * decay, 0.0))
        attns.append((grams[DN_CHUNK:] * decay).astype(BF16))
        egs.append(jnp.exp(gc))

    t_mats = [jnp.where(eye_cat, 1.0, 0.0) - jnp.where(blk_base, a, 0.0) for a in a_mats]
    mm1 = lambda x, y: dot(x.astype(BF16), bd(y))
    for off_mask in off_masks:
        z = [mm1(t, jnp.where(off_mask, a, 0.0)) for t, a in zip(t_mats, a_mats)]
        t_mats = [t - mm1(zi, t) for t, zi in zip(t_mats, z)]

    out = []
    for (qn, kn, v, beta, gc, incl, strict, last), t, attn, eg in zip(chunks, t_mats, attns, egs):
        tb = t.astype(BF16)
        glast = gc[last:last + 1, :]
        w_v = dot(tb, bd(v * beta))
        w_k = dot(tb, bd(kn * beta * eg))
        lhs = jnp.concatenate([w_k, qn * eg], axis=0).astype(BF16)
        k_dec = (kn * jnp.exp(glast - gc)).astype(BF16)
        out.append((w_v, lhs, attn, k_dec, jnp.exp(glast)))
    return out


def _dn_state_step(prep, s_bd, blockmask):
    w_v, lhs, attn, k_dec, e_last = prep
    dot = functools.partial(jnp.dot, preferred_element_type=F32)
    ps = dot(lhs, s_bd.astype(BF16))
    v_new = w_v - ps[:DN_CHUNK]
    o = ps[DN_CHUNK:] + dot(attn, _dn_bd(v_new, blockmask))
    upd = lax.dot_general(k_dec, v_new.astype(BF16), (((0,), (0,)), ((), ())), preferred_element_type=F32)
    return o, s_bd * e_last + jnp.where(blockmask, upd, 0.0)


def _dn_chunks_kernel(*refs, n_chunk, has_init):
    qf_ref, kf_ref, vf_ref, bf_ref, gf_ref, qb_ref, kb_ref, vb_ref, bb_ref, gb_ref = refs[:10]
    if has_init:
        s0_ref, of_ref, ob_ref, s_ref = refs[10:]
    else:
        of_ref, ob_ref, sout_ref, s_ref = refs[10:]

    @pl.when(pl.program_id(1) == 0)
    def _():
        s_ref[...] = s0_ref[...] if has_init else jnp.zeros_like(s_ref)

    row = lax.broadcasted_iota(jnp.int32, (DN_WIDTH, DN_WIDTH), 0)
    col = lax.broadcasted_iota(jnp.int32, (DN_WIDTH, DN_WIDTH), 1)
    blockmask = (row // DN_DV) == (col // DN_DV)
    i = lax.broadcasted_iota(jnp.int32, (DN_CHUNK, DN_WIDTH), 0)
    j = lax.broadcasted_iota(jnp.int32, (DN_CHUNK, DN_WIDTH), 1) % DN_DV
    eye_cat = i == j
    blk_base = (i // DN_BASE) == (j // DN_BASE)
    off_masks = tuple(((i // (2 * s)) == (j // (2 * s))) & ((i // s) != (j // s))
                      for s in DN_MERGES)
    chunks, where = [], []
    for c in range(n_chunk):
        rf = slice(c * DN_CHUNK, (c + 1) * DN_CHUNK)
        chunks.append((qf_ref[rf, :], kf_ref[rf, :], vf_ref[rf, :], bf_ref[rf, :], gf_ref[rf, :],
                       i >= j, i > j, DN_CHUNK - 1))
        where.append((0, of_ref, rf))
        cb = n_chunk - 1 - c
        rb = slice(cb * DN_CHUNK, (cb + 1) * DN_CHUNK)
        chunks.append((qb_ref[rb, :], kb_ref[rb, :], vb_ref[rb, :], bb_ref[rb, :], gb_ref[rb, :],
                       i <= j, i < j, 0))
        where.append((1, ob_ref, rb))
    prepared = _dn_prepare(chunks, blockmask, eye_cat, blk_base, off_masks)

    states = [s_ref[0], s_ref[1]]
    for prep, (d, o_ref, rows) in zip(prepared, where):
        o, states[d] = _dn_state_step(prep, states[d], blockmask)
        o_ref[rows, :] = o
    s_ref[0] = states[0]
    s_ref[1] = states[1]
    if not has_init:
        sout_ref[0] = states[0]
        sout_ref[1] = states[1]


def _dn_chunks(qn, kn, v, beta_f, beta_b, gc_f, gc_b):
    state_shape = jax.ShapeDtypeStruct((BATCH, 2, DN_WIDTH, DN_WIDTH), F32)
    state_spec = pl.BlockSpec((None, 2, DN_WIDTH, DN_WIDTH), lambda b, t: (b, 0, 0, 0))
    scratch = [pltpu.VMEM((2, DN_WIDTH, DN_WIDTH), F32)]
    args = (qn, kn, v, beta_f, gc_f, qn, kn, v, beta_b, gc_b)

    ctx_in = pl.BlockSpec((CTX_LEN, DN_WIDTH), lambda b, t: (ROWS_X // CTX_LEN + b, 0))
    ctx_out = pl.BlockSpec((CTX_LEN, DN_WIDTH), lambda b, t: (b, 0))
    o_fc, o_bc, states = pl.pallas_call(
        functools.partial(_dn_chunks_kernel, n_chunk=CTX_LEN // DN_CHUNK, has_init=False),
        out_shape=[jax.ShapeDtypeStruct((ROWS_C, DN_WIDTH), F32)] * 2 + [state_shape],
        grid=(BATCH, 1),
        in_specs=[ctx_in] * 10,
        out_specs=[ctx_out, ctx_out, state_spec],
        scratch_shapes=scratch,
        compiler_params=_cparams(("arbitrary", "arbitrary")),
        name="dn_chunks_ctx",
    )(*args)

    n_x = SEQ // DN_XTILE
    spec_f = pl.BlockSpec((DN_XTILE, DN_WIDTH), lambda b, t: (b * n_x + t, 0))
    spec_b = pl.BlockSpec((DN_XTILE, DN_WIDTH), lambda b, t: (b * n_x + n_x - 1 - t, 0))
    o_fx, o_bx = pl.pallas_call(
        functools.partial(_dn_chunks_kernel, n_chunk=DN_XTILE // DN_CHUNK, has_init=True),
        out_shape=[jax.ShapeDtypeStruct((ROWS_X, DN_WIDTH), F32)] * 2,
        grid=(BATCH, n_x),
        in_specs=[spec_f] * 5 + [spec_b] * 5 + [state_spec],
        out_specs=[spec_f, spec_b],
        scratch_shapes=scratch,
        compiler_params=_cparams(("arbitrary", "arbitrary")),
        name="dn_chunks",
    )(*args, states)
    return o_fx, o_bx, o_fc, o_bc


def _dn_out_kernel(of_ref, ob_ref, gt_ref, nw_ref, hsum_ref, o_ref):
    o = of_ref[...] + ob_ref[...]
    ms = _exact_rows_dot(o * o, hsum_ref[...]) * (1.0 / DN_DV)
    gt = gt_ref[...]
    o_ref[...] = (o * lax.rsqrt(ms + EPS) * nw_ref[...] * (gt * jax.nn.sigmoid(gt))).astype(BF16)


def _dn_out(o_fx, o_bx, o_fc, o_bc, gt, norm_w):
    head_of_lane = jnp.arange(DN_WIDTH) // DN_DV
    hsum = (head_of_lane[:, None] == head_of_lane[None, :]).astype(BF16)
    nw = jnp.tile(norm_w, DN_HEADS)[None]
    consts = [pl.BlockSpec((1, DN_WIDTH), lambda i: (0, 0)), pl.BlockSpec((DN_WIDTH, DN_WIDTH), lambda i: (0, 0))]
    row = pl.BlockSpec((TG, DN_WIDTH), lambda i: (i, 0))

    def call(o_f, o_b, gt_spec, n_rows, name):
        return pl.pallas_call(
            _dn_out_kernel,
            out_shape=jax.ShapeDtypeStruct((n_rows, DN_WIDTH), BF16),
            grid=(n_rows // TG,),
            in_specs=[row, row, gt_spec] + consts,
            out_specs=row,
            compiler_params=_cparams(("arbitrary",)),
            name=name,
        )(o_f, o_b, gt, nw, hsum)

    dn_x = call(o_fx, o_bx, row, ROWS_X, "dn_out")
    dn_c = call(o_fc, o_bc, pl.BlockSpec((TG, DN_WIDTH), lambda i: (ROWS_X // TG + i, 0)), ROWS_C, "dn_out_ctx")
    return dn_x, dn_c


def _dn_mixer(qkv, gt, ba, conv_w, a_log, dt_bias, norm_w):
    qn, kn, v, beta_f, beta_b, gc_f, gc_b = _dn_prep(qkv, ba, conv_w, a_log, dt_bias)
    return _dn_out(*_dn_chunks(qn, kn, v, beta_f, beta_b, gc_f, gc_b), gt, norm_w)


def _pad_w_in(w):
    d = w.shape[0]
    s = [0]
    for n in IN_SIZES:
        s.append(s[-1] + n)
    u, cq, ckv, kr, qkv, gt, ba = (w[:, s[i]:s[i + 1]] for i in range(7))
    z = lambda n: jnp.zeros((d, n), w.dtype)
    return jnp.concatenate(
        [u, cq, ckv, z(KR_LANE), kr, z(LANES - KR_LANE - MLA_ROPE), qkv, gt, ba, z(LANES - 4 * DN_HEADS)], axis=1)


def _head_slots(w, width):
    kdim = w.shape[0]
    w = w.reshape(kdim, MLA_HEADS, width)
    return jnp.pad(w, ((0, 0), (0, 0), (0, HEAD_SLOT - width))).reshape(kdim, MLA_HEADS * HEAD_SLOT)


def _rope_swap(w):
    lane = jnp.arange(HEAD_SLOT)
    half = MLA_ROPE // 2
    src = jnp.where(lane < MLA_NOPE + half, lane + half, lane - half)
    in_rope = (lane >= MLA_NOPE) & (lane < MLA_DH)
    w3 = w.reshape(w.shape[0], -1, HEAD_SLOT)
    return jnp.where(in_rope, w3[:, :, jnp.clip(src, 0, HEAD_SLOT - 1)], jnp.zeros((), w.dtype)).reshape(w.shape)


def _rope_tables():
    rows = SEQ // GRID_W
    row = jnp.repeat(jnp.arange(rows, dtype=F32), GRID_W)
    col = jnp.tile(jnp.arange(GRID_W, dtype=F32), rows)
    inv = ROPE_BASE ** (-jnp.arange(ROPE_FREQS, dtype=F32) / ROPE_FREQS)
    ang = jnp.concatenate([row[:, None] * inv, col[:, None] * inv], axis=-1)
    cos, sin = jnp.cos(ang), jnp.sin(ang)
    ones = lambda n: jnp.ones((SEQ, n), F32)
    zeros = lambda n: jnp.zeros((SEQ, n), F32)
    cos_t = jnp.concatenate([ones(MLA_NOPE), cos, cos, ones(HEAD_SLOT - MLA_DH)], axis=1)
    sin_t = jnp.concatenate([zeros(MLA_NOPE), -sin, sin, zeros(HEAD_SLOT - MLA_DH)], axis=1)
    cos_t = jnp.concatenate([jnp.tile(cos_t, (BATCH, 1)), jnp.ones((ROWS_C, HEAD_SLOT), F32)], axis=0)
    sin_t = jnp.concatenate([jnp.tile(sin_t, (BATCH, 1)), jnp.zeros((ROWS_C, HEAD_SLOT), F32)], axis=0)
    return cos_t, sin_t


def kernel(x, c, ctx, c_ctx, w_ada, b_ada, norm1_w, w_in, s5_lambda_re, s5_lambda_im, s5_log_dt, s5_b_re, s5_b_im, s5_c_re, s5_c_im, s5_d, s5_glu_w, mla_q_norm_w, mla_kv_norm_w, mla_w_uq, mla_w_ukv, mla_q_gain, mla_k_gain, dn_conv_w, dn_a_log, dn_dt_bias, dn_norm_w, w_out, norm2_w, w_ff1, w_ff2):
    h = jnp.concatenate([x.reshape(ROWS_X, D_MODEL), ctx.reshape(ROWS_C, D_MODEL)], axis=0)
    c_pad = jnp.concatenate([c, c_ctx[None], jnp.zeros((3, D_MODEL), F32)], axis=0)
    mods_all = _ada_all(c_pad, w_ada, b_ada)
    cos_t, sin_t = _rope_tables()

    for l in range(DEPTH):
        ctx_out = l < DEPTH - 1
        mods = mods_all[l, :BATCH + 1].reshape(BATCH + 1, 6, 1, D_MODEL)
        wq = _head_slots(mla_w_uq[l], MLA_DH)
        w_ukv = mla_w_ukv[l].reshape(MLA_KV_RANK, MLA_HEADS, MLA_NOPE + MLA_V)
        wk = _head_slots(w_ukv[:, :, :MLA_NOPE].reshape(MLA_KV_RANK, MLA_HEADS * MLA_NOPE), MLA_NOPE).astype(BF16)
        wvt = w_ukv[:, :, MLA_NOPE:].reshape(MLA_KV_RANK, MLA_WIDTH).T.astype(BF16)
        qg, kg = (jnp.pad(g, (0, HEAD_SLOT - MLA_DH))[None] for g in (mla_q_gain[l], mla_k_gain[l]))
        mla_params = (mla_q_norm_w[l][None], mla_kv_norm_w[l][None], wq.astype(BF16), _rope_swap(wq).astype(BF16),
                      wk, wvt, qg, _rope_swap(qg), kg, _rope_swap(kg), _rope_swap(jnp.eye(HEAD_SLOT, dtype=BF16)))
        u, qkv, gt, ba, q, k4, vt4 = _in_proj(h, norm1_w[l][None], mods, _pad_w_in(w_in[l]).astype(BF16),
                                              cos_t, sin_t, mla_params)

        s5 = _s5_mixer(u, s5_lambda_re[l], s5_lambda_im[l], s5_log_dt[l], s5_b_re[l], s5_b_im[l],
                       s5_c_re[l], s5_c_im[l], s5_d[l], s5_glu_w[l])
        ot = _flash(q, k4, vt4, ctx_out)
        dn = _dn_mixer(qkv, gt, ba, dn_conv_w[l], dn_a_log[l], dn_dt_bias[l], dn_norm_w[l])

        wo = w_out[l].astype(BF16)
        n_rows = ROWS if ctx_out else ROWS_X
        h = _out_mlp(h, s5, ot, dn, wo[:S5_WIDTH], wo[S5_WIDTH:S5_WIDTH + MLA_WIDTH], wo[S5_WIDTH + MLA_WIDTH:],
                     mods, norm2_w[l][None], w_ff1[l].astype(BF16), w_ff2[l].astype(BF16), n_rows)
    return h.reshape(BATCH, SEQ, D_MODEL)
```

```python
import functools

import jax
import jax.numpy as jnp
from jax import lax
from jax.experimental import pallas as pl
from jax.experimental.pallas import tpu as pltpu

F32 = jnp.float32
BF16 = jnp.bfloat16

D_MODEL = 1024
BATCH = 4
SEQ = 4096
DEPTH = 4
GRID_W = 64
CTX_LEN = 256
EPS = 1e-6

S5_GROUP = 16
S5_GROUPS = 16
S5_WIDTH = 256
S5_STATE = 64

MLA_HEADS = 8
MLA_NOPE = 64
MLA_ROPE = 32
MLA_V = 64
MLA_VE = MLA_V + 16
MLA_DH = 96
MLA_Q_RANK = 256
MLA_KV_RANK = 128
MLA_WIDTH = 512
ROPE_BASE = 10000.0
ROPE_FREQS = 8

DN_HEADS = 4
DN_DK = 64
DN_DV = 64
DN_QKV = 768
DN_WIDTH = 256
DN_CONV = 5
DN_CHUNK = 64

D_FF = 4 * D_MODEL
IN_SIZES = (S5_WIDTH, MLA_Q_RANK, MLA_KV_RANK, MLA_ROPE, DN_QKV, DN_WIDTH, 4 * DN_HEADS)

ROWS_X = BATCH * SEQ
ROWS_C = BATCH * CTX_LEN
ROWS = ROWS_X + ROWS_C
LANES = 128
HEAD_SLOT = LANES
KEY_BLK = 256
N_XBLK = SEQ // KEY_BLK
CTX_BLK0 = ROWS_X // KEY_BLK

IN_PAD = 1920
COL_U, COL_CQ, COL_CKV, COL_KR, COL_QKV, COL_GT, COL_BA = 0, 256, 512, 640, 768, 1536, 1792
KR_LANE = MLA_NOPE

TM = 512
TG = 1024
TQ = 4096
FLASH_KB = 2
ADA_TN = 1536
VMEM_LIMIT = 56 * 1024 * 1024

LOG2E = 1.4426950408889634


def _cparams(sem):
    return pltpu.CompilerParams(dimension_semantics=sem, vmem_limit_bytes=VMEM_LIMIT)


def _split3_dot(a, w):
    a_hi = a.astype(BF16)
    a_lo = (a - a_hi.astype(F32)).astype(BF16)
    w_hi = w.astype(BF16)
    w_lo = (w - w_hi.astype(F32)).astype(BF16)
    out = jnp.dot(a_hi, w_hi, preferred_element_type=F32)
    out += jnp.dot(a_lo, w_hi, preferred_element_type=F32)
    out += jnp.dot(a_hi, w_lo, preferred_element_type=F32)
    return out


def _rms_rows(x, w):
    return x * lax.rsqrt(jnp.mean(x * x, axis=-1, keepdims=True) + EPS) * w


def _ada_kernel(c_ref, w_ref, b_ref, o_ref):
    c = c_ref[...]
    s = c * jax.nn.sigmoid(c)
    o_ref[...] = _split3_dot(s, w_ref[...]) + b_ref[...]


def _ada_all(c_pad, w_ada, b_ada):
    tn = ADA_TN
    return pl.pallas_call(
        _ada_kernel,
        out_shape=jax.ShapeDtypeStruct((DEPTH, 8, 6 * D_MODEL), F32),
        grid=(DEPTH, 6 * D_MODEL // tn),
        in_specs=[
            pl.BlockSpec((8, D_MODEL), lambda l, j: (0, 0)),
            pl.BlockSpec((None, D_MODEL, tn), lambda l, j: (l, 0, j)),
            pl.BlockSpec((None, 1, tn), lambda l, j: (l, 0, j)),
        ],
        out_specs=pl.BlockSpec((None, 8, tn), lambda l, j: (l, 0, j)),
        compiler_params=_cparams(("arbitrary", "arbitrary")),
        name="ada_mod",
    )(c_pad, w_ada, b_ada.reshape(DEPTH, 1, 6 * D_MODEL))


def _mod_index(i):
    return jnp.minimum(i // (SEQ // TM), BATCH)


def _mla_heads(m, cos, sin, qnw, kvnw, wq, wq_sw, wk, wvt, qg, qg_sw, kg, kg_sw, perm, q_ref, k_ref, vt_ref):
    cqn = _rms_rows(m[:, 0:MLA_Q_RANK], qnw).astype(BF16)
    ckvn = _rms_rows(m[:, MLA_Q_RANK:MLA_Q_RANK + MLA_KV_RANK], kvnw).astype(BF16)
    kr = m[:, MLA_Q_RANK + MLA_KV_RANK:]
    qf = jnp.dot(cqn, wq, preferred_element_type=F32)
    qf_sw = jnp.dot(cqn, wq_sw, preferred_element_type=F32)
    kf = jnp.dot(ckvn, wk, preferred_element_type=F32)
    vt = lax.dot_general(wvt, ckvn, (((1,), (1,)), ((), ())), preferred_element_type=F32)
    kr_sw = _exact_rows_dot(kr, perm)
    q_cos, q_sin = cos * qg, sin * qg_sw
    k_cos, k_sin = cos * kg, sin * kg_sw

    def head_norm_rope(xh, xh_sw, g_cos, g_sin, scale):
        ss = jnp.sum(xh * xh, axis=-1, keepdims=True) * (1.0 / MLA_DH)
        return (xh * g_cos + xh_sw * g_sin) * (lax.rsqrt(ss + EPS) * scale)

    n_sub = m.shape[0] // KEY_BLK
    ones = jnp.ones((MLA_VE - MLA_V, m.shape[0]), BF16)
    for h in range(MLA_HEADS):
        sl = slice(h * HEAD_SLOT, (h + 1) * HEAD_SLOT)
        q_ref[h] = head_norm_rope(qf[:, sl], qf_sw[:, sl], q_cos, q_sin, MLA_DH ** -0.5 * LOG2E).astype(BF16)
        kh = head_norm_rope(kf[:, sl] + kr, kr_sw, k_cos, k_sin, 1.0).astype(BF16)
        vh = jnp.concatenate([vt[h * MLA_V:(h + 1) * MLA_V, :].astype(BF16), ones], axis=0)
        for s in range(n_sub):
            k_ref[h, s] = kh[s * KEY_BLK:(s + 1) * KEY_BLK, :]
            vt_ref[h, s] = vh[:, s * KEY_BLK:(s + 1) * KEY_BLK]


def _in_proj_kernel(h_ref, nw_ref, sh_ref, sc_ref, w_ref, cos_ref, sin_ref, qnw_ref, kvnw_ref, wq_ref, wqs_ref,
                    wk_ref, wvt_ref, qg_ref, qgs_ref, kg_ref, kgs_ref, perm_ref,
                    u_ref, qkv_ref, gt_ref, ba_ref, q_ref, k_ref, vt_ref):
    y = _rms_rows(h_ref[...], nw_ref[...]) * (1.0 + sc_ref[...]) + sh_ref[...]
    z = jnp.dot(y.astype(BF16), w_ref[...], preferred_element_type=F32)
    u_ref[...] = z[:, COL_U:COL_CQ]
    qkv_ref[...] = z[:, COL_QKV:COL_GT]
    gt_ref[...] = z[:, COL_GT:COL_BA]
    ba_ref[...] = z[:, COL_BA:IN_PAD]
    _mla_heads(z[:, COL_CQ:COL_QKV], cos_ref[...], sin_ref[...], qnw_ref[...], kvnw_ref[...], wq_ref[...],
               wqs_ref[...], wk_ref[...], wvt_ref[...], qg_ref[...], qgs_ref[...], kg_ref[...], kgs_ref[...],
               perm_ref[...], q_ref, k_ref, vt_ref)


def _in_proj(h, nw, mods, w_pad, cos_t, sin_t, mla_params):
    n_sub = TM // KEY_BLK
    mod_spec = lambda j: pl.BlockSpec((None, None, 1, D_MODEL), lambda i: (_mod_index(i), j, 0, 0))
    full = lambda a: pl.BlockSpec(a.shape, lambda i: (0,) * a.ndim)
    row = lambda n: pl.BlockSpec((TM, n), lambda i: (i, 0))
    row_widths = (COL_CQ - COL_U, COL_GT - COL_QKV, COL_BA - COL_GT, IN_PAD - COL_BA)
    return pl.pallas_call(
        _in_proj_kernel,
        out_shape=[jax.ShapeDtypeStruct((ROWS, n), F32) for n in row_widths] + [
            jax.ShapeDtypeStruct((MLA_HEADS, ROWS, HEAD_SLOT), BF16),
            jax.ShapeDtypeStruct((MLA_HEADS, ROWS // KEY_BLK, KEY_BLK, HEAD_SLOT), BF16),
            jax.ShapeDtypeStruct((MLA_HEADS, ROWS // KEY_BLK, MLA_VE, KEY_BLK), BF16),
        ],
        grid=(ROWS // TM,),
        in_specs=[row(D_MODEL), full(nw), mod_spec(0), mod_spec(1), full(w_pad), row(HEAD_SLOT), row(HEAD_SLOT)]
        + [full(p) for p in mla_params],
        out_specs=[row(n) for n in row_widths] + [
            pl.BlockSpec((MLA_HEADS, TM, HEAD_SLOT), lambda i: (0, i, 0)),
            pl.BlockSpec((MLA_HEADS, n_sub, KEY_BLK, HEAD_SLOT), lambda i: (0, i, 0, 0)),
            pl.BlockSpec((MLA_HEADS, n_sub, MLA_VE, KEY_BLK), lambda i: (0, i, 0, 0)),
        ],
        compiler_params=_cparams(("arbitrary",)),
        name="in_proj",
    )(h, nw, mods, mods, w_pad, cos_t, sin_t, *mla_params)


def _scores(k, q):
    return lax.dot_general(k, q, (((1,), (1,)), ((), ())), preferred_element_type=F32)


def _softmax_update(carry, s, vts):
    m, acc = carry
    m_new = jnp.maximum(m, jnp.max(s, axis=0, keepdims=True))
    pb = jnp.exp2(s - m_new).astype(BF16)
    acc = jnp.exp2(m - m_new) * acc
    for t, vt in enumerate(vts):
        acc += jnp.dot(vt, pb[t * KEY_BLK:(t + 1) * KEY_BLK], preferred_element_type=F32)
    return m_new, acc


def _softmax_attend(q, key_blocks):
    tq = q.shape[0]
    carry = (jnp.full((1, tq), -jnp.inf, F32), jnp.zeros((MLA_VE, tq), F32))
    s_next = _scores(key_blocks[0][0], q)
    for n, (_, vts) in enumerate(key_blocks):
        s = s_next
        if n + 1 < len(key_blocks):
            s_next = _scores(key_blocks[n + 1][0], q)
        carry = _softmax_update(carry, s, vts)
    _, acc = carry
    return (acc[:MLA_V] * (1.0 / acc[MLA_V:MLA_V + 1])).astype(BF16)


def _flash_x_kernel(q_ref, kx_ref, kc_ref, vtx_ref, vtc_ref, o_ref):
    blocks = [(kc_ref[0], [vtc_ref[0]])]
    for j in range(N_XBLK // FLASH_KB):
        k = kx_ref[j * FLASH_KB:(j + 1) * FLASH_KB].reshape(FLASH_KB * KEY_BLK, HEAD_SLOT)
        blocks.append((k, [vtx_ref[j * FLASH_KB + t] for t in range(FLASH_KB)]))
    o_ref[...] = _softmax_attend(q_ref[...], blocks)


def _flash_c_kernel(q_ref, kc_ref, vtc_ref, o_ref):
    for h in range(MLA_HEADS):
        o_ref[h * MLA_V:(h + 1) * MLA_V, :] = _softmax_attend(q_ref[h], [(kc_ref[h, 0], [vtc_ref[h, 0]])])


def _flash(q, k4, vt4, ctx_out):
    n_qx = SEQ // TQ
    ot_x = pl.pallas_call(
        _flash_x_kernel,
        out_shape=jax.ShapeDtypeStruct((MLA_WIDTH, ROWS_X), BF16),
        grid=(BATCH, MLA_HEADS, n_qx),
        in_specs=[
            pl.BlockSpec((None, TQ, HEAD_SLOT), lambda b, h, qi: (h, b * n_qx + qi, 0)),
            pl.BlockSpec((None, N_XBLK, KEY_BLK, HEAD_SLOT), lambda b, h, qi: (h, b, 0, 0)),
            pl.BlockSpec((None, 1, KEY_BLK, HEAD_SLOT), lambda b, h, qi: (h, CTX_BLK0 + b, 0, 0)),
            pl.BlockSpec((None, N_XBLK, MLA_VE, KEY_BLK), lambda b, h, qi: (h, b, 0, 0)),
            pl.BlockSpec((None, 1, MLA_VE, KEY_BLK), lambda b, h, qi: (h, CTX_BLK0 + b, 0, 0)),
        ],
        out_specs=pl.BlockSpec((MLA_V, TQ), lambda b, h, qi: (h, b * n_qx + qi)),
        compiler_params=_cparams(("arbitrary", "arbitrary", "arbitrary")),
        name="mla_flash_x",
    )(q, k4, k4, vt4, vt4)
    if not ctx_out:
        return ot_x, None
    ot_c = pl.pallas_call(
        _flash_c_kernel,
        out_shape=jax.ShapeDtypeStruct((MLA_WIDTH, ROWS_C), BF16),
        grid=(BATCH,),
        in_specs=[
            pl.BlockSpec((MLA_HEADS, CTX_LEN, HEAD_SLOT), lambda b: (0, CTX_BLK0 + b, 0)),
            pl.BlockSpec((MLA_HEADS, 1, KEY_BLK, HEAD_SLOT), lambda b: (0, CTX_BLK0 + b, 0, 0)),
            pl.BlockSpec((MLA_HEADS, 1, MLA_VE, KEY_BLK), lambda b: (0, CTX_BLK0 + b, 0, 0)),
        ],
        out_specs=pl.BlockSpec((MLA_WIDTH, CTX_LEN), lambda b: (0, b)),
        compiler_params=_cparams(("arbitrary",)),
        name="mla_flash_c",
    )(q, k4, vt4)
    return ot_x, ot_c


def _out_mlp_kernel(h_ref, s5x_ref, s5c_ref, otx_ref, otc_ref, dnx_ref, dnc_ref, wo_s5_ref, wo_mla_ref, wo_dn_ref,
                    g2_ref, sh_ref, sc_ref, g5_ref, nw_ref, w1_ref, w2_ref, o_ref):
    is_ctx = pl.program_id(0) >= ROWS_X // TM
    s5 = jnp.where(is_ctx, s5c_ref[...], s5x_ref[...])
    ot = jnp.where(is_ctx, otc_ref[...], otx_ref[...])
    dn = jnp.where(is_ctx, dnc_ref[...], dnx_ref[...])
    mix = jnp.dot(s5, wo_s5_ref[...], preferred_element_type=F32)
    mix += lax.dot_general(ot, wo_mla_ref[...], (((0,), (0,)), ((), ())), preferred_element_type=F32)
    mix += jnp.dot(dn, wo_dn_ref[...], preferred_element_type=F32)
    x1 = h_ref[...] + g2_ref[...] * mix
    xn = (_rms_rows(x1, nw_ref[...]) * (1.0 + sc_ref[...]) + sh_ref[...]).astype(BF16)
    hid = jnp.maximum(jnp.dot(xn, w1_ref[...], preferred_element_type=F32), 0.0)
    o_ref[...] = x1 + g5_ref[...] * jnp.dot((hid * hid).astype(BF16), w2_ref[...], preferred_element_type=F32)


def _out_mlp(h, s5, ot, dn, wo_s5, wo_mla, wo_dn, mods, nw, w1, w2, n_rows):
    n_xtiles = ROWS_X // TM
    (s5_x, s5_c), (ot_x, ot_c), (dn_x, dn_c) = ((a, a if n_rows == ROWS_X else c) for a, c in (s5, ot, dn))
    x_rows = lambda n: pl.BlockSpec((TM, n), lambda i: (jnp.minimum(i, n_xtiles - 1), 0))
    c_rows = lambda n: pl.BlockSpec((TM, n), lambda i: (jnp.maximum(i - n_xtiles, 0), 0))
    mod_spec = lambda j: pl.BlockSpec((None, None, 1, D_MODEL), lambda i: (_mod_index(i), j, 0, 0))
    full = lambda a: pl.BlockSpec(a.shape, lambda i: (0,) * a.ndim)
    row = lambda n: pl.BlockSpec((TM, n), lambda i: (i, 0))
    return pl.pallas_call(
        _out_mlp_kernel,
        out_shape=jax.ShapeDtypeStruct((n_rows, D_MODEL), F32),
        grid=(n_rows // TM,),
        in_specs=[
            row(D_MODEL), x_rows(S5_WIDTH), c_rows(S5_WIDTH),
            pl.BlockSpec((MLA_WIDTH, TM), lambda i: (0, jnp.minimum(i, n_xtiles - 1))),
            pl.BlockSpec((MLA_WIDTH, TM), lambda i: (0, jnp.maximum(i - n_xtiles, 0))),
            x_rows(DN_WIDTH), c_rows(DN_WIDTH),
            full(wo_s5), full(wo_mla), full(wo_dn),
            mod_spec(2), mod_spec(3), mod_spec(4), mod_spec(5),
            full(nw), full(w1), full(w2),
        ],
        out_specs=row(D_MODEL),
        compiler_params=_cparams(("arbitrary",)),
        name="out_mlp",
    )(h, s5_x, s5_c, ot_x, ot_c, dn_x, dn_c, wo_s5, wo_mla, wo_dn, mods, mods, mods, mods, nw, w1, w2)


S5_NSTATE = 2 * S5_GROUPS * S5_STATE
S5_SLABS = S5_NSTATE // LANES
S5_CHAINS = 2 * BATCH
S5_STEPS = CTX_LEN + SEQ
S5_TB = 128
S5_PITCH = S5_TB + 4


def _s5_kernel(*refs):
    u_refs = refs[:S5_CHAINS]
    flip_ref, bmat_ref, cmat_ref, are_ref, aim_ref, yf_ref, yb_ref, st_ref, hs_ref, h_ref = refs[S5_CHAINS:]

    @pl.when(pl.program_id(0) == 0)
    def _():
        h_ref[...] = jnp.zeros_like(h_ref)

    half = S5_SLABS // 2
    flip = flip_ref[...]
    for d in range(2):
        blocks = [u_refs[d * BATCH + b][...].astype(BF16) for b in range(BATCH)]
        if d == 1:
            blocks = [jnp.dot(flip, blk, preferred_element_type=F32).astype(BF16) for blk in blocks]
        bu = jnp.dot(jnp.concatenate(blocks, axis=0), bmat_ref[d], preferred_element_type=F32)
        for b in range(BATCH):
            row0 = (d * BATCH + b) * S5_PITCH
            for j in range(S5_SLABS):
                st_ref[j, row0:row0 + S5_TB, :] = bu[b * S5_TB:(b + 1) * S5_TB, j * LANES:(j + 1) * LANES]

    a_re = [are_ref[j] for j in range(half)]
    a_im = [aim_ref[j] for j in range(half)]

    def step(s, h):
        new = []
        rows = pl.ds(s, S5_CHAINS, stride=S5_PITCH)
        for j in range(half):
            hr, hi = h[j], h[half + j]
            nr = a_re[j] * hr - a_im[j] * hi + st_ref[j, rows, :]
            ni = a_re[j] * hi + a_im[j] * hr + st_ref[half + j, rows, :]
            hs_ref[j, rows, :] = nr
            hs_ref[half + j, rows, :] = ni
            new.append((nr, ni))
        return tuple(n[0] for n in new) + tuple(n[1] for n in new)

    h = lax.fori_loop(0, S5_TB, step, tuple(h_ref[j] for j in range(S5_SLABS)), unroll=4)
    for j in range(S5_SLABS):
        h_ref[j] = h[j]

    for d, y_ref in ((0, yf_ref), (1, yb_ref)):
        parts = []
        for b in range(BATCH):
            row0 = (d * BATCH + b) * S5_PITCH
            parts.append(jnp.concatenate([hs_ref[j, row0:row0 + S5_TB, :] for j in range(S5_SLABS)], axis=1))
        hm = jnp.concatenate(parts, axis=0).astype(BF16)
        y = jnp.dot(hm, cmat_ref[d], preferred_element_type=F32)
        for b in range(BATCH):
            yb = y[b * S5_TB:(b + 1) * S5_TB]
            y_ref[b] = _exact_cols_dot(flip, yb) if d == 1 else yb


def _s5_step_block(i, b, reverse):
    n_ctx = CTX_LEN // S5_TB
    n_x = SEQ // S5_TB
    ctx0 = ROWS_X // S5_TB + b * n_ctx
    if reverse:
        return jnp.where(i < n_ctx, ctx0 + n_ctx - 1 - i, b * n_x + n_x - 1 - (i - n_ctx))
    return jnp.where(i < n_ctx, ctx0 + i, b * n_x + (i - n_ctx))


def _s5_time_block(i, reverse):
    n_ctx = CTX_LEN // S5_TB
    n_x = SEQ // S5_TB
    if reverse:
        return jnp.where(i < n_ctx, n_x + n_ctx - 1 - i, n_x - 1 - (i - n_ctx))
    return jnp.where(i < n_ctx, n_x + i, i - n_ctx)


def _s5_scan_chains(u, bmat, cmat, a_re, a_im):
    full = lambda a: pl.BlockSpec(a.shape, lambda i: (0,) * a.ndim)
    flip = jnp.eye(S5_TB, dtype=BF16)[::-1]
    u_specs = [pl.BlockSpec((S5_TB, S5_WIDTH), functools.partial(lambda i, b, rev: (_s5_step_block(i, b, rev), 0),
                                                                b=c % BATCH, rev=c >= BATCH))
               for c in range(S5_CHAINS)]
    y_shape = jax.ShapeDtypeStruct((BATCH, S5_STEPS, S5_WIDTH), F32)
    return pl.pallas_call(
        _s5_kernel,
        out_shape=[y_shape, y_shape],
        grid=(S5_STEPS // S5_TB,),
        in_specs=u_specs + [full(flip), full(bmat), full(cmat), full(a_re), full(a_im)],
        out_specs=[pl.BlockSpec((BATCH, S5_TB, S5_WIDTH), lambda i: (0, _s5_time_block(i, False), 0)),
                   pl.BlockSpec((BATCH, S5_TB, S5_WIDTH), lambda i: (0, _s5_time_block(i, True), 0))],
        scratch_shapes=[pltpu.VMEM((S5_SLABS, S5_CHAINS * S5_PITCH, LANES), F32),
                        pltpu.VMEM((S5_SLABS, S5_CHAINS * S5_PITCH, LANES), F32),
                        pltpu.VMEM((S5_SLABS, S5_CHAINS, LANES), F32)],
        compiler_params=_cparams(("arbitrary",)),
        name="s5_scan",
    )(*([u] * S5_CHAINS), flip, bmat, cmat, a_re, a_im)


def _s5_glu_kernel(u_ref, yf_ref, yb_ref, d_ref, w_ref, o_ref):
    y = d_ref[...] * u_ref[...] + yf_ref[...] + yb_ref[...]
    z = jnp.dot(jax.nn.gelu(y).astype(BF16), w_ref[...], preferred_element_type=F32)
    o_ref[...] = (z[:, :S5_WIDTH] * jax.nn.sigmoid(z[:, S5_WIDTH:])).astype(BF16)


def _s5_glu(u, yf, yb, d_skip, glu_w):
    tiles_b = SEQ // TG
    consts = [pl.BlockSpec((1, S5_WIDTH), lambda i: (0, 0)), pl.BlockSpec((S5_WIDTH, 2 * S5_WIDTH), lambda i: (0, 0))]
    y_x = pl.BlockSpec((None, TG, S5_WIDTH), lambda i: (i // tiles_b, i % tiles_b, 0))
    s5_x = pl.pallas_call(
        _s5_glu_kernel,
        out_shape=jax.ShapeDtypeStruct((ROWS_X, S5_WIDTH), BF16),
        grid=(ROWS_X // TG,),
        in_specs=[pl.BlockSpec((TG, S5_WIDTH), lambda i: (i, 0)), y_x, y_x] + consts,
        out_specs=pl.BlockSpec((TG, S5_WIDTH), lambda i: (i, 0)),
        compiler_params=_cparams(("arbitrary",)),
        name="s5_glu",
    )(u, yf, yb, d_skip, glu_w)
    y_c = pl.BlockSpec((None, CTX_LEN, S5_WIDTH), lambda b: (b, SEQ // CTX_LEN, 0))
    s5_c = pl.pallas_call(
        _s5_glu_kernel,
        out_shape=jax.ShapeDtypeStruct((ROWS_C, S5_WIDTH), BF16),
        grid=(BATCH,),
        in_specs=[pl.BlockSpec((CTX_LEN, S5_WIDTH), lambda b: (ROWS_X // CTX_LEN + b, 0)), y_c, y_c] + consts,
        out_specs=pl.BlockSpec((CTX_LEN, S5_WIDTH), lambda b: (b, 0)),
        compiler_params=_cparams(("arbitrary",)),
        name="s5_glu_ctx",
    )(u, yf, yb, d_skip, glu_w)
    return s5_x, s5_c


def _s5_params(lam_re, lam_im, log_dt, b_re, b_im, c_re, c_im):
    dt = jnp.exp(log_dt)[..., None]
    mag = jnp.exp(lam_re * dt)
    a_re, a_im = mag * jnp.cos(lam_im * dt), mag * jnp.sin(lam_im * dt)
    den = lam_re * lam_re + lam_im * lam_im
    f_re = ((a_re - 1.0) * lam_re + a_im * lam_im) / den
    f_im = (a_im * lam_re - (a_re - 1.0) * lam_im) / den
    bb_re = f_re[..., None] * b_re - f_im[..., None] * b_im
    bb_im = f_re[..., None] * b_im + f_im[..., None] * b_re
    eye = jnp.eye(S5_GROUPS, dtype=F32)
    blk_b = lambda m: jnp.einsum('dgph,gk->dghkp', m, eye).reshape(2, S5_WIDTH, S5_GROUPS * S5_STATE)
    blk_c = lambda m: jnp.einsum('dghp,gk->dgpkh', m, eye).reshape(2, S5_GROUPS * S5_STATE, S5_WIDTH)
    bmat = jnp.concatenate([blk_b(bb_re), blk_b(bb_im)], axis=2).astype(BF16)
    cmat = jnp.concatenate([blk_c(c_re), -blk_c(c_im)], axis=1).astype(BF16)
    per_chain = lambda a: jnp.repeat(a.reshape(2, S5_SLABS // 2, LANES), BATCH, axis=0).transpose(1, 0, 2)
    return bmat, cmat, per_chain(a_re), per_chain(a_im)


def _s5_mixer(u, lam_re, lam_im, log_dt, b_re, b_im, c_re, c_im, d_skip, glu_w):
    bmat, cmat, a_re, a_im = _s5_params(lam_re, lam_im, log_dt, b_re, b_im, c_re, c_im)
    y_f, y_b = _s5_scan_chains(u, bmat, cmat, a_re, a_im)
    return _s5_glu(u, y_f, y_b, d_skip[None], glu_w.astype(BF16))


DN_TILE = 256
DN_HALO = 8
DN_XTILES = SEQ // DN_TILE
DN_XTILE = 512
DN_BASE = 2
DN_MERGES = (2, 4, 8, 16, 32)


def _exact_rows_dot(a, e):
    a_hi = a.astype(BF16)
    r = a - a_hi.astype(F32)
    a_mid = r.astype(BF16)
    a_lo = (r - a_mid.astype(F32)).astype(BF16)
    out = jnp.dot(a_hi, e, preferred_element_type=F32)
    out += jnp.dot(a_mid, e, preferred_element_type=F32)
    out += jnp.dot(a_lo, e, preferred_element_type=F32)
    return out


def _exact_cols_dot(e, a):
    a_hi = a.astype(BF16)
    r = a - a_hi.astype(F32)
    a_mid = r.astype(BF16)
    a_lo = (r - a_mid.astype(F32)).astype(BF16)
    out = jnp.dot(e, a_hi, preferred_element_type=F32)
    out += jnp.dot(e, a_mid, preferred_element_type=F32)
    out += jnp.dot(e, a_lo, preferred_element_type=F32)
    return out


def _dn_prep_kernel(x_ref, prev_ref, next_ref, ba_ref, cw_ref, nega_ref, dtb_ref, hsum_ref, expand_ref,
                    ltri_ref, utri_ref, qn_ref, kn_ref, v_ref, bf_ref, bb_ref, gf_ref, gb_ref, xe_ref):
    i = pl.program_id(0)
    is_ctx = i >= ROWS_X // DN_TILE
    first = jnp.logical_or(is_ctx, i % DN_XTILES == 0)
    last = jnp.logical_or(is_ctx, i % DN_XTILES == DN_XTILES - 1)
    xe_ref[0:DN_HALO, :] = jnp.where(first, 0.0, prev_ref[...])
    xe_ref[DN_HALO:DN_HALO + DN_TILE, :] = x_ref[...]
    xe_ref[DN_HALO + DN_TILE:, :] = jnp.where(last, 0.0, next_ref[...])
    pad = DN_CONV // 2
    y = jnp.zeros((DN_TILE, DN_QKV), F32)
    for j in range(DN_CONV):
        off = DN_HALO + j - pad
        y += cw_ref[j:j + 1, :] * xe_ref[off:off + DN_TILE, :]
    y = y * jax.nn.sigmoid(y)
    qk_w = DN_HEADS * DN_DK
    q, k = y[:, :qk_w], y[:, qk_w:2 * qk_w]
    hsum = hsum_ref[...]
    qn_ref[...] = q * lax.rsqrt(_exact_rows_dot(q * q, hsum) + EPS) * (DN_DK ** -0.5)
    kn_ref[...] = k * lax.rsqrt(_exact_rows_dot(k * k, hsum) + EPS)
    v_ref[...] = y[:, 2 * qk_w:]

    ba = ba_ref[...]
    lane = lax.broadcasted_iota(jnp.int32, (1, LANES), 1)
    z = ba + dtb_ref[...]
    softplus = jnp.maximum(z, 0.0) + jnp.log(1.0 + jnp.exp(-jnp.abs(z)))
    bg = jnp.where(lane < 2 * DN_HEADS, jax.nn.sigmoid(ba), nega_ref[...] * softplus)
    ex = _exact_rows_dot(bg, expand_ref[...])
    w = DN_WIDTH
    bf_ref[...] = ex[:, 0:w]
    bb_ref[...] = ex[:, w:2 * w]
    gf_ref[...] = _exact_cols_dot(ltri_ref[...], ex[:, 2 * w:3 * w])
    gb_ref[...] = _exact_cols_dot(utri_ref[...], ex[:, 3 * w:4 * w])


def _dn_prep(qkv, ba, conv_w, a_log, dt_bias):
    n_halo_blocks = ROWS // DN_HALO
    per_tile = DN_TILE // DN_HALO
    cw = jnp.concatenate([conv_w, jnp.zeros((8 - DN_CONV, DN_QKV), F32)], axis=0)
    lanes_g = jnp.zeros((LANES,), F32)
    nega = lanes_g.at[2 * DN_HEADS:4 * DN_HEADS].set(-jnp.exp(a_log.reshape(-1)))[None]
    dtb = lanes_g.at[2 * DN_HEADS:4 * DN_HEADS].set(dt_bias.reshape(-1))[None]
    head_of_lane = jnp.arange(DN_WIDTH) // DN_DV
    hsum = (head_of_lane[:, None] == head_of_lane[None, :]).astype(BF16)
    src = jnp.arange(LANES)[:, None]
    expand = jnp.concatenate(
        [(src == (g * DN_HEADS + head_of_lane)[None, :]) for g in range(4)], axis=1).astype(BF16)
    r = jnp.arange(DN_TILE)
    same_chunk = (r[:, None] // DN_CHUNK) == (r[None, :] // DN_CHUNK)
    ltri = (same_chunk & (r[:, None] >= r[None, :])).astype(BF16)
    utri = (same_chunk & (r[:, None] <= r[None, :])).astype(BF16)
    full = lambda a: pl.BlockSpec(a.shape, lambda i: (0,) * a.ndim)
    row = lambda n: pl.BlockSpec((DN_TILE, n), lambda i: (i, 0))
    return pl.pallas_call(
        _dn_prep_kernel,
        out_shape=[jax.ShapeDtypeStruct((ROWS, DN_WIDTH), F32)] * 7,
        grid=(ROWS // DN_TILE,),
        in_specs=[
            row(DN_QKV),
            pl.BlockSpec((DN_HALO, DN_QKV), lambda i: (jnp.maximum(i * per_tile - 1, 0), 0)),
            pl.BlockSpec((DN_HALO, DN_QKV), lambda i: (jnp.minimum((i + 1) * per_tile, n_halo_blocks - 1), 0)),
            row(LANES),
            full(cw), full(nega), full(dtb), full(hsum), full(expand), full(ltri), full(utri),
        ],
        out_specs=[row(DN_WIDTH)] * 7,
        scratch_shapes=[pltpu.VMEM((DN_TILE + 2 * DN_HALO, DN_QKV), F32)],
        compiler_params=_cparams(("arbitrary",)),
        name="dn_prep",
    )(qkv, qkv, qkv, ba, cw, nega, dtb, hsum, expand, ltri, utri)


def _dn_bd(y, blockmask):
    yb = y.astype(BF16)
    return jnp.where(blockmask, jnp.concatenate([yb] * DN_HEADS, axis=0), jnp.zeros((), BF16))


def _dn_prepare(chunks, blockmask, eye_cat, blk_base, off_masks):
    dot = functools.partial(jnp.dot, preferred_element_type=F32)
    bd = lambda y: _dn_bd(y, blockmask)

    a_mats, attns, egs = [], [], []
    for qn, kn, v, beta, gc, incl, strict, last in chunks:
        gcrow = jnp.sum(jnp.where(eye_cat, gc, 0.0), axis=0, keepdims=True)
        decay = jnp.exp(jnp.where(incl, gc - gcrow, -jnp.inf))
        grams = lax.dot_general(jnp.concatenate([kn, qn], axis=0).astype(BF16), bd(kn),
                                (((1,), (1,)), ((), ())), preferred_element_type=F32)
        a_mats.append(jnp.where(strict, grams[:DN_CHUNK] * beta * decay, 0.0))
        attns.append((grams[DN_CHUNK:] * decay).astype(BF16))
        egs.append(jnp.exp(gc))

    t_mats = [jnp.where(eye_cat, 1.0, 0.0) - jnp.where(blk_base, a, 0.0) for a in a_mats]
    mm1 = lambda x, y: dot(x.astype(BF16), bd(y))
    for off_mask in off_masks:
        z = [mm1(t, jnp.where(off_mask, a, 0.0)) for t, a in zip(t_mats, a_mats)]
        t_mats = [t - mm1(zi, t) for t, zi in zip(t_mats, z)]

    out = []
    for (qn, kn, v, beta, gc, incl, strict, last), t, attn, eg in zip(chunks, t_mats, attns, egs):
        tb = t.astype(BF16)
        glast = gc[last:last + 1, :]
        w_v = dot(tb, bd(v * beta))
        w_k = dot(tb, bd(kn * beta * eg))
        lhs = jnp.concatenate([w_k, qn * eg], axis=0).astype(BF16)
        k_dec = (kn * jnp.exp(glast - gc)).astype(BF16)
        out.append((w_v, lhs, attn, k_dec, jnp.exp(glast)))
    return out


def _dn_state_step(prep, s_bd, blockmask):
    w_v, lhs, attn, k_dec, e_last = prep
    dot = functools.partial(jnp.dot, preferred_element_type=F32)
    ps = dot(lhs, s_bd.astype(BF16))
    v_new = w_v - ps[:DN_CHUNK]
    o = ps[DN_CHUNK:] + dot(attn, _dn_bd(v_new, blockmask))
    upd = lax.dot_general(k_dec, v_new.astype(BF16), (((0,), (0,)), ((), ())), preferred_element_type=F32)
    return o, s_bd * e_last + jnp.where(blockmask, upd, 0.0)


def _dn_chunks_kernel(*refs, n_batch, n_chunk, has_init):
    qf_ref, kf_ref, vf_ref, bf_ref, gf_ref, qb_ref, kb_ref, vb_ref, bb_ref, gb_ref = refs[:10]
    if has_init:
        s0_ref, of_ref, ob_ref, s_ref = refs[10:]
    else:
        of_ref, ob_ref, sout_ref, s_ref = refs[10:]

    @pl.when(pl.program_id(1) == 0)
    def _():
        s_ref[...] = s0_ref[...] if has_init else jnp.zeros_like(s_ref)

    row = lax.broadcasted_iota(jnp.int32, (DN_WIDTH, DN_WIDTH), 0)
    col = lax.broadcasted_iota(jnp.int32, (DN_WIDTH, DN_WIDTH), 1)
    blockmask = (row // DN_DV) == (col // DN_DV)
    i = lax.broadcasted_iota(jnp.int32, (DN_CHUNK, DN_WIDTH), 0)
    j = lax.broadcasted_iota(jnp.int32, (DN_CHUNK, DN_WIDTH), 1) % DN_DV
    eye_cat = i == j
    blk_base = (i // DN_BASE) == (j // DN_BASE)
    off_masks = tuple(((i // (2 * s)) == (j // (2 * s))) & ((i // s) != (j // s))
                      for s in DN_MERGES)
    chunks, where = [], []
    for c in range(n_chunk):
        for bb in range(n_batch):
            base = bb * n_chunk * DN_CHUNK
            rf = slice(base + c * DN_CHUNK, base + (c + 1) * DN_CHUNK)
            chunks.append((qf_ref[rf, :], kf_ref[rf, :], vf_ref[rf, :], bf_ref[rf, :], gf_ref[rf, :],
                           i >= j, i > j, DN_CHUNK - 1))
            where.append((2 * bb, of_ref, rf))
            cb = n_chunk - 1 - c
            rb = slice(base + cb * DN_CHUNK, base + (cb + 1) * DN_CHUNK)
            chunks.append((qb_ref[rb, :], kb_ref[rb, :], vb_ref[rb, :], bb_ref[rb, :], gb_ref[rb, :],
                           i <= j, i < j, 0))
            where.append((2 * bb + 1, ob_ref, rb))
    prepared = _dn_prepare(chunks, blockmask, eye_cat, blk_base, off_masks)

    states = [s_ref[n] for n in range(2 * n_batch)]
    for prep, (n, o_ref, rows) in zip(prepared, where):
        o, states[n] = _dn_state_step(prep, states[n], blockmask)
        o_ref[rows, :] = o
    for n in range(2 * n_batch):
        s_ref[n] = states[n]
        if not has_init:
            sout_ref[n] = states[n]


def _dn_chunks(qn, kn, v, beta_f, beta_b, gc_f, gc_b):
    state_shape = jax.ShapeDtypeStruct((2 * BATCH, DN_WIDTH, DN_WIDTH), F32)
    args = (qn, kn, v, beta_f, gc_f, qn, kn, v, beta_b, gc_b)

    ctx_in = pl.BlockSpec((ROWS_C, DN_WIDTH), lambda b, t: (ROWS_X // ROWS_C, 0))
    ctx_out = pl.BlockSpec((ROWS_C, DN_WIDTH), lambda b, t: (0, 0))
    all_states = pl.BlockSpec((2 * BATCH, DN_WIDTH, DN_WIDTH), lambda b, t: (0, 0, 0))
    o_fc, o_bc, states = pl.pallas_call(
        functools.partial(_dn_chunks_kernel, n_batch=BATCH, n_chunk=CTX_LEN // DN_CHUNK, has_init=False),
        out_shape=[jax.ShapeDtypeStruct((ROWS_C, DN_WIDTH), F32)] * 2 + [state_shape],
        grid=(1, 1),
        in_specs=[ctx_in] * 10,
        out_specs=[ctx_out, ctx_out, all_states],
        scratch_shapes=[pltpu.VMEM((2 * BATCH, DN_WIDTH, DN_WIDTH), F32)],
        compiler_params=_cparams(("arbitrary", "arbitrary")),
        name="dn_chunks_ctx",
    )(*args)

    n_x = SEQ // DN_XTILE
    spec_f = pl.BlockSpec((DN_XTILE, DN_WIDTH), lambda b, t: (b * n_x + t, 0))
    spec_b = pl.BlockSpec((DN_XTILE, DN_WIDTH), lambda b, t: (b * n_x + n_x - 1 - t, 0))
    o_fx, o_bx = pl.pallas_call(
        functools.partial(_dn_chunks_kernel, n_batch=1, n_chunk=DN_XTILE // DN_CHUNK, has_init=True),
        out_shape=[jax.ShapeDtypeStruct((ROWS_X, DN_WIDTH), F32)] * 2,
        grid=(BATCH, n_x),
        in_specs=[spec_f] * 5 + [spec_b] * 5 + [pl.BlockSpec((2, DN_WIDTH, DN_WIDTH), lambda b, t: (b, 0, 0))],
        out_specs=[spec_f, spec_b],
        scratch_shapes=[pltpu.VMEM((2, DN_WIDTH, DN_WIDTH), F32)],
        compiler_params=_cparams(("arbitrary", "arbitrary")),
        name="dn_chunks",
    )(*args, states)
    return o_fx, o_bx, o_fc, o_bc


def _dn_out_kernel(of_ref, ob_ref, gt_ref, nw_ref, hsum_ref, o_ref):
    o = of_ref[...] + ob_ref[...]
    ms = _exact_rows_dot(o * o, hsum_ref[...]) * (1.0 / DN_DV)
    gt = gt_ref[...]
    o_ref[...] = (o * lax.rsqrt(ms + EPS) * nw_ref[...] * (gt * jax.nn.sigmoid(gt))).astype(BF16)


def _dn_out(o_fx, o_bx, o_fc, o_bc, gt, norm_w):
    head_of_lane = jnp.arange(DN_WIDTH) // DN_DV
    hsum = (head_of_lane[:, None] == head_of_lane[None, :]).astype(BF16)
    nw = jnp.tile(norm_w, DN_HEADS)[None]
    consts = [pl.BlockSpec((1, DN_WIDTH), lambda i: (0, 0)), pl.BlockSpec((DN_WIDTH, DN_WIDTH), lambda i: (0, 0))]
    row = pl.BlockSpec((TG, DN_WIDTH), lambda i: (i, 0))

    def call(o_f, o_b, gt_spec, n_rows, name):
        return pl.pallas_call(
            _dn_out_kernel,
            out_shape=jax.ShapeDtypeStruct((n_rows, DN_WIDTH), BF16),
            grid=(n_rows // TG,),
            in_specs=[row, row, gt_spec] + consts,
            out_specs=row,
            compiler_params=_cparams(("arbitrary",)),
            name=name,
        )(o_f, o_b, gt, nw, hsum)

    dn_x = call(o_fx, o_bx, row, ROWS_X, "dn_out")
    dn_c = call(o_fc, o_bc, pl.BlockSpec((TG, DN_WIDTH), lambda i: (ROWS_X // TG + i, 0)), ROWS_C, "dn_out_ctx")
    return dn_x, dn_c


def _dn_mixer(qkv, gt, ba, conv_w, a_log, dt_bias, norm_w):
    qn, kn, v, beta_f, beta_b, gc_f, gc_b = _dn_prep(qkv, ba, conv_w, a_log, dt_bias)
    return _dn_out(*_dn_chunks(qn, kn, v, beta_f, beta_b, gc_f, gc_b), gt, norm_w)


def _pad_w_in(w):
    d = w.shape[0]
    s = [0]
    for n in IN_SIZES:
        s.append(s[-1] + n)
    u, cq, ckv, kr, qkv, gt, ba = (w[:, s[i]:s[i + 1]] for i in range(7))
    z = lambda n: jnp.zeros((d, n), w.dtype)
    return jnp.concatenate(
        [u, cq, ckv, z(KR_LANE), kr, z(LANES - KR_LANE - MLA_ROPE), qkv, gt, ba, z(LANES - 4 * DN_HEADS)], axis=1)


def _head_slots(w, width):
    kdim = w.shape[0]
    w = w.reshape(kdim, MLA_HEADS, width)
    return jnp.pad(w, ((0, 0), (0, 0), (0, HEAD_SLOT - width))).reshape(kdim, MLA_HEADS * HEAD_SLOT)


def _rope_swap(w):
    lane = jnp.arange(HEAD_SLOT)
    half = MLA_ROPE // 2
    src = jnp.where(lane < MLA_NOPE + half, lane + half, lane - half)
    in_rope = (lane >= MLA_NOPE) & (lane < MLA_DH)
    w3 = w.reshape(w.shape[0], -1, HEAD_SLOT)
    return jnp.where(in_rope, w3[:, :, jnp.clip(src, 0, HEAD_SLOT - 1)], jnp.zeros((), w.dtype)).reshape(w.shape)


def _rope_tables():
    rows = SEQ // GRID_W
    row = jnp.repeat(jnp.arange(rows, dtype=F32), GRID_W)
    col = jnp.tile(jnp.arange(GRID_W, dtype=F32), rows)
    inv = ROPE_BASE ** (-jnp.arange(ROPE_FREQS, dtype=F32) / ROPE_FREQS)
    ang = jnp.concatenate([row[:, None] * inv, col[:, None] * inv], axis=-1)
    cos, sin = jnp.cos(ang), jnp.sin(ang)
    ones = lambda n: jnp.ones((SEQ, n), F32)
    zeros = lambda n: jnp.zeros((SEQ, n), F32)
    cos_t = jnp.concatenate([ones(MLA_NOPE), cos, cos, ones(HEAD_SLOT - MLA_DH)], axis=1)
    sin_t = jnp.concatenate([zeros(MLA_NOPE), -sin, sin, zeros(HEAD_SLOT - MLA_DH)], axis=1)
    cos_t = jnp.concatenate([jnp.tile(cos_t, (BATCH, 1)), jnp.ones((ROWS_C, HEAD_SLOT), F32)], axis=0)
    sin_t = jnp.concatenate([jnp.tile(sin_t, (BATCH, 1)), jnp.zeros((ROWS_C, HEAD_SLOT), F32)], axis=0)
    return cos_t, sin_t


def kernel(x, c, ctx, c_ctx, w_ada, b_ada, norm1_w, w_in, s5_lambda_re, s5_lambda_im, s5_log_dt, s5_b_re, s5_b_im, s5_c_re, s5_c_im, s5_d, s5_glu_w, mla_q_norm_w, mla_kv_norm_w, mla_w_uq, mla_w_ukv, mla_q_gain, mla_k_gain, dn_conv_w, dn_a_log, dn_dt_bias, dn_norm_w, w_out, norm2_w, w_ff1, w_ff2):
    h = jnp.concatenate([x.reshape(ROWS_X, D_MODEL), ctx.reshape(ROWS_C, D_MODEL)], axis=0)
    c_pad = jnp.concatenate([c, c_ctx[None], jnp.zeros((3, D_MODEL), F32)], axis=0)
    mods_all = _ada_all(c_pad, w_ada, b_ada)
    cos_t, sin_t = _rope_tables()

    for l in range(DEPTH):
        ctx_out = l < DEPTH - 1
        mods = mods_all[l, :BATCH + 1].reshape(BATCH + 1, 6, 1, D_MODEL)
        wq = _head_slots(mla_w_uq[l], MLA_DH)
        w_ukv = mla_w_ukv[l].reshape(MLA_KV_RANK, MLA_HEADS, MLA_NOPE + MLA_V)
        wk = _head_slots(w_ukv[:, :, :MLA_NOPE].reshape(MLA_KV_RANK, MLA_HEADS * MLA_NOPE), MLA_NOPE).astype(BF16)
        wvt = w_ukv[:, :, MLA_NOPE:].reshape(MLA_KV_RANK, MLA_WIDTH).T.astype(BF16)
        qg, kg = (jnp.pad(g, (0, HEAD_SLOT - MLA_DH))[None] for g in (mla_q_gain[l], mla_k_gain[l]))
        mla_params = (mla_q_norm_w[l][None], mla_kv_norm_w[l][None], wq.astype(BF16), _rope_swap(wq).astype(BF16),
                      wk, wvt, qg, _rope_swap(qg), kg, _rope_swap(kg), _rope_swap(jnp.eye(HEAD_SLOT, dtype=BF16)))
        u, qkv, gt, ba, q, k4, vt4 = _in_proj(h, norm1_w[l][None], mods, _pad_w_in(w_in[l]).astype(BF16),
                                              cos_t, sin_t, mla_params)

        s5 = _s5_mixer(u, s5_lambda_re[l], s5_lambda_im[l], s5_log_dt[l], s5_b_re[l], s5_b_im[l],
                       s5_c_re[l], s5_c_im[l], s5_d[l], s5_glu_w[l])
        ot = _flash(q, k4, vt4, ctx_out)
        dn = _dn_mixer(qkv, gt, ba, dn_conv_w[l], dn_a_log[l], dn_dt_bias[l], dn_norm_w[l])

        wo = w_out[l].astype(BF16)
        n_rows = ROWS if ctx_out else ROWS_X
        h = _out_mlp(h, s5, ot, dn, wo[:S5_WIDTH], wo[S5_WIDTH:S5_WIDTH + MLA_WIDTH], wo[S5_WIDTH + MLA_WIDTH:],
                     mods, norm2_w[l][None], w_ff1[l].astype(BF16), w_ff2[l].astype(BF16), n_rows)
    return h.reshape(BATCH, SEQ, D_MODEL)
```

```python
import functools

import jax
import jax.numpy as jnp
from jax import lax
from jax.experimental import pallas as pl
from jax.experimental.pallas import tpu as pltpu

F32 = jnp.float32
BF16 = jnp.bfloat16

D_MODEL = 1024
BATCH = 4
SEQ = 4096
DEPTH = 4
GRID_W = 64
CTX_LEN = 256
EPS = 1e-6

S5_GROUP = 16
S5_GROUPS = 16
S5_WIDTH = 256
S5_STATE = 64

MLA_HEADS = 8
MLA_NOPE = 64
MLA_ROPE = 32
MLA_V = 64
MLA_VE = MLA_V + 16
MLA_DH = 96
MLA_Q_RANK = 256
MLA_KV_RANK = 128
MLA_WIDTH = 512
ROPE_BASE = 10000.0
ROPE_FREQS = 8

DN_HEADS = 4
DN_DK = 64
DN_DV = 64
DN_QKV = 768
DN_WIDTH = 256
DN_CONV = 5
DN_CHUNK = 64

D_FF = 4 * D_MODEL
IN_SIZES = (S5_WIDTH, MLA_Q_RANK, MLA_KV_RANK, MLA_ROPE, DN_QKV, DN_WIDTH, 4 * DN_HEADS)

ROWS_X = BATCH * SEQ
ROWS_C = BATCH * CTX_LEN
ROWS = ROWS_X + ROWS_C
LANES = 128
HEAD_SLOT = LANES
KEY_BLK = 256
N_XBLK = SEQ // KEY_BLK
CTX_BLK0 = ROWS_X // KEY_BLK

IN_PAD = 1920
COL_U, COL_CQ, COL_CKV, COL_KR, COL_QKV, COL_GT, COL_BA = 0, 256, 512, 640, 768, 1536, 1792
KR_LANE = MLA_NOPE

TM = 512
TG = 1024
TQ = 4096
FLASH_KB = 2
ADA_TN = 1536
VMEM_LIMIT = 56 * 1024 * 1024

LOG2E = 1.4426950408889634


def _cparams(sem):
    return pltpu.CompilerParams(dimension_semantics=sem, vmem_limit_bytes=VMEM_LIMIT)


def _split3_dot(a, w):
    a_hi = a.astype(BF16)
    a_lo = (a - a_hi.astype(F32)).astype(BF16)
    w_hi = w.astype(BF16)
    w_lo = (w - w_hi.astype(F32)).astype(BF16)
    out = jnp.dot(a_hi, w_hi, preferred_element_type=F32)
    out += jnp.dot(a_lo, w_hi, preferred_element_type=F32)
    out += jnp.dot(a_hi, w_lo, preferred_element_type=F32)
    return out


def _rms_rows(x, w):
    return x * lax.rsqrt(jnp.mean(x * x, axis=-1, keepdims=True) + EPS) * w


def _ada_kernel(c_ref, w_ref, b_ref, o_ref):
    c = c_ref[...]
    s = c * jax.nn.sigmoid(c)
    o_ref[...] = _split3_dot(s, w_ref[...]) + b_ref[...]


def _ada_all(c_pad, w_ada, b_ada):
    tn = ADA_TN
    return pl.pallas_call(
        _ada_kernel,
        out_shape=jax.ShapeDtypeStruct((DEPTH, 8, 6 * D_MODEL), F32),
        grid=(DEPTH, 6 * D_MODEL // tn),
        in_specs=[
            pl.BlockSpec((8, D_MODEL), lambda l, j: (0, 0)),
            pl.BlockSpec((None, D_MODEL, tn), lambda l, j: (l, 0, j)),
            pl.BlockSpec((None, 1, tn), lambda l, j: (l, 0, j)),
        ],
        out_specs=pl.BlockSpec((None, 8, tn), lambda l, j: (l, 0, j)),
        compiler_params=_cparams(("arbitrary", "arbitrary")),
        name="ada_mod",
    )(c_pad, w_ada, b_ada.reshape(DEPTH, 1, 6 * D_MODEL))


def _mod_index(i):
    return jnp.minimum(i // (SEQ // TM), BATCH)


def _mla_heads(m, cos, sin, qnw, kvnw, wq, wq_sw, wk, wvt, qg, qg_sw, kg, kg_sw, perm, q_ref, k_ref, vt_ref):
    cqn = _rms_rows(m[:, 0:MLA_Q_RANK], qnw).astype(BF16)
    ckvn = _rms_rows(m[:, MLA_Q_RANK:MLA_Q_RANK + MLA_KV_RANK], kvnw).astype(BF16)
    kr = m[:, MLA_Q_RANK + MLA_KV_RANK:]
    qf = jnp.dot(cqn, wq, preferred_element_type=F32)
    qf_sw = jnp.dot(cqn, wq_sw, preferred_element_type=F32)
    kf = jnp.dot(ckvn, wk, preferred_element_type=F32)
    vt = lax.dot_general(wvt, ckvn, (((1,), (1,)), ((), ())), preferred_element_type=F32)
    kr_sw = _exact_rows_dot(kr, perm)
    q_cos, q_sin = cos * qg, sin * qg_sw
    k_cos, k_sin = cos * kg, sin * kg_sw

    def head_norm_rope(xh, xh_sw, g_cos, g_sin, scale):
        ss = jnp.sum(xh * xh, axis=-1, keepdims=True) * (1.0 / MLA_DH)
        return (xh * g_cos + xh_sw * g_sin) * (lax.rsqrt(ss + EPS) * scale)

    n_sub = m.shape[0] // KEY_BLK
    ones = jnp.ones((MLA_VE - MLA_V, m.shape[0]), BF16)
    for h in range(MLA_HEADS):
        sl = slice(h * HEAD_SLOT, (h + 1) * HEAD_SLOT)
        q_ref[h] = head_norm_rope(qf[:, sl], qf_sw[:, sl], q_cos, q_sin, MLA_DH ** -0.5 * LOG2E).astype(BF16)
        kh = head_norm_rope(kf[:, sl] + kr, kr_sw, k_cos, k_sin, 1.0).astype(BF16)
        vh = jnp.concatenate([vt[h * MLA_V:(h + 1) * MLA_V, :].astype(BF16), ones], axis=0)
        for s in range(n_sub):
            k_ref[h, s] = kh[s * KEY_BLK:(s + 1) * KEY_BLK, :]
            vt_ref[h, s] = vh[:, s * KEY_BLK:(s + 1) * KEY_BLK]


def _in_proj_kernel(h_ref, nw_ref, sh_ref, sc_ref, w_ref, cos_ref, sin_ref, qnw_ref, kvnw_ref, wq_ref, wqs_ref,
                    wk_ref, wvt_ref, qg_ref, qgs_ref, kg_ref, kgs_ref, perm_ref,
                    u_ref, qkv_ref, gt_ref, ba_ref, q_ref, k_ref, vt_ref):
    y = _rms_rows(h_ref[...], nw_ref[...]) * (1.0 + sc_ref[...]) + sh_ref[...]
    z = jnp.dot(y.astype(BF16), w_ref[...], preferred_element_type=F32)
    u_ref[...] = z[:, COL_U:COL_CQ]
    qkv_ref[...] = z[:, COL_QKV:COL_GT]
    gt_ref[...] = z[:, COL_GT:COL_BA]
    ba_ref[...] = z[:, COL_BA:IN_PAD]
    _mla_heads(z[:, COL_CQ:COL_QKV], cos_ref[...], sin_ref[...], qnw_ref[...], kvnw_ref[...], wq_ref[...],
               wqs_ref[...], wk_ref[...], wvt_ref[...], qg_ref[...], qgs_ref[...], kg_ref[...], kgs_ref[...],
               perm_ref[...], q_ref, k_ref, vt_ref)


def _in_proj(h, nw, mods, w_pad, cos_t, sin_t, mla_params):
    n_sub = TM // KEY_BLK
    mod_spec = lambda j: pl.BlockSpec((None, None, 1, D_MODEL), lambda i: (_mod_index(i), j, 0, 0))
    full = lambda a: pl.BlockSpec(a.shape, lambda i: (0,) * a.ndim)
    row = lambda n: pl.BlockSpec((TM, n), lambda i: (i, 0))
    row_widths = (COL_CQ - COL_U, COL_GT - COL_QKV, COL_BA - COL_GT, IN_PAD - COL_BA)
    return pl.pallas_call(
        _in_proj_kernel,
        out_shape=[jax.ShapeDtypeStruct((ROWS, n), F32) for n in row_widths] + [
            jax.ShapeDtypeStruct((MLA_HEADS, ROWS, HEAD_SLOT), BF16),
            jax.ShapeDtypeStruct((MLA_HEADS, ROWS // KEY_BLK, KEY_BLK, HEAD_SLOT), BF16),
            jax.ShapeDtypeStruct((MLA_HEADS, ROWS // KEY_BLK, MLA_VE, KEY_BLK), BF16),
        ],
        grid=(ROWS // TM,),
        in_specs=[row(D_MODEL), full(nw), mod_spec(0), mod_spec(1), full(w_pad), row(HEAD_SLOT), row(HEAD_SLOT)]
        + [full(p) for p in mla_params],
        out_specs=[row(n) for n in row_widths] + [
            pl.BlockSpec((MLA_HEADS, TM, HEAD_SLOT), lambda i: (0, i, 0)),
            pl.BlockSpec((MLA_HEADS, n_sub, KEY_BLK, HEAD_SLOT), lambda i: (0, i, 0, 0)),
            pl.BlockSpec((MLA_HEADS, n_sub, MLA_VE, KEY_BLK), lambda i: (0, i, 0, 0)),
        ],
        compiler_params=_cparams(("arbitrary",)),
        name="in_proj",
    )(h, nw, mods, mods, w_pad, cos_t, sin_t, *mla_params)


def _scores(k, q):
    return lax.dot_general(k, q, (((1,), (1,)), ((), ())), preferred_element_type=F32)


def _softmax_update(carry, s, vts):
    m, acc = carry
    m_new = jnp.maximum(m, jnp.max(s, axis=0, keepdims=True))
    pb = jnp.exp2(s - m_new).astype(BF16)
    acc = jnp.exp2(m - m_new) * acc
    for t, vt in enumerate(vts):
        acc += jnp.dot(vt, pb[t * KEY_BLK:(t + 1) * KEY_BLK], preferred_element_type=F32)
    return m_new, acc


def _softmax_attend(q, key_blocks):
    tq = q.shape[0]
    carry = (jnp.full((1, tq), -jnp.inf, F32), jnp.zeros((MLA_VE, tq), F32))
    s_next = _scores(key_blocks[0][0], q)
    for n, (_, vts) in enumerate(key_blocks):
        s = s_next
        if n + 1 < len(key_blocks):
            s_next = _scores(key_blocks[n + 1][0], q)
        carry = _softmax_update(carry, s, vts)
    _, acc = carry
    return (acc[:MLA_V] * (1.0 / acc[MLA_V:MLA_V + 1])).astype(BF16)


def _flash_x_kernel(q_ref, kx_ref, kc_ref, vtx_ref, vtc_ref, o_ref):
    blocks = [(kc_ref[0], [vtc_ref[0]])]
    for j in range(N_XBLK // FLASH_KB):
        k = kx_ref[j * FLASH_KB:(j + 1) * FLASH_KB].reshape(FLASH_KB * KEY_BLK, HEAD_SLOT)
        blocks.append((k, [vtx_ref[j * FLASH_KB + t] for t in range(FLASH_KB)]))
    o_ref[...] = _softmax_attend(q_ref[...], blocks)


def _flash_c_kernel(q_ref, kc_ref, vtc_ref, o_ref):
    for h in range(MLA_HEADS):
        o_ref[h * MLA_V:(h + 1) * MLA_V, :] = _softmax_attend(q_ref[h], [(kc_ref[h, 0], [vtc_ref[h, 0]])])


def _flash(q, k4, vt4, ctx_out):
    n_qx = SEQ // TQ
    ot_x = pl.pallas_call(
        _flash_x_kernel,
        out_shape=jax.ShapeDtypeStruct((MLA_WIDTH, ROWS_X), BF16),
        grid=(BATCH, MLA_HEADS, n_qx),
        in_specs=[
            pl.BlockSpec((None, TQ, HEAD_SLOT), lambda b, h, qi: (h, b * n_qx + qi, 0)),
            pl.BlockSpec((None, N_XBLK, KEY_BLK, HEAD_SLOT), lambda b, h, qi: (h, b, 0, 0)),
            pl.BlockSpec((None, 1, KEY_BLK, HEAD_SLOT), lambda b, h, qi: (h, CTX_BLK0 + b, 0, 0)),
            pl.BlockSpec((None, N_XBLK, MLA_VE, KEY_BLK), lambda b, h, qi: (h, b, 0, 0)),
            pl.BlockSpec((None, 1, MLA_VE, KEY_BLK), lambda b, h, qi: (h, CTX_BLK0 + b, 0, 0)),
        ],
        out_specs=pl.BlockSpec((MLA_V, TQ), lambda b, h, qi: (h, b * n_qx + qi)),
        compiler_params=_cparams(("arbitrary", "arbitrary", "arbitrary")),
        name="mla_flash_x",
    )(q, k4, k4, vt4, vt4)
    if not ctx_out:
        return ot_x, None
    ot_c = pl.pallas_call(
        _flash_c_kernel,
        out_shape=jax.ShapeDtypeStruct((MLA_WIDTH, ROWS_C), BF16),
        grid=(BATCH,),
        in_specs=[
            pl.BlockSpec((MLA_HEADS, CTX_LEN, HEAD_SLOT), lambda b: (0, CTX_BLK0 + b, 0)),
            pl.BlockSpec((MLA_HEADS, 1, KEY_BLK, HEAD_SLOT), lambda b: (0, CTX_BLK0 + b, 0, 0)),
            pl.BlockSpec((MLA_HEADS, 1, MLA_VE, KEY_BLK), lambda b: (0, CTX_BLK0 + b, 0, 0)),
        ],
        out_specs=pl.BlockSpec((MLA_WIDTH, CTX_LEN), lambda b: (0, b)),
        compiler_params=_cparams(("arbitrary",)),
        name="mla_flash_c",
    )(q, k4, vt4)
    return ot_x, ot_c


def _out_mlp_kernel(h_ref, s5x_ref, s5c_ref, otx_ref, otc_ref, dnx_ref, dnc_ref, wo_ref,
                    g2_ref, sh_ref, sc_ref, g5_ref, nw_ref, w1_ref, w2_ref, o_ref):
    wo_s5_ref = wo_ref.at[0:S5_WIDTH]
    wo_mla_ref = wo_ref.at[S5_WIDTH:S5_WIDTH + MLA_WIDTH]
    wo_dn_ref = wo_ref.at[S5_WIDTH + MLA_WIDTH:]
    is_ctx = pl.program_id(0) >= ROWS_X // TM
    s5 = jnp.where(is_ctx, s5c_ref[...], s5x_ref[...])
    ot = jnp.where(is_ctx, otc_ref[...], otx_ref[...])
    dn = jnp.where(is_ctx, dnc_ref[...], dnx_ref[...])
    mix = jnp.dot(s5, wo_s5_ref[...], preferred_element_type=F32)
    mix += lax.dot_general(ot, wo_mla_ref[...], (((0,), (0,)), ((), ())), preferred_element_type=F32)
    mix += jnp.dot(dn, wo_dn_ref[...], preferred_element_type=F32)
    x1 = h_ref[...] + g2_ref[...] * mix
    xn = (_rms_rows(x1, nw_ref[...]) * (1.0 + sc_ref[...]) + sh_ref[...]).astype(BF16)
    hid = jnp.maximum(jnp.dot(xn, w1_ref[...], preferred_element_type=F32), 0.0)
    o_ref[...] = x1 + g5_ref[...] * jnp.dot((hid * hid).astype(BF16), w2_ref[...], preferred_element_type=F32)


def _out_mlp(h, s5, ot, dn, wo, mods, nw, w1, w2, n_rows):
    n_xtiles = ROWS_X // TM
    (s5_x, s5_c), (ot_x, ot_c), (dn_x, dn_c) = ((a, a if n_rows == ROWS_X else c) for a, c in (s5, ot, dn))
    x_rows = lambda n: pl.BlockSpec((TM, n), lambda i: (jnp.minimum(i, n_xtiles - 1), 0))
    c_rows = lambda n: pl.BlockSpec((TM, n), lambda i: (jnp.maximum(i - n_xtiles, 0), 0))
    mod_spec = lambda j: pl.BlockSpec((None, None, 1, D_MODEL), lambda i: (_mod_index(i), j, 0, 0))
    full = lambda a: pl.BlockSpec(a.shape, lambda i: (0,) * a.ndim)
    row = lambda n: pl.BlockSpec((TM, n), lambda i: (i, 0))
    return pl.pallas_call(
        _out_mlp_kernel,
        out_shape=jax.ShapeDtypeStruct((n_rows, D_MODEL), F32),
        grid=(n_rows // TM,),
        in_specs=[
            row(D_MODEL), x_rows(S5_WIDTH), c_rows(S5_WIDTH),
            pl.BlockSpec((MLA_WIDTH, TM), lambda i: (0, jnp.minimum(i, n_xtiles - 1))),
            pl.BlockSpec((MLA_WIDTH, TM), lambda i: (0, jnp.maximum(i - n_xtiles, 0))),
            x_rows(DN_WIDTH), c_rows(DN_WIDTH),
            full(wo),
            mod_spec(2), mod_spec(3), mod_spec(4), mod_spec(5),
            full(nw), full(w1), full(w2),
        ],
        out_specs=row(D_MODEL),
        compiler_params=_cparams(("arbitrary",)),
        name="out_mlp",
    )(h, s5_x, s5_c, ot_x, ot_c, dn_x, dn_c, wo, mods, mods, mods, mods, nw, w1, w2)


S5_NSTATE = 2 * S5_GROUPS * S5_STATE
S5_SLABS = S5_NSTATE // LANES
S5_CHAINS = 2 * BATCH
S5_STEPS = CTX_LEN + SEQ
S5_TB = 128
S5_PITCH = S5_TB + 4


def _s5_kernel(*refs):
    u_refs = refs[:S5_CHAINS]
    flip_ref, bmat_ref, cmat_ref, are_ref, aim_ref, yf_ref, yb_ref, st_ref, hs_ref, h_ref = refs[S5_CHAINS:]

    @pl.when(pl.program_id(0) == 0)
    def _():
        h_ref[...] = jnp.zeros_like(h_ref)

    half = S5_SLABS // 2
    flip = flip_ref[...]
    for d in range(2):
        blocks = [u_refs[d * BATCH + b][...].astype(BF16) for b in range(BATCH)]
        if d == 1:
            blocks = [jnp.dot(flip, blk, preferred_element_type=F32).astype(BF16) for blk in blocks]
        bu = jnp.dot(jnp.concatenate(blocks, axis=0), bmat_ref[d], preferred_element_type=F32)
        for b in range(BATCH):
            row0 = (d * BATCH + b) * S5_PITCH
            for j in range(S5_SLABS):
                st_ref[j, row0:row0 + S5_TB, :] = bu[b * S5_TB:(b + 1) * S5_TB, j * LANES:(j + 1) * LANES]

    a_re = [are_ref[j] for j in range(half)]
    a_im = [aim_ref[j] for j in range(half)]

    def step(s, h):
        new = []
        rows = pl.ds(s, S5_CHAINS, stride=S5_PITCH)
        for j in range(half):
            hr, hi = h[j], h[half + j]
            nr = a_re[j] * hr - a_im[j] * hi + st_ref[j, rows, :]
            ni = a_re[j] * hi + a_im[j] * hr + st_ref[half + j, rows, :]
            hs_ref[j, rows, :] = nr
            hs_ref[half + j, rows, :] = ni
            new.append((nr, ni))
        return tuple(n[0] for n in new) + tuple(n[1] for n in new)

    h = lax.fori_loop(0, S5_TB, step, tuple(h_ref[j] for j in range(S5_SLABS)), unroll=4)
    for j in range(S5_SLABS):
        h_ref[j] = h[j]

    for d, y_ref in ((0, yf_ref), (1, yb_ref)):
        parts = []
        for b in range(BATCH):
            row0 = (d * BATCH + b) * S5_PITCH
            parts.append(jnp.concatenate([hs_ref[j, row0:row0 + S5_TB, :] for j in range(S5_SLABS)], axis=1))
        hm = jnp.concatenate(parts, axis=0).astype(BF16)
        y = jnp.dot(hm, cmat_ref[d], preferred_element_type=F32)
        for b in range(BATCH):
            yb = y[b * S5_TB:(b + 1) * S5_TB]
            y_ref[b] = _exact_cols_dot(flip, yb) if d == 1 else yb


def _s5_step_block(i, b, reverse):
    n_ctx = CTX_LEN // S5_TB
    n_x = SEQ // S5_TB
    ctx0 = ROWS_X // S5_TB + b * n_ctx
    if reverse:
        return jnp.where(i < n_ctx, ctx0 + n_ctx - 1 - i, b * n_x + n_x - 1 - (i - n_ctx))
    return jnp.where(i < n_ctx, ctx0 + i, b * n_x + (i - n_ctx))


def _s5_time_block(i, reverse):
    n_ctx = CTX_LEN // S5_TB
    n_x = SEQ // S5_TB
    if reverse:
        return jnp.where(i < n_ctx, n_x + n_ctx - 1 - i, n_x - 1 - (i - n_ctx))
    return jnp.where(i < n_ctx, n_x + i, i - n_ctx)


def _s5_scan_chains(u, bmat, cmat, a_re, a_im):
    full = lambda a: pl.BlockSpec(a.shape, lambda i: (0,) * a.ndim)
    flip = jnp.eye(S5_TB, dtype=BF16)[::-1]
    u_specs = [pl.BlockSpec((S5_TB, S5_WIDTH), functools.partial(lambda i, b, rev: (_s5_step_block(i, b, rev), 0),
                                                                b=c % BATCH, rev=c >= BATCH))
               for c in range(S5_CHAINS)]
    y_shape = jax.ShapeDtypeStruct((BATCH, S5_STEPS, S5_WIDTH), F32)
    return pl.pallas_call(
        _s5_kernel,
        out_shape=[y_shape, y_shape],
        grid=(S5_STEPS // S5_TB,),
        in_specs=u_specs + [full(flip), full(bmat), full(cmat), full(a_re), full(a_im)],
        out_specs=[pl.BlockSpec((BATCH, S5_TB, S5_WIDTH), lambda i: (0, _s5_time_block(i, False), 0)),
                   pl.BlockSpec((BATCH, S5_TB, S5_WIDTH), lambda i: (0, _s5_time_block(i, True), 0))],
        scratch_shapes=[pltpu.VMEM((S5_SLABS, S5_CHAINS * S5_PITCH, LANES), F32),
                        pltpu.VMEM((S5_SLABS, S5_CHAINS * S5_PITCH, LANES), F32),
                        pltpu.VMEM((S5_SLABS, S5_CHAINS, LANES), F32)],
        compiler_params=_cparams(("arbitrary",)),
        name="s5_scan",
    )(*([u] * S5_CHAINS), flip, bmat, cmat, a_re, a_im)


def _s5_glu_kernel(u_ref, yf_ref, yb_ref, d_ref, w_ref, o_ref):
    y = d_ref[...] * u_ref[...] + yf_ref[...] + yb_ref[...]
    z = jnp.dot(jax.nn.gelu(y).astype(BF16), w_ref[...], preferred_element_type=F32)
    o_ref[...] = (z[:, :S5_WIDTH] * jax.nn.sigmoid(z[:, S5_WIDTH:])).astype(BF16)


def _s5_glu(u, yf, yb, d_skip, glu_w):
    tiles_b = SEQ // TG
    consts = [pl.BlockSpec((1, S5_WIDTH), lambda i: (0, 0)), pl.BlockSpec((S5_WIDTH, 2 * S5_WIDTH), lambda i: (0, 0))]
    y_x = pl.BlockSpec((None, TG, S5_WIDTH), lambda i: (i // tiles_b, i % tiles_b, 0))
    s5_x = pl.pallas_call(
        _s5_glu_kernel,
        out_shape=jax.ShapeDtypeStruct((ROWS_X, S5_WIDTH), BF16),
        grid=(ROWS_X // TG,),
        in_specs=[pl.BlockSpec((TG, S5_WIDTH), lambda i: (i, 0)), y_x, y_x] + consts,
        out_specs=pl.BlockSpec((TG, S5_WIDTH), lambda i: (i, 0)),
        compiler_params=_cparams(("arbitrary",)),
        name="s5_glu",
    )(u, yf, yb, d_skip, glu_w)
    y_c = pl.BlockSpec((None, CTX_LEN, S5_WIDTH), lambda b: (b, SEQ // CTX_LEN, 0))
    s5_c = pl.pallas_call(
        _s5_glu_kernel,
        out_shape=jax.ShapeDtypeStruct((ROWS_C, S5_WIDTH), BF16),
        grid=(BATCH,),
        in_specs=[pl.BlockSpec((CTX_LEN, S5_WIDTH), lambda b: (ROWS_X // CTX_LEN + b, 0)), y_c, y_c] + consts,
        out_specs=pl.BlockSpec((CTX_LEN, S5_WIDTH), lambda b: (b, 0)),
        compiler_params=_cparams(("arbitrary",)),
        name="s5_glu_ctx",
    )(u, yf, yb, d_skip, glu_w)
    return s5_x, s5_c


def _s5_params(lam_re, lam_im, log_dt, b_re, b_im, c_re, c_im):
    dt = jnp.exp(log_dt)[..., None]
    mag = jnp.exp(lam_re * dt)
    a_re, a_im = mag * jnp.cos(lam_im * dt), mag * jnp.sin(lam_im * dt)
    den = lam_re * lam_re + lam_im * lam_im
    f_re = ((a_re - 1.0) * lam_re + a_im * lam_im) / den
    f_im = (a_im * lam_re - (a_re - 1.0) * lam_im) / den
    bb_re = f_re[..., None] * b_re - f_im[..., None] * b_im
    bb_im = f_re[..., None] * b_im + f_im[..., None] * b_re
    eye = jnp.eye(S5_GROUPS, dtype=F32)
    blk_b = lambda m: jnp.einsum('dgph,gk->dghkp', m, eye).reshape(2, S5_WIDTH, S5_GROUPS * S5_STATE)
    blk_c = lambda m: jnp.einsum('dghp,gk->dgpkh', m, eye).reshape(2, S5_GROUPS * S5_STATE, S5_WIDTH)
    bmat = jnp.concatenate([blk_b(bb_re), blk_b(bb_im)], axis=2).astype(BF16)
    cmat = jnp.concatenate([blk_c(c_re), -blk_c(c_im)], axis=1).astype(BF16)
    per_chain = lambda a: jnp.repeat(a.reshape(2, S5_SLABS // 2, LANES), BATCH, axis=0).transpose(1, 0, 2)
    return bmat, cmat, per_chain(a_re), per_chain(a_im)


def _s5_mixer(u, lam_re, lam_im, log_dt, b_re, b_im, c_re, c_im, d_skip, glu_w):
    bmat, cmat, a_re, a_im = _s5_params(lam_re, lam_im, log_dt, b_re, b_im, c_re, c_im)
    y_f, y_b = _s5_scan_chains(u, bmat, cmat, a_re, a_im)
    return _s5_glu(u, y_f, y_b, d_skip[None], glu_w.astype(BF16))


DN_TILE = 256
DN_HALO = 8
DN_XTILES = SEQ // DN_TILE
DN_XTILE = 512
DN_BASE = 2
DN_MERGES = (2, 4, 8, 16, 32)


def _exact_rows_dot(a, e):
    a_hi = a.astype(BF16)
    r = a - a_hi.astype(F32)
    a_mid = r.astype(BF16)
    a_lo = (r - a_mid.astype(F32)).astype(BF16)
    out = jnp.dot(a_hi, e, preferred_element_type=F32)
    out += jnp.dot(a_mid, e, preferred_element_type=F32)
    out += jnp.dot(a_lo, e, preferred_element_type=F32)
    return out


def _exact_cols_dot(e, a):
    a_hi = a.astype(BF16)
    r = a - a_hi.astype(F32)
    a_mid = r.astype(BF16)
    a_lo = (r - a_mid.astype(F32)).astype(BF16)
    out = jnp.dot(e, a_hi, preferred_element_type=F32)
    out += jnp.dot(e, a_mid, preferred_element_type=F32)
    out += jnp.dot(e, a_lo, preferred_element_type=F32)
    return out


def _dn_prep_kernel(x_ref, prev_ref, next_ref, ba_ref, cw_ref, nega_ref, dtb_ref, hsum_ref, expand_ref,
                    ltri_ref, utri_ref, qn_ref, kn_ref, v_ref, bf_ref, bb_ref, gf_ref, gb_ref, xe_ref):
    i = pl.program_id(0)
    is_ctx = i >= ROWS_X // DN_TILE
    first = jnp.logical_or(is_ctx, i % DN_XTILES == 0)
    last = jnp.logical_or(is_ctx, i % DN_XTILES == DN_XTILES - 1)
    xe_ref[0:DN_HALO, :] = jnp.where(first, 0.0, prev_ref[...])
    xe_ref[DN_HALO:DN_HALO + DN_TILE, :] = x_ref[...]
    xe_ref[DN_HALO + DN_TILE:, :] = jnp.where(last, 0.0, next_ref[...])
    pad = DN_CONV // 2
    y = jnp.zeros((DN_TILE, DN_QKV), F32)
    for j in range(DN_CONV):
        off = DN_HALO + j - pad
        y += cw_ref[j:j + 1, :] * xe_ref[off:off + DN_TILE, :]
    y = y * jax.nn.sigmoid(y)
    qk_w = DN_HEADS * DN_DK
    q, k = y[:, :qk_w], y[:, qk_w:2 * qk_w]
    hsum = hsum_ref[...]
    qn_ref[...] = q * lax.rsqrt(_exact_rows_dot(q * q, hsum) + EPS) * (DN_DK ** -0.5)
    kn_ref[...] = k * lax.rsqrt(_exact_rows_dot(k * k, hsum) + EPS)
    v_ref[...] = y[:, 2 * qk_w:]

    ba = ba_ref[...]
    lane = lax.broadcasted_iota(jnp.int32, (1, LANES), 1)
    z = ba + dtb_ref[...]
    softplus = jnp.maximum(z, 0.0) + jnp.log(1.0 + jnp.exp(-jnp.abs(z)))
    bg = jnp.where(lane < 2 * DN_HEADS, jax.nn.sigmoid(ba), nega_ref[...] * softplus)
    ex = _exact_rows_dot(bg, expand_ref[...])
    w = DN_WIDTH
    bf_ref[...] = ex[:, 0:w]
    bb_ref[...] = ex[:, w:2 * w]
    gf_ref[...] = _exact_cols_dot(ltri_ref[...], ex[:, 2 * w:3 * w])
    gb_ref[...] = _exact_cols_dot(utri_ref[...], ex[:, 3 * w:4 * w])


def _dn_prep(qkv, ba, conv_w, a_log, dt_bias):
    n_halo_blocks = ROWS // DN_HALO
    per_tile = DN_TILE // DN_HALO
    cw = jnp.concatenate([conv_w, jnp.zeros((8 - DN_CONV, DN_QKV), F32)], axis=0)
    lanes_g = jnp.zeros((LANES,), F32)
    nega = lanes_g.at[2 * DN_HEADS:4 * DN_HEADS].set(-jnp.exp(a_log.reshape(-1)))[None]
    dtb = lanes_g.at[2 * DN_HEADS:4 * DN_HEADS].set(dt_bias.reshape(-1))[None]
    head_of_lane = jnp.arange(DN_WIDTH) // DN_DV
    hsum = (head_of_lane[:, None] == head_of_lane[None, :]).astype(BF16)
    src = jnp.arange(LANES)[:, None]
    expand = jnp.concatenate(
        [(src == (g * DN_HEADS + head_of_lane)[None, :]) for g in range(4)], axis=1).astype(BF16)
    r = jnp.arange(DN_TILE)
    same_chunk = (r[:, None] // DN_CHUNK) == (r[None, :] // DN_CHUNK)
    ltri = (same_chunk & (r[:, None] >= r[None, :])).astype(BF16)
    utri = (same_chunk & (r[:, None] <= r[None, :])).astype(BF16)
    full = lambda a: pl.BlockSpec(a.shape, lambda i: (0,) * a.ndim)
    row = lambda n: pl.BlockSpec((DN_TILE, n), lambda i: (i, 0))
    return pl.pallas_call(
        _dn_prep_kernel,
        out_shape=[jax.ShapeDtypeStruct((ROWS, DN_WIDTH), F32)] * 7,
        grid=(ROWS // DN_TILE,),
        in_specs=[
            row(DN_QKV),
            pl.BlockSpec((DN_HALO, DN_QKV), lambda i: (jnp.maximum(i * per_tile - 1, 0), 0)),
            pl.BlockSpec((DN_HALO, DN_QKV), lambda i: (jnp.minimum((i + 1) * per_tile, n_halo_blocks - 1), 0)),
            row(LANES),
            full(cw), full(nega), full(dtb), full(hsum), full(expand), full(ltri), full(utri),
        ],
        out_specs=[row(DN_WIDTH)] * 7,
        scratch_shapes=[pltpu.VMEM((DN_TILE + 2 * DN_HALO, DN_QKV), F32)],
        compiler_params=_cparams(("arbitrary",)),
        name="dn_prep",
    )(qkv, qkv, qkv, ba, cw, nega, dtb, hsum, expand, ltri, utri)


def _dn_bd(y, blockmask):
    yb = y.astype(BF16)
    return jnp.where(blockmask, jnp.concatenate([yb] * DN_HEADS, axis=0), jnp.zeros((), BF16))


def _dn_prepare(chunks, blockmask, eye_cat, blk_base, off_masks):
    dot = functools.partial(jnp.dot, preferred_element_type=F32)
    bd = lambda y: _dn_bd(y, blockmask)

    a_mats, attns, egs = [], [], []
    for qn, kn, v, beta, gc, incl, strict, last in chunks:
        gcrow = jnp.sum(jnp.where(eye_cat, gc, 0.0), axis=0, keepdims=True)
        decay = jnp.exp(jnp.where(incl, gc - gcrow, -jnp.inf))
        grams = lax.dot_general(jnp.concatenate([kn, qn], axis=0).astype(BF16), bd(kn),
                                (((1,), (1,)), ((), ())), preferred_element_type=F32)
        a_mats.append(jnp.where(strict, grams[:DN_CHUNK] * beta * decay, 0.0))
        attns.append((grams[DN_CHUNK:] * decay).astype(BF16))
        egs.append(jnp.exp(gc))

    t_mats = [jnp.where(eye_cat, 1.0, 0.0) - jnp.where(blk_base, a, 0.0) for a in a_mats]
    mm1 = lambda x, y: dot(x.astype(BF16), bd(y))
    for off_mask in off_masks:
        z = [mm1(t, jnp.where(off_mask, a, 0.0)) for t, a in zip(t_mats, a_mats)]
        t_mats = [t - mm1(zi, t) for t, zi in zip(t_mats, z)]

    out = []
    for (qn, kn, v, beta, gc, incl, strict, last), t, attn, eg in zip(chunks, t_mats, attns, egs):
        tb = t.astype(BF16)
        glast = gc[last:last + 1, :]
        w_v = dot(tb, bd(v * beta))
        w_k = dot(tb, bd(kn * beta * eg))
        lhs = jnp.concatenate([w_k, qn * eg], axis=0).astype(BF16)
        k_dec = (kn * jnp.exp(glast - gc)).astype(BF16)
        out.append((w_v, lhs, attn, k_dec, jnp.exp(glast)))
    return out


def _dn_state_step(prep, s_bd, blockmask):
    w_v, lhs, attn, k_dec, e_last = prep
    dot = functools.partial(jnp.dot, preferred_element_type=F32)
    ps = dot(lhs, s_bd.astype(BF16))
    v_new = w_v - ps[:DN_CHUNK]
    o = ps[DN_CHUNK:] + dot(attn, _dn_bd(v_new, blockmask))
    upd = lax.dot_general(k_dec, v_new.astype(BF16), (((0,), (0,)), ((), ())), preferred_element_type=F32)
    return o, s_bd * e_last + jnp.where(blockmask, upd, 0.0)


def _dn_chunks_kernel(*refs, n_batch, n_chunk, has_init):
    qf_ref, kf_ref, vf_ref, bf_ref, gf_ref, qb_ref, kb_ref, vb_ref, bb_ref, gb_ref = refs[:10]
    if has_init:
        s0_ref, of_ref, ob_ref, s_ref = refs[10:]
    else:
        of_ref, ob_ref, sout_ref, s_ref = refs[10:]

    @pl.when(pl.program_id(1) == 0)
    def _():
        s_ref[...] = s0_ref[...] if has_init else jnp.zeros_like(s_ref)

    row = lax.broadcasted_iota(jnp.int32, (DN_WIDTH, DN_WIDTH), 0)
    col = lax.broadcasted_iota(jnp.int32, (DN_WIDTH, DN_WIDTH), 1)
    blockmask = (row // DN_DV) == (col // DN_DV)
    i = lax.broadcasted_iota(jnp.int32, (DN_CHUNK, DN_WIDTH), 0)
    j = lax.broadcasted_iota(jnp.int32, (DN_CHUNK, DN_WIDTH), 1) % DN_DV
    eye_cat = i == j
    blk_base = (i // DN_BASE) == (j // DN_BASE)
    off_masks = tuple(((i // (2 * s)) == (j // (2 * s))) & ((i // s) != (j // s))
                      for s in DN_MERGES)
    chunks, where = [], []
    for c in range(n_chunk):
        for bb in range(n_batch):
            base = bb * n_chunk * DN_CHUNK
            rf = slice(base + c * DN_CHUNK, base + (c + 1) * DN_CHUNK)
            chunks.append((qf_ref[rf, :], kf_ref[rf, :], vf_ref[rf, :], bf_ref[rf, :], gf_ref[rf, :],
                           i >= j, i > j, DN_CHUNK - 1))
            where.append((2 * bb, of_ref, rf))
            cb = n_chunk - 1 - c
            rb = slice(base + cb * DN_CHUNK, base + (cb + 1) * DN_CHUNK)
            chunks.append((qb_ref[rb, :], kb_ref[rb, :], vb_ref[rb, :], bb_ref[rb, :], gb_ref[rb, :],
                           i <= j, i < j, 0))
            where.append((2 * bb + 1, ob_ref, rb))
    prepared = _dn_prepare(chunks, blockmask, eye_cat, blk_base, off_masks)

    states = [s_ref[n] for n in range(2 * n_batch)]
    for prep, (n, o_ref, rows) in zip(prepared, where):
        o, states[n] = _dn_state_step(prep, states[n], blockmask)
        o_ref[rows, :] = o
    for n in range(2 * n_batch):
        s_ref[n] = states[n]
        if not has_init:
            sout_ref[n] = states[n]


def _dn_chunks(qn, kn, v, beta_f, beta_b, gc_f, gc_b):
    state_shape = jax.ShapeDtypeStruct((2 * BATCH, DN_WIDTH, DN_WIDTH), F32)
    args = (qn, kn, v, beta_f, gc_f, qn, kn, v, beta_b, gc_b)

    ctx_in = pl.BlockSpec((ROWS_C, DN_WIDTH), lambda b, t: (ROWS_X // ROWS_C, 0))
    ctx_out = pl.BlockSpec((ROWS_C, DN_WIDTH), lambda b, t: (0, 0))
    all_states = pl.BlockSpec((2 * BATCH, DN_WIDTH, DN_WIDTH), lambda b, t: (0, 0, 0))
    o_fc, o_bc, states = pl.pallas_call(
        functools.partial(_dn_chunks_kernel, n_batch=BATCH, n_chunk=CTX_LEN // DN_CHUNK, has_init=False),
        out_shape=[jax.ShapeDtypeStruct((ROWS_C, DN_WIDTH), F32)] * 2 + [state_shape],
        grid=(1, 1),
        in_specs=[ctx_in] * 10,
        out_specs=[ctx_out, ctx_out, all_states],
        scratch_shapes=[pltpu.VMEM((2 * BATCH, DN_WIDTH, DN_WIDTH), F32)],
        compiler_params=_cparams(("arbitrary", "arbitrary")),
        name="dn_chunks_ctx",
    )(*args)

    n_x = SEQ // DN_XTILE
    spec_f = pl.BlockSpec((DN_XTILE, DN_WIDTH), lambda b, t: (b * n_x + t, 0))
    spec_b = pl.BlockSpec((DN_XTILE, DN_WIDTH), lambda b, t: (b * n_x + n_x - 1 - t, 0))
    o_fx, o_bx = pl.pallas_call(
        functools.partial(_dn_chunks_kernel, n_batch=1, n_chunk=DN_XTILE // DN_CHUNK, has_init=True),
        out_shape=[jax.ShapeDtypeStruct((ROWS_X, DN_WIDTH), F32)] * 2,
        grid=(BATCH, n_x),
        in_specs=[spec_f] * 5 + [spec_b] * 5 + [pl.BlockSpec((2, DN_WIDTH, DN_WIDTH), lambda b, t: (b, 0, 0))],
        out_specs=[spec_f, spec_b],
        scratch_shapes=[pltpu.VMEM((2, DN_WIDTH, DN_WIDTH), F32)],
        compiler_params=_cparams(("arbitrary", "arbitrary")),
        name="dn_chunks",
    )(*args, states)
    return o_fx, o_bx, o_fc, o_bc


def _dn_out_kernel(of_ref, ob_ref, gt_ref, nw_ref, hsum_ref, o_ref):
    o = of_ref[...] + ob_ref[...]
    ms = _exact_rows_dot(o * o, hsum_ref[...]) * (1.0 / DN_DV)
    gt = gt_ref[...]
    o_ref[...] = (o * lax.rsqrt(ms + EPS) * nw_ref[...] * (gt * jax.nn.sigmoid(gt))).astype(BF16)


def _dn_out(o_fx, o_bx, o_fc, o_bc, gt, norm_w):
    head_of_lane = jnp.arange(DN_WIDTH) // DN_DV
    hsum = (head_of_lane[:, None] == head_of_lane[None, :]).astype(BF16)
    nw = jnp.tile(norm_w, DN_HEADS)[None]
    consts = [pl.BlockSpec((1, DN_WIDTH), lambda i: (0, 0)), pl.BlockSpec((DN_WIDTH, DN_WIDTH), lambda i: (0, 0))]
    row = pl.BlockSpec((TG, DN_WIDTH), lambda i: (i, 0))

    def call(o_f, o_b, gt_spec, n_rows, name):
        return pl.pallas_call(
            _dn_out_kernel,
            out_shape=jax.ShapeDtypeStruct((n_rows, DN_WIDTH), BF16),
            grid=(n_rows // TG,),
            in_specs=[row, row, gt_spec] + consts,
            out_specs=row,
            compiler_params=_cparams(("arbitrary",)),
            name=name,
        )(o_f, o_b, gt, nw, hsum)

    dn_x = call(o_fx, o_bx, row, ROWS_X, "dn_out")
    dn_c = call(o_fc, o_bc, pl.BlockSpec((TG, DN_WIDTH), lambda i: (ROWS_X // TG + i, 0)), ROWS_C, "dn_out_ctx")
    return dn_x, dn_c


def _dn_mixer(qkv, gt, ba, conv_w, a_log, dt_bias, norm_w):
    qn, kn, v, beta_f, beta_b, gc_f, gc_b = _dn_prep(qkv, ba, conv_w, a_log, dt_bias)
    return _dn_out(*_dn_chunks(qn, kn, v, beta_f, beta_b, gc_f, gc_b), gt, norm_w)


def _pad_w_in(w):
    d = w.shape[0]
    s = [0]
    for n in IN_SIZES:
        s.append(s[-1] + n)
    u, cq, ckv, kr, qkv, gt, ba = (w[:, s[i]:s[i + 1]] for i in range(7))
    z = lambda n: jnp.zeros((d, n), w.dtype)
    return jnp.concatenate(
        [u, cq, ckv, z(KR_LANE), kr, z(LANES - KR_LANE - MLA_ROPE), qkv, gt, ba, z(LANES - 4 * DN_HEADS)], axis=1)


def _head_slots(w, width):
    kdim = w.shape[0]
    w = w.reshape(kdim, MLA_HEADS, width)
    return jnp.pad(w, ((0, 0), (0, 0), (0, HEAD_SLOT - width))).reshape(kdim, MLA_HEADS * HEAD_SLOT)


def _rope_swap(w):
    lane = jnp.arange(HEAD_SLOT)
    half = MLA_ROPE // 2
    src = jnp.where(lane < MLA_NOPE + half, lane + half, lane - half)
    in_rope = (lane >= MLA_NOPE) & (lane < MLA_DH)
    w3 = w.reshape(w.shape[0], -1, HEAD_SLOT)
    return jnp.where(in_rope, w3[:, :, jnp.clip(src, 0, HEAD_SLOT - 1)], jnp.zeros((), w.dtype)).reshape(w.shape)


def _rope_tables():
    rows = SEQ // GRID_W
    row = jnp.repeat(jnp.arange(rows, dtype=F32), GRID_W)
    col = jnp.tile(jnp.arange(GRID_W, dtype=F32), rows)
    inv = ROPE_BASE ** (-jnp.arange(ROPE_FREQS, dtype=F32) / ROPE_FREQS)
    ang = jnp.concatenate([row[:, None] * inv, col[:, None] * inv], axis=-1)
    cos, sin = jnp.cos(ang), jnp.sin(ang)
    ones = lambda n: jnp.ones((SEQ, n), F32)
    zeros = lambda n: jnp.zeros((SEQ, n), F32)
    cos_t = jnp.concatenate([ones(MLA_NOPE), cos, cos, ones(HEAD_SLOT - MLA_DH)], axis=1)
    sin_t = jnp.concatenate([zeros(MLA_NOPE), -sin, sin, zeros(HEAD_SLOT - MLA_DH)], axis=1)
    cos_t = jnp.concatenate([jnp.tile(cos_t, (BATCH, 1)), jnp.ones((ROWS_C, HEAD_SLOT), F32)], axis=0)
    sin_t = jnp.concatenate([jnp.tile(sin_t, (BATCH, 1)), jnp.zeros((ROWS_C, HEAD_SLOT), F32)], axis=0)
    return cos_t, sin_t


def kernel(x, c, ctx, c_ctx, w_ada, b_ada, norm1_w, w_in, s5_lambda_re, s5_lambda_im, s5_log_dt, s5_b_re, s5_b_im, s5_c_re, s5_c_im, s5_d, s5_glu_w, mla_q_norm_w, mla_kv_norm_w, mla_w_uq, mla_w_ukv, mla_q_gain, mla_k_gain, dn_conv_w, dn_a_log, dn_dt_bias, dn_norm_w, w_out, norm2_w, w_ff1, w_ff2):
    h = jnp.concatenate([x.reshape(ROWS_X, D_MODEL), ctx.reshape(ROWS_C, D_MODEL)], axis=0)
    c_pad = jnp.concatenate([c, c_ctx[None], jnp.zeros((3, D_MODEL), F32)], axis=0)
    mods_all = _ada_all(c_pad, w_ada, b_ada)
    cos_t, sin_t = _rope_tables()

    for l in range(DEPTH):
        ctx_out = l < DEPTH - 1
        mods = mods_all[l, :BATCH + 1].reshape(BATCH + 1, 6, 1, D_MODEL)
        wq = _head_slots(mla_w_uq[l], MLA_DH)
        w_ukv = mla_w_ukv[l].reshape(MLA_KV_RANK, MLA_HEADS, MLA_NOPE + MLA_V)
        wk = _head_slots(w_ukv[:, :, :MLA_NOPE].reshape(MLA_KV_RANK, MLA_HEADS * MLA_NOPE), MLA_NOPE).astype(BF16)
        wvt = w_ukv[:, :, MLA_NOPE:].reshape(MLA_KV_RANK, MLA_WIDTH).T.astype(BF16)
        qg, kg = (jnp.pad(g, (0, HEAD_SLOT - MLA_DH))[None] for g in (mla_q_gain[l], mla_k_gain[l]))
        mla_params = (mla_q_norm_w[l][None], mla_kv_norm_w[l][None], wq.astype(BF16), _rope_swap(wq).astype(BF16),
                      wk, wvt, qg, _rope_swap(qg), kg, _rope_swap(kg), _rope_swap(jnp.eye(HEAD_SLOT, dtype=BF16)))
        u, qkv, gt, ba, q, k4, vt4 = _in_proj(h, norm1_w[l][None], mods, _pad_w_in(w_in[l]).astype(BF16),
                                              cos_t, sin_t, mla_params)

        s5 = _s5_mixer(u, s5_lambda_re[l], s5_lambda_im[l], s5_log_dt[l], s5_b_re[l], s5_b_im[l],
                       s5_c_re[l], s5_c_im[l], s5_d[l], s5_glu_w[l])
        ot = _flash(q, k4, vt4, ctx_out)
        dn = _dn_mixer(qkv, gt, ba, dn_conv_w[l], dn_a_log[l], dn_dt_bias[l], dn_norm_w[l])

        n_rows = ROWS if ctx_out else ROWS_X
        h = _out_mlp(h, s5, ot, dn, w_out[l].astype(BF16), mods, norm2_w[l][None], w_ff1[l].astype(BF16), w_ff2[l].astype(BF16), n_rows)
    return h.reshape(BATCH, SEQ, D_MODEL)
```

```python
import functools

import jax
import jax.numpy as jnp
from jax import lax
from jax.experimental import pallas as pl
from jax.experimental.pallas import tpu as pltpu

F32 = jnp.float32
BF16 = jnp.bfloat16

D_MODEL = 1024
BATCH = 4
SEQ = 4096
DEPTH = 4
GRID_W = 64
CTX_LEN = 256
EPS = 1e-6

S5_GROUP = 16
S5_GROUPS = 16
S5_WIDTH = 256
S5_STATE = 64

MLA_HEADS = 8
MLA_NOPE = 64
MLA_ROPE = 32
MLA_V = 64
MLA_VE = MLA_V + 16
MLA_DH = 96
MLA_Q_RANK = 256
MLA_KV_RANK = 128
MLA_WIDTH = 512
ROPE_BASE = 10000.0
ROPE_FREQS = 8

DN_HEADS = 4
DN_DK = 64
DN_DV = 64
DN_QKV = 768
DN_WIDTH = 256
DN_CONV = 5
DN_CHUNK = 64

D_FF = 4 * D_MODEL
IN_SIZES = (S5_WIDTH, MLA_Q_RANK, MLA_KV_RANK, MLA_ROPE, DN_QKV, DN_WIDTH, 4 * DN_HEADS)

ROWS_X = BATCH * SEQ
ROWS_C = BATCH * CTX_LEN
ROWS = ROWS_X + ROWS_C
LANES = 128
HEAD_SLOT = LANES
KEY_BLK = 256
N_XBLK = SEQ // KEY_BLK
CTX_BLK0 = ROWS_X // KEY_BLK

IN_PAD = 1920
COL_U, COL_CQ, COL_CKV, COL_KR, COL_QKV, COL_GT, COL_BA = 0, 256, 512, 640, 768, 1536, 1792
KR_LANE = MLA_NOPE

TM = 512
TG = 1024
TQ = 4096
FLASH_KB = 2
ADA_TN = 1536
VMEM_LIMIT = 56 * 1024 * 1024

LOG2E = 1.4426950408889634


def _cparams(sem):
    return pltpu.CompilerParams(dimension_semantics=sem, vmem_limit_bytes=VMEM_LIMIT)


def _split3_dot(a, w):
    a_hi = a.astype(BF16)
    a_lo = (a - a_hi.astype(F32)).astype(BF16)
    w_hi = w.astype(BF16)
    w_lo = (w - w_hi.astype(F32)).astype(BF16)
    out = jnp.dot(a_hi, w_hi, preferred_element_type=F32)
    out += jnp.dot(a_lo, w_hi, preferred_element_type=F32)
    out += jnp.dot(a_hi, w_lo, preferred_element_type=F32)
    return out


def _rms_rows(x, w):
    return x * lax.rsqrt(jnp.mean(x * x, axis=-1, keepdims=True) + EPS) * w


def _ada_kernel(c_ref, w_ref, b_ref, o_ref):
    c = c_ref[...]
    s = c * jax.nn.sigmoid(c)
    o_ref[...] = _split3_dot(s, w_ref[...]) + b_ref[...]


def _ada_all(c_pad, w_ada, b_ada):
    tn = ADA_TN
    return pl.pallas_call(
        _ada_kernel,
        out_shape=jax.ShapeDtypeStruct((DEPTH, 8, 6 * D_MODEL), F32),
        grid=(DEPTH, 6 * D_MODEL // tn),
        in_specs=[
            pl.BlockSpec((8, D_MODEL), lambda l, j: (0, 0)),
            pl.BlockSpec((None, D_MODEL, tn), lambda l, j: (l, 0, j)),
            pl.BlockSpec((None, 1, tn), lambda l, j: (l, 0, j)),
        ],
        out_specs=pl.BlockSpec((None, 8, tn), lambda l, j: (l, 0, j)),
        compiler_params=_cparams(("arbitrary", "arbitrary")),
        name="ada_mod",
    )(c_pad, w_ada, b_ada.reshape(DEPTH, 1, 6 * D_MODEL))


def _mod_index(i):
    return jnp.minimum(i // (SEQ // TM), BATCH)


def _mla_heads(m, cos, sin, qnw, kvnw, wq, wq_sw, wk, wvt, qg, qg_sw, kg, kg_sw, perm, q_ref, k_ref, vt_ref):
    cqn = _rms_rows(m[:, 0:MLA_Q_RANK], qnw).astype(BF16)
    ckvn = _rms_rows(m[:, MLA_Q_RANK:MLA_Q_RANK + MLA_KV_RANK], kvnw).astype(BF16)
    kr = m[:, MLA_Q_RANK + MLA_KV_RANK:]
    qf = jnp.dot(cqn, wq, preferred_element_type=F32)
    qf_sw = jnp.dot(cqn, wq_sw, preferred_element_type=F32)
    kf = jnp.dot(ckvn, wk, preferred_element_type=F32)
    vt = lax.dot_general(wvt, ckvn, (((1,), (1,)), ((), ())), preferred_element_type=F32)
    kr_sw = _exact_rows_dot(kr, perm)
    q_cos, q_sin = cos * qg, sin * qg_sw
    k_cos, k_sin = cos * kg, sin * kg_sw

    def head_norm_rope(xh, xh_sw, g_cos, g_sin, scale):
        ss = jnp.sum(xh * xh, axis=-1, keepdims=True) * (1.0 / MLA_DH)
        return (xh * g_cos + xh_sw * g_sin) * (lax.rsqrt(ss + EPS) * scale)

    n_sub = m.shape[0] // KEY_BLK
    ones = jnp.ones((MLA_VE - MLA_V, m.shape[0]), BF16)
    for h in range(MLA_HEADS):
        sl = slice(h * HEAD_SLOT, (h + 1) * HEAD_SLOT)
        q_ref[h] = head_norm_rope(qf[:, sl], qf_sw[:, sl], q_cos, q_sin, MLA_DH ** -0.5 * LOG2E).astype(BF16)
        kh = head_norm_rope(kf[:, sl] + kr, kr_sw, k_cos, k_sin, 1.0).astype(BF16)
        vh = jnp.concatenate([vt[h * MLA_V:(h + 1) * MLA_V, :].astype(BF16), ones], axis=0)
        for s in range(n_sub):
            k_ref[h, s] = kh[s * KEY_BLK:(s + 1) * KEY_BLK, :]
            vt_ref[h, s] = vh[:, s * KEY_BLK:(s + 1) * KEY_BLK]


def _in_proj_kernel(h_ref, nw_ref, sh_ref, sc_ref, w_ref, cos_ref, sin_ref, qnw_ref, kvnw_ref, wq_ref, wqs_ref,
                    wk_ref, wvt_ref, qg_ref, qgs_ref, kg_ref, kgs_ref, perm_ref,
                    u_ref, qkv_ref, gt_ref, ba_ref, q_ref, k_ref, vt_ref):
    y = _rms_rows(h_ref[...], nw_ref[...]) * (1.0 + sc_ref[...]) + sh_ref[...]
    z = jnp.dot(y.astype(BF16), w_ref[...], preferred_element_type=F32)
    u_ref[...] = z[:, COL_U:COL_CQ]
    qkv_ref[...] = z[:, COL_QKV:COL_GT]
    gt_ref[...] = z[:, COL_GT:COL_BA]
    ba_ref[...] = z[:, COL_BA:IN_PAD]
    _mla_heads(z[:, COL_CQ:COL_QKV], cos_ref[...], sin_ref[...], qnw_ref[...], kvnw_ref[...], wq_ref[...],
               wqs_ref[...], wk_ref[...], wvt_ref[...], qg_ref[...], qgs_ref[...], kg_ref[...], kgs_ref[...],
               perm_ref[...], q_ref, k_ref, vt_ref)


def _in_proj(h, nw, mods, w_pad, cos_t, sin_t, mla_params):
    n_sub = TM // KEY_BLK
    mod_spec = lambda j: pl.BlockSpec((None, None, 1, D_MODEL), lambda i: (_mod_index(i), j, 0, 0))
    full = lambda a: pl.BlockSpec(a.shape, lambda i: (0,) * a.ndim)
    row = lambda n: pl.BlockSpec((TM, n), lambda i: (i, 0))
    row_widths = (COL_CQ - COL_U, COL_GT - COL_QKV, COL_BA - COL_GT, IN_PAD - COL_BA)
    return pl.pallas_call(
        _in_proj_kernel,
        out_shape=[jax.ShapeDtypeStruct((ROWS, n), F32) for n in row_widths] + [
            jax.ShapeDtypeStruct((MLA_HEADS, ROWS, HEAD_SLOT), BF16),
            jax.ShapeDtypeStruct((MLA_HEADS, ROWS // KEY_BLK, KEY_BLK, HEAD_SLOT), BF16),
            jax.ShapeDtypeStruct((MLA_HEADS, ROWS // KEY_BLK, MLA_VE, KEY_BLK), BF16),
        ],
        grid=(ROWS // TM,),
        in_specs=[row(D_MODEL), full(nw), mod_spec(0), mod_spec(1), full(w_pad), row(HEAD_SLOT), row(HEAD_SLOT)]
        + [full(p) for p in mla_params],
        out_specs=[row(n) for n in row_widths] + [
            pl.BlockSpec((MLA_HEADS, TM, HEAD_SLOT), lambda i: (0, i, 0)),
            pl.BlockSpec((MLA_HEADS, n_sub, KEY_BLK, HEAD_SLOT), lambda i: (0, i, 0, 0)),
            pl.BlockSpec((MLA_HEADS, n_sub, MLA_VE, KEY_BLK), lambda i: (0, i, 0, 0)),
        ],
        compiler_params=_cparams(("arbitrary",)),
        name="in_proj",
    )(h, nw, mods, mods, w_pad, cos_t, sin_t, *mla_params)


def _scores(k, q):
    return lax.dot_general(k, q, (((1,), (1,)), ((), ())), preferred_element_type=F32)


def _softmax_update(carry, s, vts):
    m, acc = carry
    m_new = jnp.maximum(m, jnp.max(s, axis=0, keepdims=True))
    pb = jnp.exp2(s - m_new).astype(BF16)
    acc = jnp.exp2(m - m_new) * acc
    for t, vt in enumerate(vts):
        acc += jnp.dot(vt, pb[t * KEY_BLK:(t + 1) * KEY_BLK], preferred_element_type=F32)
    return m_new, acc


def _softmax_attend(q, key_blocks):
    tq = q.shape[0]
    carry = (jnp.full((1, tq), -jnp.inf, F32), jnp.zeros((MLA_VE, tq), F32))
    s_next = _scores(key_blocks[0][0], q)
    for n, (_, vts) in enumerate(key_blocks):
        s = s_next
        if n + 1 < len(key_blocks):
            s_next = _scores(key_blocks[n + 1][0], q)
        carry = _softmax_update(carry, s, vts)
    _, acc = carry
    return (acc[:MLA_V] * (1.0 / acc[MLA_V:MLA_V + 1])).astype(BF16)


def _flash_x_kernel(q_ref, kx_ref, kc_ref, vtx_ref, vtc_ref, o_ref):
    blocks = [(kc_ref[0], [vtc_ref[0]])]
    for j in range(N_XBLK // FLASH_KB):
        k = kx_ref[j * FLASH_KB:(j + 1) * FLASH_KB].reshape(FLASH_KB * KEY_BLK, HEAD_SLOT)
        blocks.append((k, [vtx_ref[j * FLASH_KB + t] for t in range(FLASH_KB)]))
    o_ref[...] = _softmax_attend(q_ref[...], blocks)


def _flash_c_kernel(q_ref, kc_ref, vtc_ref, o_ref):
    for h in range(MLA_HEADS):
        o_ref[h * MLA_V:(h + 1) * MLA_V, :] = _softmax_attend(q_ref[h], [(kc_ref[h, 0], [vtc_ref[h, 0]])])


def _flash(q, k4, vt4, ctx_out):
    n_qx = SEQ // TQ
    ot_x = pl.pallas_call(
        _flash_x_kernel,
        out_shape=jax.ShapeDtypeStruct((MLA_WIDTH, ROWS_X), BF16),
        grid=(BATCH, MLA_HEADS, n_qx),
        in_specs=[
            pl.BlockSpec((None, TQ, HEAD_SLOT), lambda b, h, qi: (h, b * n_qx + qi, 0)),
            pl.BlockSpec((None, N_XBLK, KEY_BLK, HEAD_SLOT), lambda b, h, qi: (h, b, 0, 0)),
            pl.BlockSpec((None, 1, KEY_BLK, HEAD_SLOT), lambda b, h, qi: (h, CTX_BLK0 + b, 0, 0)),
            pl.BlockSpec((None, N_XBLK, MLA_VE, KEY_BLK), lambda b, h, qi: (h, b, 0, 0)),
            pl.BlockSpec((None, 1, MLA_VE, KEY_BLK), lambda b, h, qi: (h, CTX_BLK0 + b, 0, 0)),
        ],
        out_specs=pl.BlockSpec((MLA_V, TQ), lambda b, h, qi: (h, b * n_qx + qi)),
        compiler_params=_cparams(("arbitrary", "arbitrary", "arbitrary")),
        name="mla_flash_x",
    )(q, k4, k4, vt4, vt4)
    if not ctx_out:
        return ot_x, None
    ot_c = pl.pallas_call(
        _flash_c_kernel,
        out_shape=jax.ShapeDtypeStruct((MLA_WIDTH, ROWS_C), BF16),
        grid=(BATCH,),
        in_specs=[
            pl.BlockSpec((MLA_HEADS, CTX_LEN, HEAD_SLOT), lambda b: (0, CTX_BLK0 + b, 0)),
            pl.BlockSpec((MLA_HEADS, 1, KEY_BLK, HEAD_SLOT), lambda b: (0, CTX_BLK0 + b, 0, 0)),
            pl.BlockSpec((MLA_HEADS, 1, MLA_VE, KEY_BLK), lambda b: (0, CTX_BLK0 + b, 0, 0)),
        ],
        out_specs=pl.BlockSpec((MLA_WIDTH, CTX_LEN), lambda b: (0, b)),
        compiler_params=_cparams(("arbitrary",)),
        name="mla_flash_c",
    )(q, k4, vt4)
    return ot_x, ot_c


def _out_mlp_kernel(h_ref, s5x_ref, s5c_ref, otx_ref, otc_ref, dnx_ref, dnc_ref, wo_ref,
                    g2_ref, sh_ref, sc_ref, g5_ref, nw_ref, w1_ref, w2_ref, o_ref):
    wo_s5_ref = wo_ref.at[0:S5_WIDTH]
    wo_mla_ref = wo_ref.at[S5_WIDTH:S5_WIDTH + MLA_WIDTH]
    wo_dn_ref = wo_ref.at[S5_WIDTH + MLA_WIDTH:]
    is_ctx = pl.program_id(0) >= ROWS_X // TM
    s5 = jnp.where(is_ctx, s5c_ref[...], s5x_ref[...])
    ot = jnp.where(is_ctx, otc_ref[...], otx_ref[...])
    dn = jnp.where(is_ctx, dnc_ref[...], dnx_ref[...])
    mix = jnp.dot(s5, wo_s5_ref[...], preferred_element_type=F32)
    mix += lax.dot_general(ot, wo_mla_ref[...], (((0,), (0,)), ((), ())), preferred_element_type=F32)
    mix += jnp.dot(dn, wo_dn_ref[...], preferred_element_type=F32)
    x1 = h_ref[...] + g2_ref[...] * mix
    xn = (_rms_rows(x1, nw_ref[...]) * (1.0 + sc_ref[...]) + sh_ref[...]).astype(BF16)
    hid = jnp.maximum(jnp.dot(xn, w1_ref[...], preferred_element_type=F32), 0.0)
    o_ref[...] = x1 + g5_ref[...] * jnp.dot((hid * hid).astype(BF16), w2_ref[...], preferred_element_type=F32)


def _out_mlp(h, s5, ot, dn, wo, mods, nw, w1, w2, n_rows):
    n_xtiles = ROWS_X // TM
    (s5_x, s5_c), (ot_x, ot_c), (dn_x, dn_c) = ((a, a if n_rows == ROWS_X else c) for a, c in (s5, ot, dn))
    x_rows = lambda n: pl.BlockSpec((TM, n), lambda i: (jnp.minimum(i, n_xtiles - 1), 0))
    c_rows = lambda n: pl.BlockSpec((TM, n), lambda i: (jnp.maximum(i - n_xtiles, 0), 0))
    mod_spec = lambda j: pl.BlockSpec((None, None, 1, D_MODEL), lambda i: (_mod_index(i), j, 0, 0))
    full = lambda a: pl.BlockSpec(a.shape, lambda i: (0,) * a.ndim, pipeline_mode=pl.Buffered(1))
    row = lambda n: pl.BlockSpec((TM, n), lambda i: (i, 0))
    return pl.pallas_call(
        _out_mlp_kernel,
        out_shape=jax.ShapeDtypeStruct((n_rows, D_MODEL), F32),
        grid=(n_rows // TM,),
        in_specs=[
            row(D_MODEL), x_rows(S5_WIDTH), c_rows(S5_WIDTH),
            pl.BlockSpec((MLA_WIDTH, TM), lambda i: (0, jnp.minimum(i, n_xtiles - 1))),
            pl.BlockSpec((MLA_WIDTH, TM), lambda i: (0, jnp.maximum(i - n_xtiles, 0))),
            x_rows(DN_WIDTH), c_rows(DN_WIDTH),
            full(wo),
            mod_spec(2), mod_spec(3), mod_spec(4), mod_spec(5),
            full(nw), full(w1), full(w2),
        ],
        out_specs=row(D_MODEL),
        compiler_params=_cparams(("arbitrary",)),
        name="out_mlp",
    )(h, s5_x, s5_c, ot_x, ot_c, dn_x, dn_c, wo, mods, mods, mods, mods, nw, w1, w2)


S5_NSTATE = 2 * S5_GROUPS * S5_STATE
S5_SLABS = S5_NSTATE // LANES
S5_CHAINS = 2 * BATCH
S5_STEPS = CTX_LEN + SEQ
S5_TB = 128
S5_PITCH = S5_TB + 4


def _s5_kernel(*refs):
    u_refs = refs[:S5_CHAINS]
    flip_ref, bmat_ref, cmat_ref, are_ref, aim_ref, yf_ref, yb_ref, st_ref, hs_ref, h_ref = refs[S5_CHAINS:]

    @pl.when(pl.program_id(0) == 0)
    def _():
        h_ref[...] = jnp.zeros_like(h_ref)

    half = S5_SLABS // 2
    flip = flip_ref[...]
    for d in range(2):
        blocks = [u_refs[d * BATCH + b][...].astype(BF16) for b in range(BATCH)]
        if d == 1:
            blocks = [jnp.dot(flip, blk, preferred_element_type=F32).astype(BF16) for blk in blocks]
        bu = jnp.dot(jnp.concatenate(blocks, axis=0), bmat_ref[d], preferred_element_type=F32)
        for b in range(BATCH):
            row0 = (d * BATCH + b) * S5_PITCH
            for j in range(S5_SLABS):
                st_ref[j, row0:row0 + S5_TB, :] = bu[b * S5_TB:(b + 1) * S5_TB, j * LANES:(j + 1) * LANES]

    a_re = [are_ref[j] for j in range(half)]
    a_im = [aim_ref[j] for j in range(half)]

    def step(s, h):
        new = []
        rows = pl.ds(s, S5_CHAINS, stride=S5_PITCH)
        for j in range(half):
            hr, hi = h[j], h[half + j]
            nr = a_re[j] * hr - a_im[j] * hi + st_ref[j, rows, :]
            ni = a_re[j] * hi + a_im[j] * hr + st_ref[half + j, rows, :]
            hs_ref[j, rows, :] = nr
            hs_ref[half + j, rows, :] = ni
            new.append((nr, ni))
        return tuple(n[0] for n in new) + tuple(n[1] for n in new)

    h = lax.fori_loop(0, S5_TB, step, tuple(h_ref[j] for j in range(S5_SLABS)), unroll=4)
    for j in range(S5_SLABS):
        h_ref[j] = h[j]

    for d, y_ref in ((0, yf_ref), (1, yb_ref)):
        parts = []
        for b in range(BATCH):
            row0 = (d * BATCH + b) * S5_PITCH
            parts.append(jnp.concatenate([hs_ref[j, row0:row0 + S5_TB, :] for j in range(S5_SLABS)], axis=1))
        hm = jnp.concatenate(parts, axis=0).astype(BF16)
        y = jnp.dot(hm, cmat_ref[d], preferred_element_type=F32)
        for b in range(BATCH):
            yb = y[b * S5_TB:(b + 1) * S5_TB]
            y_ref[b] = _exact_cols_dot(flip, yb) if d == 1 else yb


def _s5_step_block(i, b, reverse):
    n_ctx = CTX_LEN // S5_TB
    n_x = SEQ // S5_TB
    ctx0 = ROWS_X // S5_TB + b * n_ctx
    if reverse:
        return jnp.where(i < n_ctx, ctx0 + n_ctx - 1 - i, b * n_x + n_x - 1 - (i - n_ctx))
    return jnp.where(i < n_ctx, ctx0 + i, b * n_x + (i - n_ctx))


def _s5_time_block(i, reverse):
    n_ctx = CTX_LEN // S5_TB
    n_x = SEQ // S5_TB
    if reverse:
        return jnp.where(i < n_ctx, n_x + n_ctx - 1 - i, n_x - 1 - (i - n_ctx))
    return jnp.where(i < n_ctx, n_x + i, i - n_ctx)


def _s5_scan_chains(u, bmat, cmat, a_re, a_im):
    full = lambda a: pl.BlockSpec(a.shape, lambda i: (0,) * a.ndim)
    flip = jnp.eye(S5_TB, dtype=BF16)[::-1]
    u_specs = [pl.BlockSpec((S5_TB, S5_WIDTH), functools.partial(lambda i, b, rev: (_s5_step_block(i, b, rev), 0),
                                                                b=c % BATCH, rev=c >= BATCH))
               for c in range(S5_CHAINS)]
    y_shape = jax.ShapeDtypeStruct((BATCH, S5_STEPS, S5_WIDTH), F32)
    return pl.pallas_call(
        _s5_kernel,
        out_shape=[y_shape, y_shape],
        grid=(S5_STEPS // S5_TB,),
        in_specs=u_specs + [full(flip), full(bmat), full(cmat), full(a_re), full(a_im)],
        out_specs=[pl.BlockSpec((BATCH, S5_TB, S5_WIDTH), lambda i: (0, _s5_time_block(i, False), 0)),
                   pl.BlockSpec((BATCH, S5_TB, S5_WIDTH), lambda i: (0, _s5_time_block(i, True), 0))],
        scratch_shapes=[pltpu.VMEM((S5_SLABS, S5_CHAINS * S5_PITCH, LANES), F32),
                        pltpu.VMEM((S5_SLABS, S5_CHAINS * S5_PITCH, LANES), F32),
                        pltpu.VMEM((S5_SLABS, S5_CHAINS, LANES), F32)],
        compiler_params=_cparams(("arbitrary",)),
        name="s5_scan",
    )(*([u] * S5_CHAINS), flip, bmat, cmat, a_re, a_im)


def _s5_glu_kernel(u_ref, yf_ref, yb_ref, d_ref, w_ref, o_ref):
    y = d_ref[...] * u_ref[...] + yf_ref[...] + yb_ref[...]
    z = jnp.dot(jax.nn.gelu(y).astype(BF16), w_ref[...], preferred_element_type=F32)
    o_ref[...] = (z[:, :S5_WIDTH] * jax.nn.sigmoid(z[:, S5_WIDTH:])).astype(BF16)


def _s5_glu(u, yf, yb, d_skip, glu_w):
    tiles_b = SEQ // TG
    consts = [pl.BlockSpec((1, S5_WIDTH), lambda i: (0, 0)), pl.BlockSpec((S5_WIDTH, 2 * S5_WIDTH), lambda i: (0, 0))]
    y_x = pl.BlockSpec((None, TG, S5_WIDTH), lambda i: (i // tiles_b, i % tiles_b, 0))
    s5_x = pl.pallas_call(
        _s5_glu_kernel,
        out_shape=jax.ShapeDtypeStruct((ROWS_X, S5_WIDTH), BF16),
        grid=(ROWS_X // TG,),
        in_specs=[pl.BlockSpec((TG, S5_WIDTH), lambda i: (i, 0)), y_x, y_x] + consts,
        out_specs=pl.BlockSpec((TG, S5_WIDTH), lambda i: (i, 0)),
        compiler_params=_cparams(("arbitrary",)),
        name="s5_glu",
    )(u, yf, yb, d_skip, glu_w)
    y_c = pl.BlockSpec((None, CTX_LEN, S5_WIDTH), lambda b: (b, SEQ // CTX_LEN, 0))
    s5_c = pl.pallas_call(
        _s5_glu_kernel,
        out_shape=jax.ShapeDtypeStruct((ROWS_C, S5_WIDTH), BF16),
        grid=(BATCH,),
        in_specs=[pl.BlockSpec((CTX_LEN, S5_WIDTH), lambda b: (ROWS_X // CTX_LEN + b, 0)), y_c, y_c] + consts,
        out_specs=pl.BlockSpec((CTX_LEN, S5_WIDTH), lambda b: (b, 0)),
        compiler_params=_cparams(("arbitrary",)),
        name="s5_glu_ctx",
    )(u, yf, yb, d_skip, glu_w)
    return s5_x, s5_c


def _s5_params(lam_re, lam_im, log_dt, b_re, b_im, c_re, c_im):
    dt = jnp.exp(log_dt)[..., None]
    mag = jnp.exp(lam_re * dt)
    a_re, a_im = mag * jnp.cos(lam_im * dt), mag * jnp.sin(lam_im * dt)
    den = lam_re * lam_re + lam_im * lam_im
    f_re = ((a_re - 1.0) * lam_re + a_im * lam_im) / den
    f_im = (a_im * lam_re - (a_re - 1.0) * lam_im) / den
    bb_re = f_re[..., None] * b_re - f_im[..., None] * b_im
    bb_im = f_re[..., None] * b_im + f_im[..., None] * b_re
    eye = jnp.eye(S5_GROUPS, dtype=F32)
    blk_b = lambda m: jnp.einsum('dgph,gk->dghkp', m, eye).reshape(2, S5_WIDTH, S5_GROUPS * S5_STATE)
    blk_c = lambda m: jnp.einsum('dghp,gk->dgpkh', m, eye).reshape(2, S5_GROUPS * S5_STATE, S5_WIDTH)
    bmat = jnp.concatenate([blk_b(bb_re), blk_b(bb_im)], axis=2).astype(BF16)
    cmat = jnp.concatenate([blk_c(c_re), -blk_c(c_im)], axis=1).astype(BF16)
    per_chain = lambda a: jnp.repeat(a.reshape(2, S5_SLABS // 2, LANES), BATCH, axis=0).transpose(1, 0, 2)
    return bmat, cmat, per_chain(a_re), per_chain(a_im)


def _s5_mixer(u, lam_re, lam_im, log_dt, b_re, b_im, c_re, c_im, d_skip, glu_w):
    bmat, cmat, a_re, a_im = _s5_params(lam_re, lam_im, log_dt, b_re, b_im, c_re, c_im)
    y_f, y_b = _s5_scan_chains(u, bmat, cmat, a_re, a_im)
    return _s5_glu(u, y_f, y_b, d_skip[None], glu_w.astype(BF16))


DN_TILE = 256
DN_HALO = 8
DN_XTILES = SEQ // DN_TILE
DN_XTILE = 512
DN_BASE = 2
DN_MERGES = (2, 4, 8, 16, 32)


def _exact_rows_dot(a, e):
    a_hi = a.astype(BF16)
    r = a - a_hi.astype(F32)
    a_mid = r.astype(BF16)
    a_lo = (r - a_mid.astype(F32)).astype(BF16)
    out = jnp.dot(a_hi, e, preferred_element_type=F32)
    out += jnp.dot(a_mid, e, preferred_element_type=F32)
    out += jnp.dot(a_lo, e, preferred_element_type=F32)
    return out


def _exact_cols_dot(e, a):
    a_hi = a.astype(BF16)
    r = a - a_hi.astype(F32)
    a_mid = r.astype(BF16)
    a_lo = (r - a_mid.astype(F32)).astype(BF16)
    out = jnp.dot(e, a_hi, preferred_element_type=F32)
    out += jnp.dot(e, a_mid, preferred_element_type=F32)
    out += jnp.dot(e, a_lo, preferred_element_type=F32)
    return out


def _dn_prep_kernel(x_ref, prev_ref, next_ref, ba_ref, cw_ref, nega_ref, dtb_ref, hsum_ref, expand_ref,
                    ltri_ref, utri_ref, qn_ref, kn_ref, v_ref, bf_ref, bb_ref, gf_ref, gb_ref, xe_ref):
    i = pl.program_id(0)
    is_ctx = i >= ROWS_X // DN_TILE
    first = jnp.logical_or(is_ctx, i % DN_XTILES == 0)
    last = jnp.logical_or(is_ctx, i % DN_XTILES == DN_XTILES - 1)
    xe_ref[0:DN_HALO, :] = jnp.where(first, 0.0, prev_ref[...])
    xe_ref[DN_HALO:DN_HALO + DN_TILE, :] = x_ref[...]
    xe_ref[DN_HALO + DN_TILE:, :] = jnp.where(last, 0.0, next_ref[...])
    pad = DN_CONV // 2
    y = jnp.zeros((DN_TILE, DN_QKV), F32)
    for j in range(DN_CONV):
        off = DN_HALO + j - pad
        y += cw_ref[j:j + 1, :] * xe_ref[off:off + DN_TILE, :]
    y = y * jax.nn.sigmoid(y)
    qk_w = DN_HEADS * DN_DK
    q, k = y[:, :qk_w], y[:, qk_w:2 * qk_w]
    hsum = hsum_ref[...]
    qn_ref[...] = q * lax.rsqrt(_exact_rows_dot(q * q, hsum) + EPS) * (DN_DK ** -0.5)
    kn_ref[...] = k * lax.rsqrt(_exact_rows_dot(k * k, hsum) + EPS)
    v_ref[...] = y[:, 2 * qk_w:]

    ba = ba_ref[...]
    lane = lax.broadcasted_iota(jnp.int32, (1, LANES), 1)
    z = ba + dtb_ref[...]
    softplus = jnp.maximum(z, 0.0) + jnp.log(1.0 + jnp.exp(-jnp.abs(z)))
    bg = jnp.where(lane < 2 * DN_HEADS, jax.nn.sigmoid(ba), nega_ref[...] * softplus)
    ex = _exact_rows_dot(bg, expand_ref[...])
    w = DN_WIDTH
    bf_ref[...] = ex[:, 0:w]
    bb_ref[...] = ex[:, w:2 * w]
    gf_ref[...] = _exact_cols_dot(ltri_ref[...], ex[:, 2 * w:3 * w])
    gb_ref[...] = _exact_cols_dot(utri_ref[...], ex[:, 3 * w:4 * w])


def _dn_prep(qkv, ba, conv_w, a_log, dt_bias):
    n_halo_blocks = ROWS // DN_HALO
    per_tile = DN_TILE // DN_HALO
    cw = jnp.concatenate([conv_w, jnp.zeros((8 - DN_CONV, DN_QKV), F32)], axis=0)
    lanes_g = jnp.zeros((LANES,), F32)
    nega = lanes_g.at[2 * DN_HEADS:4 * DN_HEADS].set(-jnp.exp(a_log.reshape(-1)))[None]
    dtb = lanes_g.at[2 * DN_HEADS:4 * DN_HEADS].set(dt_bias.reshape(-1))[None]
    head_of_lane = jnp.arange(DN_WIDTH) // DN_DV
    hsum = (head_of_lane[:, None] == head_of_lane[None, :]).astype(BF16)
    src = jnp.arange(LANES)[:, None]
    expand = jnp.concatenate(
        [(src == (g * DN_HEADS + head_of_lane)[None, :]) for g in range(4)], axis=1).astype(BF16)
    r = jnp.arange(DN_TILE)
    same_chunk = (r[:, None] // DN_CHUNK) == (r[None, :] // DN_CHUNK)
    ltri = (same_chunk & (r[:, None] >= r[None, :])).astype(BF16)
    utri = (same_chunk & (r[:, None] <= r[None, :])).astype(BF16)
    full = lambda a: pl.BlockSpec(a.shape, lambda i: (0,) * a.ndim)
    row = lambda n: pl.BlockSpec((DN_TILE, n), lambda i: (i, 0))
    return pl.pallas_call(
        _dn_prep_kernel,
        out_shape=[jax.ShapeDtypeStruct((ROWS, DN_WIDTH), F32)] * 7,
        grid=(ROWS // DN_TILE,),
        in_specs=[
            row(DN_QKV),
            pl.BlockSpec((DN_HALO, DN_QKV), lambda i: (jnp.maximum(i * per_tile - 1, 0), 0)),
            pl.BlockSpec((DN_HALO, DN_QKV), lambda i: (jnp.minimum((i + 1) * per_tile, n_halo_blocks - 1), 0)),
            row(LANES),
            full(cw), full(nega), full(dtb), full(hsum), full(expand), full(ltri), full(utri),
        ],
        out_specs=[row(DN_WIDTH)] * 7,
        scratch_shapes=[pltpu.VMEM((DN_TILE + 2 * DN_HALO, DN_QKV), F32)],
        compiler_params=_cparams(("arbitrary",)),
        name="dn_prep",
    )(qkv, qkv, qkv, ba, cw, nega, dtb, hsum, expand, ltri, utri)


def _dn_bd(y, blockmask):
    yb = y.astype(BF16)
    return jnp.where(blockmask, jnp.concatenate([yb] * DN_HEADS, axis=0), jnp.zeros((), BF16))


def _dn_prepare(chunks, blockmask, eye_cat, blk_base, off_masks):
    dot = functools.partial(jnp.dot, preferred_element_type=F32)
    bd = lambda y: _dn_bd(y, blockmask)

    a_mats, attns, egs = [], [], []
    for qn, kn, v, beta, gc, incl, strict, last in chunks:
        gcrow = jnp.sum(jnp.where(eye_cat, gc, 0.0), axis=0, keepdims=True)
        decay = jnp.exp(jnp.where(incl, gc - gcrow, -jnp.inf))
        grams = lax.dot_general(jnp.concatenate([kn, qn], axis=0).astype(BF16), bd(kn),
                                (((1,), (1,)), ((), ())), preferred_element_type=F32)
        a_mats.append(jnp.where(strict, grams[:DN_CHUNK] * beta * decay, 0.0))
        attns.append((grams[DN_CHUNK:] * decay).astype(BF16))
        egs.append(jnp.exp(gc))

    t_mats = [jnp.where(eye_cat, 1.0, 0.0) - jnp.where(blk_base, a, 0.0) for a in a_mats]
    mm1 = lambda x, y: dot(x.astype(BF16), bd(y))
    for off_mask in off_masks:
        z = [mm1(t, jnp.where(off_mask, a, 0.0)) for t, a in zip(t_mats, a_mats)]
        t_mats = [t - mm1(zi, t) for t, zi in zip(t_mats, z)]

    out = []
    for (qn, kn, v, beta, gc, incl, strict, last), t, attn, eg in zip(chunks, t_mats, attns, egs):
        tb = t.astype(BF16)
        glast = gc[last:last + 1, :]
        w_v = dot(tb, bd(v * beta))
        w_k = dot(tb, bd(kn * beta * eg))
        lhs = jnp.concatenate([w_k, qn * eg], axis=0).astype(BF16)
        k_dec = (kn * jnp.exp(glast - gc)).astype(BF16)
        out.append((w_v, lhs, attn, k_dec, jnp.exp(glast)))
    return out


def _dn_state_step(prep, s_bd, blockmask):
    w_v, lhs, attn, k_dec, e_last = prep
    dot = functools.partial(jnp.dot, preferred_element_type=F32)
    ps = dot(lhs, s_bd.astype(BF16))
    v_new = w_v - ps[:DN_CHUNK]
    o = ps[DN_CHUNK:] + dot(attn, _dn_bd(v_new, blockmask))
    upd = lax.dot_general(k_dec, v_new.astype(BF16), (((0,), (0,)), ((), ())), preferred_element_type=F32)
    return o, s_bd * e_last + jnp.where(blockmask, upd, 0.0)


def _dn_chunks_kernel(*refs, n_batch, n_chunk, has_init):
    qf_ref, kf_ref, vf_ref, bf_ref, gf_ref, qb_ref, kb_ref, vb_ref, bb_ref, gb_ref = refs[:10]
    if has_init:
        s0_ref, of_ref, ob_ref, s_ref = refs[10:]
    else:
        of_ref, ob_ref, sout_ref, s_ref = refs[10:]

    @pl.when(pl.program_id(1) == 0)
    def _():
        s_ref[...] = s0_ref[...] if has_init else jnp.zeros_like(s_ref)

    row = lax.broadcasted_iota(jnp.int32, (DN_WIDTH, DN_WIDTH), 0)
    col = lax.broadcasted_iota(jnp.int32, (DN_WIDTH, DN_WIDTH), 1)
    blockmask = (row // DN_DV) == (col // DN_DV)
    i = lax.broadcasted_iota(jnp.int32, (DN_CHUNK, DN_WIDTH), 0)
    j = lax.broadcasted_iota(jnp.int32, (DN_CHUNK, DN_WIDTH), 1) % DN_DV
    eye_cat = i == j
    blk_base = (i // DN_BASE) == (j // DN_BASE)
    off_masks = tuple(((i // (2 * s)) == (j // (2 * s))) & ((i // s) != (j // s))
                      for s in DN_MERGES)
    chunks, where = [], []
    for c in range(n_chunk):
        for bb in range(n_batch):
            base = bb * n_chunk * DN_CHUNK
            rf = slice(base + c * DN_CHUNK, base + (c + 1) * DN_CHUNK)
            chunks.append((qf_ref[rf, :], kf_ref[rf, :], vf_ref[rf, :], bf_ref[rf, :], gf_ref[rf, :],
                           i >= j, i > j, DN_CHUNK - 1))
            where.append((2 * bb, of_ref, rf))
            cb = n_chunk - 1 - c
            rb = slice(base + cb * DN_CHUNK, base + (cb + 1) * DN_CHUNK)
            chunks.append((qb_ref[rb, :], kb_ref[rb, :], vb_ref[rb, :], bb_ref[rb, :], gb_ref[rb, :],
                           i <= j, i < j, 0))
            where.append((2 * bb + 1, ob_ref, rb))
    prepared = _dn_prepare(chunks, blockmask, eye_cat, blk_base, off_masks)

    states = [s_ref[n] for n in range(2 * n_batch)]
    for prep, (n, o_ref, rows) in zip(prepared, where):
        o, states[n] = _dn_state_step(prep, states[n], blockmask)
        o_ref[rows, :] = o
    for n in range(2 * n_batch):
        s_ref[n] = states[n]
        if not has_init:
            sout_ref[n] = states[n]


def _dn_chunks(qn, kn, v, beta_f, beta_b, gc_f, gc_b):
    state_shape = jax.ShapeDtypeStruct((2 * BATCH, DN_WIDTH, DN_WIDTH), F32)
    args = (qn, kn, v, beta_f, gc_f, qn, kn, v, beta_b, gc_b)

    ctx_in = pl.BlockSpec((ROWS_C, DN_WIDTH), lambda b, t: (ROWS_X // ROWS_C, 0))
    ctx_out = pl.BlockSpec((ROWS_C, DN_WIDTH), lambda b, t: (0, 0))
    all_states = pl.BlockSpec((2 * BATCH, DN_WIDTH, DN_WIDTH), lambda b, t: (0, 0, 0))
    o_fc, o_bc, states = pl.pallas_call(
        functools.partial(_dn_chunks_kernel, n_batch=BATCH, n_chunk=CTX_LEN // DN_CHUNK, has_init=False),
        out_shape=[jax.ShapeDtypeStruct((ROWS_C, DN_WIDTH), F32)] * 2 + [state_shape],
        grid=(1, 1),
        in_specs=[ctx_in] * 10,
        out_specs=[ctx_out, ctx_out, all_states],
        scratch_shapes=[pltpu.VMEM((2 * BATCH, DN_WIDTH, DN_WIDTH), F32)],
        compiler_params=_cparams(("arbitrary", "arbitrary")),
        name="dn_chunks_ctx",
    )(*args)

    n_x = SEQ // DN_XTILE
    spec_f = pl.BlockSpec((DN_XTILE, DN_WIDTH), lambda b, t: (b * n_x + t, 0))
    spec_b = pl.BlockSpec((DN_XTILE, DN_WIDTH), lambda b, t: (b * n_x + n_x - 1 - t, 0))
    o_fx, o_bx = pl.pallas_call(
        functools.partial(_dn_chunks_kernel, n_batch=1, n_chunk=DN_XTILE // DN_CHUNK, has_init=True),
        out_shape=[jax.ShapeDtypeStruct((ROWS_X, DN_WIDTH), F32)] * 2,
        grid=(BATCH, n_x),
        in_specs=[spec_f] * 5 + [spec_b] * 5 + [pl.BlockSpec((2, DN_WIDTH, DN_WIDTH), lambda b, t: (b, 0, 0))],
        out_specs=[spec_f, spec_b],
        scratch_shapes=[pltpu.VMEM((2, DN_WIDTH, DN_WIDTH), F32)],
        compiler_params=_cparams(("arbitrary", "arbitrary")),
        name="dn_chunks",
    )(*args, states)
    return o_fx, o_bx, o_fc, o_bc


def _dn_out_kernel(of_ref, ob_ref, gt_ref, nw_ref, hsum_ref, o_ref):
    o = of_ref[...] + ob_ref[...]
    ms = _exact_rows_dot(o * o, hsum_ref[...]) * (1.0 / DN_DV)
    gt = gt_ref[...]
    o_ref[...] = (o * lax.rsqrt(ms + EPS) * nw_ref[...] * (gt * jax.nn.sigmoid(gt))).astype(BF16)


def _dn_out(o_fx, o_bx, o_fc, o_bc, gt, norm_w):
    head_of_lane = jnp.arange(DN_WIDTH) // DN_DV
    hsum = (head_of_lane[:, None] == head_of_lane[None, :]).astype(BF16)
    nw = jnp.tile(norm_w, DN_HEADS)[None]
    consts = [pl.BlockSpec((1, DN_WIDTH), lambda i: (0, 0)), pl.BlockSpec((DN_WIDTH, DN_WIDTH), lambda i: (0, 0))]
    row = pl.BlockSpec((TG, DN_WIDTH), lambda i: (i, 0))

    def call(o_f, o_b, gt_spec, n_rows, name):
        return pl.pallas_call(
            _dn_out_kernel,
            out_shape=jax.ShapeDtypeStruct((n_rows, DN_WIDTH), BF16),
            grid=(n_rows // TG,),
            in_specs=[row, row, gt_spec] + consts,
            out_specs=row,
            compiler_params=_cparams(("arbitrary",)),
            name=name,
        )(o_f, o_b, gt, nw, hsum)

    dn_x = call(o_fx, o_bx, row, ROWS_X, "dn_out")
    dn_c = call(o_fc, o_bc, pl.BlockSpec((TG, DN_WIDTH), lambda i: (ROWS_X // TG + i, 0)), ROWS_C, "dn_out_ctx")
    return dn_x, dn_c


def _dn_mixer(qkv, gt, ba, conv_w, a_log, dt_bias, norm_w):
    qn, kn, v, beta_f, beta_b, gc_f, gc_b = _dn_prep(qkv, ba, conv_w, a_log, dt_bias)
    return _dn_out(*_dn_chunks(qn, kn, v, beta_f, beta_b, gc_f, gc_b), gt, norm_w)


def _pad_w_in(w):
    d = w.shape[0]
    s = [0]
    for n in IN_SIZES:
        s.append(s[-1] + n)
    u, cq, ckv, kr, qkv, gt, ba = (w[:, s[i]:s[i + 1]] for i in range(7))
    z = lambda n: jnp.zeros((d, n), w.dtype)
    return jnp.concatenate(
        [u, cq, ckv, z(KR_LANE), kr, z(LANES - KR_LANE - MLA_ROPE), qkv, gt, ba, z(LANES - 4 * DN_HEADS)], axis=1)


def _head_slots(w, width):
    kdim = w.shape[0]
    w = w.reshape(kdim, MLA_HEADS, width)
    return jnp.pad(w, ((0, 0), (0, 0), (0, HEAD_SLOT - width))).reshape(kdim, MLA_HEADS * HEAD_SLOT)


def _rope_swap(w):
    lane = jnp.arange(HEAD_SLOT)
    half = MLA_ROPE // 2
    src = jnp.where(lane < MLA_NOPE + half, lane + half, lane - half)
    in_rope = (lane >= MLA_NOPE) & (lane < MLA_DH)
    w3 = w.reshape(w.shape[0], -1, HEAD_SLOT)
    return jnp.where(in_rope, w3[:, :, jnp.clip(src, 0, HEAD_SLOT - 1)], jnp.zeros((), w.dtype)).reshape(w.shape)


def _rope_tables():
    rows = SEQ // GRID_W
    row = jnp.repeat(jnp.arange(rows, dtype=F32), GRID_W)
    col = jnp.tile(jnp.arange(GRID_W, dtype=F32), rows)
    inv = ROPE_BASE ** (-jnp.arange(ROPE_FREQS, dtype=F32) / ROPE_FREQS)
    ang = jnp.concatenate([row[:, None] * inv, col[:, None] * inv], axis=-1)
    cos, sin = jnp.cos(ang), jnp.sin(ang)
    ones = lambda n: jnp.ones((SEQ, n), F32)
    zeros = lambda n: jnp.zeros((SEQ, n), F32)
    cos_t = jnp.concatenate([ones(MLA_NOPE), cos, cos, ones(HEAD_SLOT - MLA_DH)], axis=1)
    sin_t = jnp.concatenate([zeros(MLA_NOPE), -sin, sin, zeros(HEAD_SLOT - MLA_DH)], axis=1)
    cos_t = jnp.concatenate([jnp.tile(cos_t, (BATCH, 1)), jnp.ones((ROWS_C, HEAD_SLOT), F32)], axis=0)
    sin_t = jnp.concatenate([jnp.tile(sin_t, (BATCH, 1)), jnp.zeros((ROWS_C, HEAD_SLOT), F32)], axis=0)
    return cos_t, sin_t


def kernel(x, c, ctx, c_ctx, w_ada, b_ada, norm1_w, w_in, s5_lambda_re, s5_lambda_im, s5_log_dt, s5_b_re, s5_b_im, s5_c_re, s5_c_im, s5_d, s5_glu_w, mla_q_norm_w, mla_kv_norm_w, mla_w_uq, mla_w_ukv, mla_q_gain, mla_k_gain, dn_conv_w, dn_a_log, dn_dt_bias, dn_norm_w, w_out, norm2_w, w_ff1, w_ff2):
    h = jnp.concatenate([x.reshape(ROWS_X, D_MODEL), ctx.reshape(ROWS_C, D_MODEL)], axis=0)
    c_pad = jnp.concatenate([c, c_ctx[None], jnp.zeros((3, D_MODEL), F32)], axis=0)
    mods_all = _ada_all(c_pad, w_ada, b_ada)
    cos_t, sin_t = _rope_tables()

    for l in range(DEPTH):
        ctx_out = l < DEPTH - 1
        mods = mods_all[l, :BATCH + 1].reshape(BATCH + 1, 6, 1, D_MODEL)
        wq = _head_slots(mla_w_uq[l], MLA_DH)
        w_ukv = mla_w_ukv[l].reshape(MLA_KV_RANK, MLA_HEADS, MLA_NOPE + MLA_V)
        wk = _head_slots(w_ukv[:, :, :MLA_NOPE].reshape(MLA_KV_RANK, MLA_HEADS * MLA_NOPE), MLA_NOPE).astype(BF16)
        wvt = w_ukv[:, :, MLA_NOPE:].reshape(MLA_KV_RANK, MLA_WIDTH).T.astype(BF16)
        qg, kg = (jnp.pad(g, (0, HEAD_SLOT - MLA_DH))[None] for g in (mla_q_gain[l], mla_k_gain[l]))
        mla_params = (mla_q_norm_w[l][None], mla_kv_norm_w[l][None], wq.astype(BF16), _rope_swap(wq).astype(BF16),
                      wk, wvt, qg, _rope_swap(qg), kg, _rope_swap(kg), _rope_swap(jnp.eye(HEAD_SLOT, dtype=BF16)))
        u, qkv, gt, ba, q, k4, vt4 = _in_proj(h, norm1_w[l][None], mods, _pad_w_in(w_in[l]).astype(BF16),
                                              cos_t, sin_t, mla_params)

        s5 = _s5_mixer(u, s5_lambda_re[l], s5_lambda_im[l], s5_log_dt[l], s5_b_re[l], s5_b_im[l],
                       s5_c_re[l], s5_c_im[l], s5_d[l], s5_glu_w[l])
        ot = _flash(q, k4, vt4, ctx_out)
        dn = _dn_mixer(qkv, gt, ba, dn_conv_w[l], dn_a_log[l], dn_dt_bias[l], dn_norm_w[l])

        n_rows = ROWS if ctx_out else ROWS_X
        h = _out_mlp(h, s5, ot, dn, w_out[l].astype(BF16), mods, norm2_w[l][None], w_ff1[l].astype(BF16), w_ff2[l].astype(BF16), n_rows)
    return h.reshape(BATCH, SEQ, D_MODEL)
```
